```python
import jax, jax.numpy as jnp
from jax import lax
import numpy as np

D_MODEL = 1024
BATCH = 32
SEQ = 256
DEPTH = 2
DEC_BATCH = 2
DEC_SEQ = 4096
PAST_LEN = 512

GRID_W = 64
A_HEAD = 64
A_WIDTH = D_MODEL
A_HEADS = A_WIDTH // A_HEAD
A_DECAY_RANK = 64
A_ICLR_RANK = 64
A_GATE_RANK = 128
A_COLS = 3 * A_WIDTH + 2 * A_DECAY_RANK + 2 * A_ICLR_RANK + A_GATE_RANK
POOL_WINDOWS = (2, 4, 8, 16)
B_GROUPS = len(POOL_WINDOWS)
B_WIDTH = D_MODEL // 2
B_GROUP = B_WIDTH // B_GROUPS
C_WIDTH = D_MODEL // 2
C_CONV = 31
IN_COLS = A_COLS + B_WIDTH + 2 * C_WIDTH + 3 * D_MODEL
P_HEADS = 8
P_KEYS = 128
P_EXPERTS = P_KEYS * P_KEYS
P_QDIM = 256
P_HALF = P_QDIM // 2
P_TOPK = 16
P_CHUNK = 128
EPS = 1e-6
GN_EPS = 64e-5
LN_EPS = 1e-5

kernel_name = 'hybrid_rwkv7_pool_conformer_peer_dit'


def rmsnorm(x, g):
    xf = x.astype(jnp.float32)
    y = xf * lax.rsqrt(jnp.mean(xf * xf, axis=-1, keepdims=True) + EPS)
    return (y * g.astype(jnp.float32)).astype(x.dtype)


def layernorm(x, g, b, eps):
    xf = x.astype(jnp.float32)
    mu = jnp.mean(xf, axis=-1, keepdims=True)
    var = jnp.mean(jnp.square(xf - mu), axis=-1, keepdims=True)
    y = (xf - mu) * lax.rsqrt(var + eps) * g.astype(jnp.float32) + b.astype(jnp.float32)
    return y.astype(x.dtype)


def centred_shift(x):
    prev = jnp.pad(x[:, :-1], ((0, 0), (1, 0), (0, 0)))
    nxt = jnp.pad(x[:, 1:], ((0, 0), (0, 1), (0, 0)))
    return 0.5 * (prev + nxt) - x


def rwkv_mix(pa, s0, mu, w0, w2, a0, a2, g2, k_k, k_a, r_k, gn_g, gn_b):
    Bn, T, _ = pa.shape
    dt = pa.dtype
    xa = pa + centred_shift(pa) * mu
    i1 = A_WIDTH
    i2 = 2 * A_WIDTH
    i3 = 3 * A_WIDTH
    i4 = i3 + 2 * A_DECAY_RANK
    i5 = i4 + 2 * A_ICLR_RANK
    r, k, v, w_lo, a_lo, g_lo = jnp.split(xa, [i1, i2, i3, i4, i5], axis=-1)
    w_lo = w_lo.reshape(Bn, T, 2, A_DECAY_RANK)
    a_lo = a_lo.reshape(Bn, T, 2, A_ICLR_RANK)
    w = -jax.nn.softplus(-(w0 + jnp.einsum('btdr,drc->btdc', jnp.tanh(w_lo), w2))) - 0.5
    decay = jnp.exp(-jnp.exp(w.astype(jnp.float32))).astype(dt)
    a = jax.nn.sigmoid(a0 + jnp.einsum('btdr,drc->btdc', a_lo, a2))
    g = jax.nn.sigmoid(g_lo) @ g2
    hs = (A_HEADS, A_HEAD)
    kk = (k * k_k).reshape(Bn, T, *hs).astype(jnp.float32)
    kk = (kk / jnp.maximum(jnp.linalg.norm(kk, axis=-1, keepdims=True), 1e-12)).astype(dt)
    k_dir = (k[:, :, None, :] * (1 + (a - 1) * k_a)).reshape(Bn, T, 2, *hs)
    a_h = a.reshape(Bn, T, 2, *hs)
    decay_h = decay.reshape(Bn, T, 2, *hs)
    r_h = r.reshape(Bn, T, *hs)
    v_h = v.reshape(Bn, T, *hs)

    def both(z):
        return jnp.stack([z, jnp.flip(z, axis=1)], axis=2)

    def orient(z):
        return jnp.stack([z[:, :, 0], jnp.flip(z[:, :, 1], axis=1)], axis=2)

    tm = lambda z: jnp.moveaxis(z, 1, 0)
    seqs = (tm(both(r_h)), tm(orient(decay_h)), tm(orient(k_dir)), tm(both(v_h)), tm(both(kk)), tm(orient(a_h)))

    def step(S, inp):
        r_t, w_t, k_t, v_t, kk_t, a_t = inp
        sa = jnp.einsum('bdhvk,bdhk->bdhv', S, -kk_t)
        S = S * w_t[..., None, :] + sa[..., :, None] * (kk_t * a_t)[..., None, :] + v_t[..., :, None] * k_t[..., None, :]
        y = jnp.einsum('bdhvk,bdhk->bdhv', S, r_t)
        return S, y

    s_fin, ys = lax.scan(step, s0.astype(dt), seqs)
    ys = jnp.moveaxis(ys, 0, 1)
    y = ys[:, :, 0] + jnp.flip(ys[:, :, 1], axis=1)
    y = layernorm(y, gn_g.reshape(hs), gn_b.reshape(hs), GN_EPS)
    bonus = jnp.einsum('bthk,btdhk,hk->bth', r_h, k_dir, r_k)[..., None] * v_h
    return (y + bonus).reshape(Bn, T, A_WIDTH) * g, s_fin


def centred_mean(x, axis, window):
    L = x.shape[axis]
    cs = jnp.cumsum(x.astype(jnp.float32), axis=axis)
    cs = jnp.concatenate([jnp.zeros_like(lax.slice_in_dim(cs, 0, 1, axis=axis)), cs], axis=axis)
    idx = jnp.arange(L)
    lo = jnp.clip(idx - window // 2, 0, L)
    hi = jnp.clip(idx - window // 2 + window, 0, L)
    s = jnp.take(cs, hi, axis=axis) - jnp.take(cs, lo, axis=axis)
    shape = [1] * x.ndim
    shape[axis] = L
    cnt = (hi - lo).astype(jnp.float32).reshape(shape)
    return (s / cnt).astype(x.dtype)


def pool_mix(pb, grid, pool_w, pool_b, pool_scale):
    Bn, T, C = pb.shape
    z = pb.reshape(Bn, T // GRID_W, GRID_W, C) if grid else pb
    groups = jnp.split(z, B_GROUPS, axis=-1)
    d = jnp.stack([centred_mean(zg, 1, w) - zg for zg, w in zip(groups, POOL_WINDOWS)], axis=-2)
    y = jnp.einsum('...gc,gcd->...gd', d, pool_w) + pool_b
    return (y * pool_scale.reshape(B_GROUPS, B_GROUP)).reshape(Bn, T, C)


def conv_mix(pc, grid, conv_w, conv_b, ln_g, ln_b):
    u_a, u_g = jnp.split(pc, 2, axis=-1)
    u = u_a * jax.nn.sigmoid(u_g)
    Bn, T, C = u.shape
    if grid:
        u = u.reshape(Bn * (T // GRID_W), GRID_W, C)
    u = lax.conv_general_dilated(u, conv_w[:, None, :], window_strides=(1,), padding='SAME',
                                 dimension_numbers=('NWC', 'WIO', 'NWC'), feature_group_count=C)
    u = (u + conv_b).reshape(Bn, T, C)
    return jax.nn.silu(layernorm(u, ln_g, ln_b, LN_EPS))


def peer(h, w_q, sub_keys, u_tab, v_tab):
    Bn, T, D = h.shape

    def block(xc):
        q = (xc @ w_q).reshape(-1, P_HEADS, 2, P_HALF)
        s = jnp.einsum('chpq,hpnq->chpn', q, sub_keys)
        s_top, i_top = lax.top_k(s, P_TOPK)
        cand = s_top[:, :, 0, :, None] + s_top[:, :, 1, None, :]
        cand = cand.reshape(cand.shape[0], P_HEADS, P_TOPK * P_TOPK)
        best, j = lax.top_k(cand, P_TOPK)
        e = (jnp.take_along_axis(i_top[:, :, 0], j // P_TOPK, axis=-1) * P_KEYS
             + jnp.take_along_axis(i_top[:, :, 1], j % P_TOPK, axis=-1))
        gate = jax.nn.softmax(best.astype(jnp.float32), axis=-1).astype(xc.dtype)
        u = jnp.take(u_tab, e, axis=0)
        hid = jax.nn.gelu(jnp.einsum('cd,chkd->chk', xc, u), approximate=False)
        vsel = jnp.take(v_tab, e, axis=0)
        return jnp.einsum('chk,chkd->cd', gate * hid, vsel)

    out = lax.map(block, h.reshape(-1, P_CHUNK, D))
    return out.reshape(Bn, T, D)


def setup_inputs(seed: int = 0) -> dict:
    key = jax.random.key(seed)
    ks = iter(jax.random.split(key, 48))
    f32 = jnp.float32
    nrm = lambda shape, s: jax.random.normal(next(ks), shape, f32) * s
    gain = lambda shape: 1.0 + nrm(shape, 0.02)
    L = DEPTH
    return {
        'x_prompt': nrm((BATCH, SEQ, D_MODEL), 1.0),
        'x_sample': nrm((DEC_BATCH, DEC_SEQ, D_MODEL), 1.0),
        'state_rwkv': nrm((DEC_BATCH, DEPTH, 2, A_HEADS, A_HEAD, A_HEAD), 0.5),
        'c': nrm((DEC_BATCH, D_MODEL), 1.0),
        'c_ctx': nrm((D_MODEL,), 1.0),
        'ada_w': nrm((L, D_MODEL, 6 * D_MODEL), 0.5 * D_MODEL ** -0.5),
        'ada_b': nrm((L, 6 * D_MODEL), 0.02),
        'norm1_g': gain((L, D_MODEL)),
        'norm2_g': gain((L, D_MODEL)),
        'w_in': nrm((L, D_MODEL, IN_COLS), D_MODEL ** -0.5),
        'shift_mu': jax.random.uniform(next(ks), (L, A_COLS), f32),
        'decay_w0': jax.random.uniform(next(ks), (L, 2, A_WIDTH), f32, minval=-4.0, maxval=0.0),
        'decay_w2': nrm((L, 2, A_DECAY_RANK, A_WIDTH), 0.5 * A_DECAY_RANK ** -0.5),
        'iclr_a0': nrm((L, 2, A_WIDTH), 0.5),
        'iclr_a2': nrm((L, 2, A_ICLR_RANK, A_WIDTH), 0.5 * A_ICLR_RANK ** -0.5),
        'gate_g2': nrm((L, A_GATE_RANK, A_WIDTH), A_GATE_RANK ** -0.5),
        'k_k': 0.85 + nrm((L, A_WIDTH), 0.05),
        'k_a': 1.0 + nrm((L, A_WIDTH), 0.05),
        'r_k': nrm((L, A_HEADS, A_HEAD), 0.1),
        'gn_g': gain((L, A_WIDTH)),
        'gn_b': nrm((L, A_WIDTH), 0.02),
        'w_a_out': nrm((L, A_WIDTH, D_MODEL), A_WIDTH ** -0.5),
        'pool_w': nrm((L, B_GROUPS, B_GROUP, B_GROUP), B_GROUP ** -0.5),
        'pool_b': nrm((L, B_GROUPS, B_GROUP), 0.02),
        'pool_scale': 1.0 + nrm((L, B_WIDTH), 0.1),
        'w_b_out': nrm((L, B_WIDTH, D_MODEL), B_WIDTH ** -0.5),
        'conv_w': nrm((L, C_CONV, C_WIDTH), C_CONV ** -0.5),
        'conv_b': nrm((L, C_WIDTH), 0.02),
        'cln_g': gain((L, C_WIDTH)),
        'cln_b': nrm((L, C_WIDTH), 0.02),
        'w_c_out': nrm((L, C_WIDTH, D_MODEL), C_WIDTH ** -0.5),
        'w_out': nrm((L, D_MODEL, D_MODEL), D_MODEL ** -0.5),
        'peer_wq': nrm((L, D_MODEL, P_HEADS * P_QDIM), D_MODEL ** -0.5),
        'peer_keys': nrm((L, P_HEADS, 2, P_KEYS, P_HALF), P_HALF ** -0.5),
        'peer_u': nrm((L, P_EXPERTS, D_MODEL), D_MODEL ** -0.5),
        'peer_v': nrm((L, P_EXPERTS, D_MODEL), P_HEADS ** -0.5),
        'final_g': gain((D_MODEL,)),
    }


def reference(x_prompt, x_sample, state_rwkv, c, c_ctx, ada_w, ada_b, norm1_g, norm2_g, w_in, shift_mu,
              decay_w0, decay_w2, iclr_a0, iclr_a2, gate_g2, k_k, k_a, r_k, gn_g, gn_b, w_a_out,
              pool_w, pool_b, pool_scale, w_b_out, conv_w, conv_b, cln_g, cln_b, w_c_out, w_out,
              peer_wq, peer_keys, peer_u, peer_v, final_g):
    def layer(x, cond, grid, s0, l):
        mod = jax.nn.silu(cond) @ ada_w[l] + ada_b[l]
        sh1, sc1, gt1, sh2, sc2, gt2 = jnp.split(mod[:, None, :], 6, axis=-1)
        h = rmsnorm(x, norm1_g[l]) * (1 + sc1) + sh1
        proj = h @ w_in[l]
        pa, pb, pc, pg = jnp.split(proj, [A_COLS, A_COLS + B_WIDTH, A_COLS + B_WIDTH + 2 * C_WIDTH], axis=-1)
        ya, s_fin = rwkv_mix(pa, s0, shift_mu[l], decay_w0[l], decay_w2[l], iclr_a0[l], iclr_a2[l], gate_g2[l],
                             k_k[l], k_a[l], r_k[l], gn_g[l], gn_b[l])
        yb = pool_mix(pb, grid, pool_w[l], pool_b[l], pool_scale[l])
        yc = conv_mix(pc, grid, conv_w[l], conv_b[l], cln_g[l], cln_b[l])
        ga, gb, gc = jnp.split(jax.nn.sigmoid(pg), 3, axis=-1)
        merged = ga * (ya @ w_a_out[l]) + gb * (yb @ w_b_out[l]) + gc * (yc @ w_c_out[l])
        x = x + gt1 * (merged @ w_out[l])
        h2 = rmsnorm(x, norm2_g[l]) * (1 + sc2) + sh2
        x = x + gt2 * peer(h2, peer_wq[l], peer_keys[l], peer_u[l], peer_v[l])
        return x, s_fin

    xp = x_prompt
    s_zero = jnp.zeros((x_prompt.shape[0], 2, A_HEADS, A_HEAD, A_HEAD), x_prompt.dtype)
    ctx_states = []
    for l in range(DEPTH):
        xp, s_ctx = layer(xp, c_ctx[None, :], False, s_zero, l)
        ctx_states.append(s_ctx)
    new_state_rwkv = jnp.stack(ctx_states, axis=1)

    xs = x_sample
    for l in range(DEPTH):
        xs, _ = layer(xs, c, True, state_rwkv[:, l], l)

    return (rmsnorm(xp, final_g), rmsnorm(xs, final_g), new_state_rwkv)
```

```python
import functools

import jax
import jax.numpy as jnp
from jax import lax
from jax.experimental import pallas as pl
from jax.experimental.pallas import tpu as pltpu

F32 = jnp.float32
BF16 = jnp.bfloat16

D_MODEL = 1024
A_HEADS = 16
A_HEAD = 64
A_LORA = 384
A_COLS = 3 * D_MODEL + A_LORA
B_WIDTH = 512
B_GROUP = 128
POOL_WINDOWS = (2, 4, 8, 16)
POOL_PAD = 16
C_WIDTH = 512
C_CONV = 31
CONV_PAD = 16
GRID_W = 64
P_HEADS = 8
P_KEYS = 128
P_TOPK = 16
P_HALF = 128
P_EXPERTS = P_KEYS * P_KEYS
EPS = 1e-6
GN_EPS = 64e-5
LN_EPS = 1e-5

LANES = 128
VMEM_LIMIT = 56 * 1024 * 1024

TM_PROJ = 512
TM_PEER = 512
PEER_EB = 1024
SCAN_TB = 64


def _cparams(*sem):
    return pltpu.CompilerParams(dimension_semantics=sem, vmem_limit_bytes=VMEM_LIMIT)


def _sigmoid(x):
    return 1.0 / (1.0 + jnp.exp(-x))


def _dot(a, b):
    return jnp.dot(a, b, preferred_element_type=F32)


def _segsum(x, ones_bd):
    hi = x.astype(BF16)
    lo = (x - hi.astype(F32)).astype(BF16)
    return _dot(hi, ones_bd) + _dot(lo, ones_bd)


def _mod_kernel(c_ref, w_ref, b_ref, o_ref):
    c = c_ref[...]
    s = c * _sigmoid(c)
    o_ref[...] = jnp.dot(s, w_ref[...], preferred_element_type=F32,
                         precision=lax.Precision.HIGHEST) + b_ref[...]


def _modulation(cond8, ada_w, ada_b):
    L = ada_w.shape[0]
    nb = 6
    return pl.pallas_call(
        _mod_kernel,
        grid=(L, nb),
        in_specs=[pl.BlockSpec((8, D_MODEL), lambda l, j: (0, 0)),
                  pl.BlockSpec((None, D_MODEL, D_MODEL), lambda l, j: (l, 0, j)),
                  pl.BlockSpec((None, 1, D_MODEL), lambda l, j: (l, 0, j))],
        out_specs=pl.BlockSpec((None, 8, D_MODEL), lambda l, j: (l, 0, j)),
        out_shape=jax.ShapeDtypeStruct((L, 8, 6 * D_MODEL), F32),
        compiler_params=_cparams("parallel", "parallel"),
        name="adaln_mod",
    )(cond8, ada_w, ada_b.reshape(L, 1, 6 * D_MODEL))


def _mod_row_map(n_ctx_tiles, tiles_per_dec_seq):
    def index_map(i):
        row = jnp.where(i < n_ctx_tiles, 0, 1 + (i - n_ctx_tiles) // tiles_per_dec_seq)
        return (row, 0, 0)
    return index_map


def _inproj_kernel(x_ref, mod_ref, g_ref, w_ref, *o_refs):
    x = x_ref[...]
    ms = jnp.mean(x * x, axis=-1, keepdims=True)
    y = x * lax.rsqrt(ms + EPS) * g_ref[...]
    sh = mod_ref[0, :, 0:D_MODEL]
    sc = mod_ref[0, :, D_MODEL:2 * D_MODEL]
    h = (y * (1.0 + sc) + sh).astype(BF16)
    off = 0
    for o_ref in o_refs:
        n = o_ref.shape[1]
        o_ref[...] = _dot(h, w_ref[:, off:off + n])
        off += n


def _inproj(x, mod3, g, w, splits, mod_map, name):
    n_tok = x.shape[0]
    n_out = w.shape[1]
    return pl.pallas_call(
        _inproj_kernel,
        grid=(n_tok // TM_PROJ,),
        in_specs=[pl.BlockSpec((TM_PROJ, D_MODEL), lambda i: (i, 0)),
                  pl.BlockSpec((1, 1, 6 * D_MODEL), mod_map),
                  pl.BlockSpec((1, D_MODEL), lambda i: (0, 0)),
                  pl.BlockSpec((D_MODEL, n_out), lambda i: (0, 0))],
        out_specs=[pl.BlockSpec((TM_PROJ, n), lambda i: (i, 0)) for n in splits],
        out_shape=[jax.ShapeDtypeStruct((n_tok, n), F32) for n in splits],
        compiler_params=_cparams("parallel"),
        name=name,
    )(x, mod3, g, w)


def _prep_kernel(pa_ref, hp_ref, hn_ref, mu_ref, w0_ref, w2_ref, a0_ref, a2_ref, g2_ref,
                 kkw_ref, ka_ref, rk_ref, ones_ref,
                 r_o, v_o, kk_o, dec_o, kd_o, bb_o, bv_o, g_o):
    tm = pa_ref.shape[0]

    def shifted(c0, c1):
        x = pa_ref[:, c0:c1]
        row = lax.broadcasted_iota(jnp.int32, x.shape, 0)
        prev = jnp.where(row == 0, hp_ref[0, :, c0:c1], pltpu.roll(x, 1, 0))
        nxt = jnp.where(row == tm - 1, hn_ref[0, :, c0:c1], pltpu.roll(x, tm - 1, 0))
        return x + (0.5 * (prev + nxt) - x) * mu_ref[:, c0:c1]

    d = D_MODEL
    r = shifted(0, d)
    k = shifted(d, 2 * d)
    v = shifted(2 * d, 3 * d)
    lo = shifted(3 * d, 3 * d + A_LORA)
    w_lo = lo[:, 0:128]
    a_lo = lo[:, 128:256]
    g_lo = lo[:, 256:384]
    ones_bd = ones_ref[...]

    r_o[...] = r
    v_o[...] = v

    kkraw = k * kkw_ref[...]
    nrm = jnp.sqrt(_segsum(kkraw * kkraw, ones_bd))
    kk = kkraw / jnp.maximum(nrm, 1e-12)
    kk_o[...] = kk

    z = -(w0_ref[...] + _dot(jnp.tanh(w_lo).astype(BF16), w2_ref[...]))
    softplus = jnp.maximum(z, 0.0) + jnp.log(1.0 + jnp.exp(-jnp.abs(z)))
    dec_o[...] = jnp.exp(-jnp.exp(-softplus - 0.5))

    a = _sigmoid(a0_ref[...] + _dot(a_lo.astype(BF16), a2_ref[...]))
    ka = ka_ref[...]
    kd0 = k * (1.0 + (a[:, 0:d] - 1.0) * ka)
    kd1 = k * (1.0 + (a[:, d:2 * d] - 1.0) * ka)
    kd_o[:, 0:d] = kd0
    kd_o[:, d:2 * d] = kd1
    bb_o[:, 0:d] = kk * a[:, 0:d]
    bb_o[:, d:2 * d] = kk * a[:, d:2 * d]

    bonus = _segsum(r * (kd0 + kd1) * rk_ref[...], ones_bd)
    bv_o[...] = bonus * v
    g_o[...] = _dot(_sigmoid(g_lo).astype(BF16), g2_ref[...])


def _rwkv_prep(pa, halo_prev, halo_next, tm, params):
    n_tok = pa.shape[0]
    d = D_MODEL
    row = lambda n: pl.BlockSpec((1, n), lambda i: (0, 0))
    full = lambda a, b: pl.BlockSpec((a, b), lambda i: (0, 0))
    tile = lambda n: pl.BlockSpec((tm, n), lambda i: (i, 0))
    halo = pl.BlockSpec((1, 1, A_COLS), lambda i: (i, 0, 0))
    outs = [d, d, d, 2 * d, 2 * d, 2 * d, d, d]
    return pl.pallas_call(
        _prep_kernel,
        grid=(n_tok // tm,),
        in_specs=[tile(A_COLS), halo, halo, row(A_COLS), row(2 * d), full(128, 2 * d), row(2 * d),
                  full(128, 2 * d), full(128, d), row(d), row(d), row(d), full(d, d)],
        out_specs=[tile(n) for n in outs],
        out_shape=[jax.ShapeDtypeStruct((n_tok, n), F32) for n in outs],
        compiler_params=_cparams("parallel"),
        name="rwkv_prep",
    )(pa, halo_prev, halo_next, *params)


def _scan_kernel(kk_ref, w_ref, b_ref, kd_ref, r_ref, v_ref, s0_ref, y_ref, sfin_ref, s_scr):
    tb = kk_ref.shape[0]
    nk = kk_ref.shape[1]
    j = pl.program_id(1)

    @pl.when(j == 0)
    def _():
        s_scr[...] = s0_ref[...]

    def step(t, carry):
        parts = [None] * 4
        for k in range(nk):
            term = s_scr[k] * kk_ref[t, pl.ds(k, 1), :]
            parts[k % 4] = term if parts[k % 4] is None else parts[k % 4] + term
        sa = -((parts[0] + parts[1]) + (parts[2] + parts[3]))
        vt = v_ref[t]
        ys = [None] * 2
        for k in range(nk):
            s_new = (s_scr[k] * w_ref[t, pl.ds(k, 1), :] + sa * b_ref[t, pl.ds(k, 1), :]
                     + vt * kd_ref[t, pl.ds(k, 1), :])
            s_scr[k] = s_new
            term = s_new * r_ref[t, pl.ds(k, 1), :]
            ys[k % 2] = term if ys[k % 2] is None else ys[k % 2] + term
        y_ref[t] = ys[0] + ys[1]
        return carry

    lax.fori_loop(0, tb, step, 0)

    @pl.when(j == pl.num_programs(1) - 1)
    def _():
        sfin_ref[...] = s_scr[...]


def _rwkv_scan(kk_t, w_t, b_t, kd_t, r_t, v_t, s0_t):
    T, nk, L = kk_t.shape
    vr = v_t.shape[1]
    tb = min(SCAN_TB, T)
    kspec = pl.BlockSpec((tb, nk, LANES), lambda g, j: (j, 0, g))
    vspec = pl.BlockSpec((tb, vr, LANES), lambda g, j: (j, 0, g))
    sspec = pl.BlockSpec((nk, vr, LANES), lambda g, j: (0, 0, g))
    return pl.pallas_call(
        _scan_kernel,
        grid=(L // LANES, T // tb),
        in_specs=[kspec, kspec, kspec, kspec, kspec, vspec, sspec],
        out_specs=[vspec, sspec],
        out_shape=[jax.ShapeDtypeStruct((T, vr, L), F32), jax.ShapeDtypeStruct((nk, vr, L), F32)],
        scratch_shapes=[pltpu.VMEM((nk, vr, LANES), F32)],
        compiler_params=_cparams("parallel", "arbitrary"),
        name="rwkv_scan",
    )(kk_t, w_t, b_t, kd_t, r_t, v_t, s0_t)


def _value_split(n_batch):
    chains = 2 * n_batch * A_HEADS
    if chains % LANES == 0:
        return 1
    assert LANES % chains == 0 and A_HEAD % (LANES // chains) == 0
    return LANES // chains


def _to_chain_layout(shared, per_dir, s0, B, T, vs):
    H, K = A_HEADS, A_HEAD
    chains = 2 * B * H

    def lanes_k(x2):
        x = jnp.transpose(x2, (2, 4, 0, 1, 3)).reshape(T, K, chains)
        return jnp.repeat(x, vs, axis=-1) if vs > 1 else x

    def both(x):
        x = x.reshape(B, T, H, K)
        return jnp.stack([x, jnp.flip(x, axis=1)], axis=0)

    def orient(x):
        x = x.reshape(B, T, 2, H, K)
        return jnp.stack([x[:, :, 0], jnp.flip(x[:, :, 1], axis=1)], axis=0)

    r, v, kk = shared
    dec, kd, bb = per_dir
    kk_t, r_t = lanes_k(both(kk)), lanes_k(both(r))
    w_t, kd_t, b_t = lanes_k(orient(dec)), lanes_k(orient(kd)), lanes_k(orient(bb))
    vq = K // vs
    v_t = jnp.transpose(both(v), (2, 4, 0, 1, 3)).reshape(T, vs, vq, chains)
    v_t = jnp.transpose(v_t, (0, 2, 3, 1)).reshape(T, vq, chains * vs)
    s0_t = jnp.transpose(s0, (4, 3, 1, 0, 2)).reshape(K, vs, vq, chains)
    s0_t = jnp.transpose(s0_t, (0, 2, 3, 1)).reshape(K, vq, chains * vs)
    return kk_t, w_t, b_t, kd_t, r_t, v_t, s0_t


def _from_chain_layout(y_t, sfin_t, B, T, vs):
    H, K = A_HEADS, A_HEAD
    vq = K // vs
    y = y_t.reshape(T, vq, 2, B, H, vs)
    y = jnp.transpose(y, (2, 3, 0, 4, 5, 1)).reshape(2, B, T, H * K)
    y = (y[0] + jnp.flip(y[1], axis=1)).reshape(B * T, H * K)
    s = sfin_t.reshape(K, vq, 2, B, H, vs)
    s = jnp.transpose(s, (3, 2, 4, 5, 1, 0)).reshape(B, 2, H, K, K)
    return y, s


def _pool_tail(d2, g, pw_ref, pb_ref, ps_ref):
    sl = slice(g * B_GROUP, (g + 1) * B_GROUP)
    y = _dot(d2.astype(BF16), pw_ref[g]) + pb_ref[:, sl]
    return y * ps_ref[:, sl]


def _window_count(t, w, n):
    lo = jnp.clip(t - w // 2, 0, n)
    hi = jnp.clip(t - w // 2 + w, 0, n)
    return (hi - lo).astype(F32)


def _pool_seq_kernel(z_ref, pw_ref, pb_ref, ps_ref, o_ref):
    n = z_ref.shape[0]
    pad = jnp.zeros((POOL_PAD, B_GROUP), F32)
    t = lax.broadcasted_iota(jnp.int32, (n, B_GROUP), 0)
    for g, w in enumerate(POOL_WINDOWS):
        sl = slice(g * B_GROUP, (g + 1) * B_GROUP)
        z = z_ref[:, sl]
        zp = jnp.concatenate([pad, z, pad], axis=0)
        acc = None
        for j in range(w):
            off = j - w // 2
            term = zp if off == 0 else pltpu.roll(zp, (-off) % (n + 2 * POOL_PAD), 0)
            acc = term if acc is None else acc + term
        d = acc[POOL_PAD:POOL_PAD + n] / _window_count(t, w, n) - z
        o_ref[:, sl] = _pool_tail(d, g, pw_ref, pb_ref, ps_ref)


def _pool_grid_kernel(z_ref, pw_ref, pb_ref, ps_ref, o_ref):
    rows, cb = z_ref.shape[0], z_ref.shape[1]
    pad = jnp.zeros((POOL_PAD, cb, B_GROUP), F32)
    t = lax.broadcasted_iota(jnp.int32, (rows, cb, B_GROUP), 0)
    for g, w in enumerate(POOL_WINDOWS):
        sl = slice(g * B_GROUP, (g + 1) * B_GROUP)
        z = z_ref[:, :, sl]
        zp = jnp.concatenate([pad, z, pad], axis=0)
        acc = None
        for j in range(w):
            s = POOL_PAD + j - w // 2
            term = zp[s:s + rows]
            acc = term if acc is None else acc + term
        d = acc / _window_count(t, w, rows) - z
        y = _pool_tail(d.reshape(rows * cb, B_GROUP), g, pw_ref, pb_ref, ps_ref)
        o_ref[:, :, sl] = y.reshape(rows, cb, B_GROUP)


def _pool_mix(pb, n_ctx, seq, dec_seq, pw, pbias, pscale):
    n_tok = pb.shape[0]
    wspecs = lambda nd: [pl.BlockSpec((4, B_GROUP, B_GROUP), lambda *a: (0, 0, 0)),
                         pl.BlockSpec((1, B_WIDTH), lambda *a: (0, 0)),
                         pl.BlockSpec((1, B_WIDTH), lambda *a: (0, 0))]
    y_ctx = pl.pallas_call(
        _pool_seq_kernel,
        grid=(n_ctx // seq,),
        in_specs=[pl.BlockSpec((seq, B_WIDTH), lambda i: (i, 0))] + wspecs(1),
        out_specs=pl.BlockSpec((seq, B_WIDTH), lambda i: (i, 0)),
        out_shape=jax.ShapeDtypeStruct((n_ctx, B_WIDTH), F32),
        compiler_params=_cparams("parallel"),
        name="pool_seq",
    )(pb, pw, pbias, pscale)
    rows = dec_seq // GRID_W
    n_dec = (n_tok - n_ctx) // dec_seq
    ctx_blocks = n_ctx // dec_seq
    cb = 8
    pb4 = pb.reshape(n_tok // dec_seq, rows, GRID_W, B_WIDTH)
    y_lat = pl.pallas_call(
        _pool_grid_kernel,
        grid=(n_dec, GRID_W // cb),
        in_specs=[pl.BlockSpec((None, rows, cb, B_WIDTH), lambda b, c: (b + ctx_blocks, 0, c, 0))]
        + wspecs(2),
        out_specs=pl.BlockSpec((None, rows, cb, B_WIDTH), lambda b, c: (b, 0, c, 0)),
        out_shape=jax.ShapeDtypeStruct((n_dec, rows, GRID_W, B_WIDTH), F32),
        compiler_params=_cparams("parallel", "parallel"),
        name="pool_grid",
    )(pb4, pw, pbias, pscale)
    return jnp.concatenate([y_ctx, y_lat.reshape(n_tok - n_ctx, B_WIDTH)], axis=0)


def _conv_kernel(pc_ref, cw_ref, cb_ref, lg_ref, lb_ref, o_ref, *, n_ctx_tiles):
    tm = pc_ref.shape[0]
    u = pc_ref[:, 0:C_WIDTH] * _sigmoid(pc_ref[:, C_WIDTH:2 * C_WIDTH])
    pad = jnp.zeros((CONV_PAD, C_WIDTH), F32)
    up = jnp.concatenate([pad, u, pad], axis=0)
    is_ctx = pl.program_id(0) < n_ctx_tiles
    t = lax.broadcasted_iota(jnp.int32, (tm, C_WIDTH), 0)
    pos = jnp.where(is_ctx, t, t & (GRID_W - 1))
    seg = jnp.where(is_ctx, tm, GRID_W)
    acc = jnp.zeros((tm, C_WIDTH), F32)
    for j in range(C_CONV):
        off = j - C_CONV // 2
        rolled = up if off == 0 else pltpu.roll(up, (-off) % (tm + 2 * CONV_PAD), 0)
        tap = rolled[CONV_PAD:CONV_PAD + tm]
        q = pos + off
        tap = jnp.where(q >= 0, jnp.where(q < seg, tap, 0.0), 0.0)
        acc = acc + tap * cw_ref[pl.ds(j, 1), :]
    c = acc + cb_ref[...]
    mu = jnp.mean(c, axis=-1, keepdims=True)
    cc = c - mu
    var = jnp.mean(cc * cc, axis=-1, keepdims=True)
    y = cc * lax.rsqrt(var + LN_EPS) * lg_ref[...] + lb_ref[...]
    o_ref[...] = y * _sigmoid(y)


def _conv_mix(pc, tm, n_ctx_tiles, cw, cb, lg, lb):
    n_tok = pc.shape[0]
    row = pl.BlockSpec((1, C_WIDTH), lambda i: (0, 0))
    return pl.pallas_call(
        functools.partial(_conv_kernel, n_ctx_tiles=n_ctx_tiles),
        grid=(n_tok // tm,),
        in_specs=[pl.BlockSpec((tm, 2 * C_WIDTH), lambda i: (i, 0)),
                  pl.BlockSpec((C_CONV, C_WIDTH), lambda i: (0, 0)), row, row, row],
        out_specs=pl.BlockSpec((tm, C_WIDTH), lambda i: (i, 0)),
        out_shape=jax.ShapeDtypeStruct((n_tok, C_WIDTH), F32),
        compiler_params=_cparams("parallel"),
        name="conv_mix",
    )(pc, cw, cb, lg, lb)


def _merge_kernel(y_ref, bv_ref, g_ref, yb_ref, yc_ref, pg_ref, x_ref, mod_ref, gng_ref, gnb_ref,
                  n2g_ref, wa_ref, wb_ref, wc_ref, wo_ref, ones_ref, xo_ref, h2_ref):
    d = D_MODEL
    ones_bd = ones_ref[...]
    y = y_ref[...]
    mean = _segsum(y, ones_bd) * (1.0 / A_HEAD)
    yc = y - mean
    var = _segsum(yc * yc, ones_bd) * (1.0 / A_HEAD)
    yn = yc * lax.rsqrt(var + GN_EPS) * gng_ref[...] + gnb_ref[...]
    ya = (yn + bv_ref[...]) * g_ref[...]
    merged = (_sigmoid(pg_ref[:, 0:d]) * _dot(ya.astype(BF16), wa_ref[...])
              + _sigmoid(pg_ref[:, d:2 * d]) * _dot(yb_ref[...].astype(BF16), wb_ref[...])
              + _sigmoid(pg_ref[:, 2 * d:3 * d]) * _dot(yc_ref[...].astype(BF16), wc_ref[...]))
    gt1 = mod_ref[0, :, 2 * d:3 * d]
    xn = x_ref[...] + gt1 * _dot(merged.astype(BF16), wo_ref[...])
    xo_ref[...] = xn
    ms = jnp.mean(xn * xn, axis=-1, keepdims=True)
    h2 = xn * lax.rsqrt(ms + EPS) * n2g_ref[...]
    sh2 = mod_ref[0, :, 3 * d:4 * d]
    sc2 = mod_ref[0, :, 4 * d:5 * d]
    h2_ref[...] = (h2 * (1.0 + sc2) + sh2).astype(BF16)


def _merge(y, bv, g, yb, yc, pg, x, mod3, mod_map, tm, gng, gnb, n2g, wa, wb, wc, wo, ones_bd):
    n_tok = x.shape[0]
    d = D_MODEL
    tile = lambda n: pl.BlockSpec((tm, n), lambda i: (i, 0))
    row = pl.BlockSpec((1, d), lambda i: (0, 0))
    full = lambda a: pl.BlockSpec((a, d), lambda i: (0, 0))
    return pl.pallas_call(
        _merge_kernel,
        grid=(n_tok // tm,),
        in_specs=[tile(d), tile(d), tile(d), tile(B_WIDTH), tile(C_WIDTH), tile(3 * d), tile(d),
                  pl.BlockSpec((1, 1, 6 * d), mod_map), row, row, row,
                  full(d), full(B_WIDTH), full(C_WIDTH), full(d), full(d)],
        out_specs=[tile(d), tile(d)],
        out_shape=[jax.ShapeDtypeStruct((n_tok, d), F32), jax.ShapeDtypeStruct((n_tok, d), BF16)],
        compiler_params=_cparams("parallel"),
        name="merge_outproj",
    )(y, bv, g, yb, yc, pg, x, mod3, gng, gnb, n2g, wa, wb, wc, wo, ones_bd)


def _peer_score_kernel(h_ref, wq_ref, keys_ref, s0_o, s1_o, e0_o, e1_o, tau_o):
    tmp = h_ref.shape[0]
    q_t = lax.dot_general(wq_ref[...], h_ref[...], (((1,), (1,)), ((), ())),
                          preferred_element_type=F32).astype(BF16)
    neg = jnp.float32(-jnp.inf)
    row16 = lax.broadcasted_iota(jnp.int32, (P_TOPK, tmp), 0)

    def top16(s):
        out = jnp.full((P_TOPK, tmp), neg, F32)
        for it in range(P_TOPK):
            m = jnp.max(s, axis=0, keepdims=True)
            out = jnp.where(row16 == it, m, out)
            s = jnp.where(s >= m, neg, s)
        return out

    for h in range(P_HEADS):
        s = []
        for p in range(2):
            hp = 2 * h + p
            s.append(_dot(keys_ref[hp], q_t[hp * P_HALF:(hp + 1) * P_HALF, :]))
        a16 = top16(s[0])
        b16 = top16(s[1])
        cands = [a16[r:r + 1, :] + b16 for r in range(P_TOPK)]
        cmax = a16[0:1, :] + b16[0:1, :]
        work = list(cands)
        tau = cmax
        for it in range(P_TOPK):
            m = work[0]
            for c in work[1:]:
                m = jnp.maximum(m, c)
            tau = jnp.max(m, axis=0, keepdims=True)
            if it + 1 < P_TOPK:
                work = [jnp.where(c >= tau, neg, c) for c in work]
        z = jnp.zeros((1, tmp), F32)
        for c in cands:
            z = z + jnp.sum(jnp.where(c >= tau, jnp.exp(c - cmax), 0.0), axis=0, keepdims=True)
        s0_o[h] = s[0]
        s1_o[h] = s[1]
        e0_o[h] = jnp.exp(s[0] - a16[0:1, :]) / z
        e1_o[h] = jnp.exp(s[1] - b16[0:1, :])
        tau_o[pl.ds(h, 1), :] = tau


def _peer_scores(h2, wq_t, keys):
    n_tok = h2.shape[0]
    tmp = TM_PEER
    big = pl.BlockSpec((P_HEADS, P_KEYS, tmp), lambda i: (0, 0, i))
    big_shape = jax.ShapeDtypeStruct((P_HEADS, P_KEYS, n_tok), F32)
    return pl.pallas_call(
        _peer_score_kernel,
        grid=(n_tok // tmp,),
        in_specs=[pl.BlockSpec((tmp, D_MODEL), lambda i: (i, 0)),
                  pl.BlockSpec((2 * P_HEADS * P_HALF, D_MODEL), lambda i: (0, 0)),
                  pl.BlockSpec((2 * P_HEADS, P_KEYS, P_HALF), lambda i: (0, 0, 0))],
        out_specs=[big, big, big, big, pl.BlockSpec((P_HEADS, tmp), lambda i: (0, i))],
        out_shape=[big_shape, big_shape, big_shape, big_shape,
                   jax.ShapeDtypeStruct((P_HEADS, n_tok), F32)],
        compiler_params=_cparams("parallel"),
        name="peer_scores",
    )(h2, wq_t, keys)


def _peer_expert_kernel(h_ref, u_ref, vt_ref, s0_ref, e0_ref, s1_ref, e1_ref, tau_ref, x_ref,
                        mod_ref, o_ref, acc_ref):
    e = pl.program_id(1)

    @pl.when(e == 0)
    def _():
        acc_ref[...] = jnp.zeros_like(acc_ref)

    hid = lax.dot_general(u_ref[...], h_ref[...], (((1,), (1,)), ((), ())),
                          preferred_element_type=F32)
    act = 0.5 * hid * (1.0 + lax.erf(hid * (2.0 ** -0.5)))
    n_i = PEER_EB // P_KEYS
    blocks = []
    for il in range(n_i):
        w = None
        for h in range(P_HEADS):
            c = s0_ref[h, pl.ds(il, 1), :] + s1_ref[h]
            p = e0_ref[h, pl.ds(il, 1), :] * e1_ref[h]
            term = jnp.where(c >= tau_ref[pl.ds(h, 1), :], p, 0.0)
            w = term if w is None else w + term
        blocks.append((w * act[il * P_KEYS:(il + 1) * P_KEYS, :]).astype(BF16))
    gated = jnp.concatenate(blocks, axis=0)
    acc_ref[...] += _dot(vt_ref[...], gated)

    @pl.when(e == pl.num_programs(1) - 1)
    def _():
        gt2 = mod_ref[0, :, 5 * D_MODEL:6 * D_MODEL]
        o_ref[...] = x_ref[...] + gt2 * acc_ref[...].T


def _peer_experts(h2, u_bf, vt_bf, s0, s1, e0, e1, tau, x, mod3, mod_map):
    n_tok = h2.shape[0]
    tmp = TM_PEER
    n_i = PEER_EB // P_KEYS
    sel = pl.BlockSpec((P_HEADS, n_i, tmp), lambda i, e: (0, e, i))
    full = pl.BlockSpec((P_HEADS, P_KEYS, tmp), lambda i, e: (0, 0, i))
    mod_map2 = lambda i, e: mod_map(i)
    return pl.pallas_call(
        _peer_expert_kernel,
        grid=(n_tok // tmp, P_EXPERTS // PEER_EB),
        in_specs=[pl.BlockSpec((tmp, D_MODEL), lambda i, e: (i, 0)),
                  pl.BlockSpec((PEER_EB, D_MODEL), lambda i, e: (e, 0)),
                  pl.BlockSpec((D_MODEL, PEER_EB), lambda i, e: (0, e)),
                  sel, sel, full, full,
                  pl.BlockSpec((P_HEADS, tmp), lambda i, e: (0, i)),
                  pl.BlockSpec((tmp, D_MODEL), lambda i, e: (i, 0)),
                  pl.BlockSpec((1, 1, 6 * D_MODEL), mod_map2)],
        out_specs=pl.BlockSpec((tmp, D_MODEL), lambda i, e: (i, 0)),
        out_shape=jax.ShapeDtypeStruct((n_tok, D_MODEL), F32),
        scratch_shapes=[pltpu.VMEM((D_MODEL, tmp), F32)],
        compiler_params=_cparams("parallel", "arbitrary"),
        name="peer_experts",
    )(h2, u_bf, vt_bf, s0, e0, s1, e1, tau, x, mod3)


def _final_norm_kernel(x_ref, g_ref, o_ref):
    x = x_ref[...]
    ms = jnp.mean(x * x, axis=-1, keepdims=True)
    o_ref[...] = x * lax.rsqrt(ms + EPS) * g_ref[...]


def _final_norm(x, g):
    n_tok = x.shape[0]
    return pl.pallas_call(
        _final_norm_kernel,
        grid=(n_tok // TM_PROJ,),
        in_specs=[pl.BlockSpec((TM_PROJ, D_MODEL), lambda i: (i, 0)),
                  pl.BlockSpec((1, D_MODEL), lambda i: (0, 0))],
        out_specs=pl.BlockSpec((TM_PROJ, D_MODEL), lambda i: (i, 0)),
        out_shape=jax.ShapeDtypeStruct((n_tok, D_MODEL), F32),
        compiler_params=_cparams("parallel"),
        name="final_norm",
    )(x, g)


def _block_diag2(w):
    z = jnp.zeros_like(w[0])
    return jnp.concatenate([jnp.concatenate([w[0], z], axis=1),
                            jnp.concatenate([z, w[1]], axis=1)], axis=0)


def kernel(x_prompt, x_sample, state_rwkv, c, c_ctx, ada_w, ada_b, norm1_g, norm2_g, w_in, shift_mu,
           decay_w0, decay_w2, iclr_a0, iclr_a2, gate_g2, k_k, k_a, r_k, gn_g, gn_b, w_a_out,
           pool_w, pool_b, pool_scale, w_b_out, conv_w, conv_b, cln_g, cln_b, w_c_out, w_out,
           peer_wq, peer_keys, peer_u, peer_v, final_g):
    d = D_MODEL
    batch, seq, _ = x_prompt.shape
    dec_batch, dec_seq, _ = x_sample.shape
    depth = ada_w.shape[0]
    n_ctx = batch * seq
    n_dec = dec_batch * dec_seq
    tm = seq
    assert n_ctx % dec_seq == 0 and dec_seq % tm == 0 and dec_seq % GRID_W == 0
    assert n_ctx % TM_PROJ == 0 and dec_seq % TM_PROJ == 0 and dec_seq % TM_PEER == 0
    assert n_ctx % TM_PEER == 0 and tm % GRID_W == 0 and dec_batch <= 7

    x = jnp.concatenate([x_prompt.reshape(n_ctx, d), x_sample.reshape(n_dec, d)], axis=0)
    n_tok = n_ctx + n_dec

    cond8 = jnp.zeros((8, d), F32).at[0].set(c_ctx).at[1:1 + dec_batch].set(c)
    mod_all = _modulation(cond8, ada_w, ada_b)

    seg_ids = jnp.arange(d) // A_HEAD
    ones_bd = (seg_ids[:, None] == seg_ids[None, :]).astype(BF16)

    map_proj = _mod_row_map(n_ctx // TM_PROJ, dec_seq // TM_PROJ)
    map_tm = _mod_row_map(n_ctx // tm, dec_seq // tm)
    map_peer = _mod_row_map(n_ctx // TM_PEER, dec_seq // TM_PEER)

    n_tiles = n_tok // tm
    tile_idx = jnp.arange(n_tiles)
    dec_tile = (tile_idx - n_ctx // tm) % (dec_seq // tm)
    is_start = jnp.where(tile_idx < n_ctx // tm, True, dec_tile == 0)[:, None]
    is_end = jnp.where(tile_idx < n_ctx // tm, True, dec_tile == dec_seq // tm - 1)[:, None]

    ctx_states = []
    for l in range(depth):
        mod3 = mod_all[l][:, None, :]
        w_in_bf = w_in[l].astype(BF16)
        (pa,) = _inproj(x, mod3, norm1_g[l][None], w_in_bf[:, :A_COLS], (A_COLS,), map_proj, "inproj_a")
        pb, pc, pg = _inproj(x, mod3, norm1_g[l][None], w_in_bf[:, A_COLS:],
                             (B_WIDTH, 2 * C_WIDTH, 3 * d), map_proj, "inproj_bcg")

        pa3 = pa.reshape(n_tiles, tm, A_COLS)
        zero_row = jnp.zeros((1, A_COLS), F32)
        halo_prev = jnp.concatenate([zero_row, pa3[:-1, tm - 1, :]], axis=0)
        halo_next = jnp.concatenate([pa3[1:, 0, :], zero_row], axis=0)
        halo_prev = jnp.where(is_start, 0.0, halo_prev)[:, None, :]
        halo_next = jnp.where(is_end, 0.0, halo_next)[:, None, :]
        prep_params = (shift_mu[l][None], decay_w0[l].reshape(1, 2 * d),
                       _block_diag2(decay_w2[l]).astype(BF16), iclr_a0[l].reshape(1, 2 * d),
                       _block_diag2(iclr_a2[l]).astype(BF16), gate_g2[l].astype(BF16),
                       k_k[l][None], k_a[l][None], r_k[l].reshape(1, d), ones_bd)
        r, v, kk, dec, kd, bb, bv, g = _rwkv_prep(pa, halo_prev, halo_next, tm, prep_params)

        ys = []
        for lo, hi, nb, nt, s0 in (
                (0, n_ctx, batch, seq, jnp.zeros((batch, 2, A_HEADS, A_HEAD, A_HEAD), F32)),
                (n_ctx, n_tok, dec_batch, dec_seq, state_rwkv[:, l])):
            vs = _value_split(nb)
            ops = _to_chain_layout((r[lo:hi], v[lo:hi], kk[lo:hi]), (dec[lo:hi], kd[lo:hi], bb[lo:hi]),
                                   s0, nb, nt, vs)
            y_t, sfin_t = _rwkv_scan(*ops)
            y_rows, s_fin = _from_chain_layout(y_t, sfin_t, nb, nt, vs)
            ys.append(y_rows)
            if lo == 0:
                ctx_states.append(s_fin)
        y_scan = jnp.concatenate(ys, axis=0)

        yb = _pool_mix(pb, n_ctx, seq, dec_seq, pool_w[l].astype(BF16), pool_b[l].reshape(1, B_WIDTH),
                       pool_scale[l][None])
        yc = _conv_mix(pc, tm, n_ctx // tm, conv_w[l], conv_b[l][None], cln_g[l][None], cln_b[l][None])

        x, h2 = _merge(y_scan, bv, g, yb, yc, pg, x, mod3, map_tm, tm, gn_g[l][None], gn_b[l][None],
                       norm2_g[l][None], w_a_out[l].astype(BF16), w_b_out[l].astype(BF16),
                       w_c_out[l].astype(BF16), w_out[l].astype(BF16), ones_bd)

        wq_t = peer_wq[l].T.astype(BF16)
        keys = peer_keys[l].reshape(2 * P_HEADS, P_KEYS, P_HALF).astype(BF16)
        s0p, s1p, e0p, e1p, tau = _peer_scores(h2, wq_t, keys)
        x = _peer_experts(h2, peer_u[l].astype(BF16), peer_v[l].T.astype(BF16), s0p, s1p, e0p, e1p,
                          tau, x, mod3, map_peer)

    y = _final_norm(x, final_g[None])
    new_state = jnp.stack(ctx_states, axis=1)
    return (y[:n_ctx].reshape(batch, seq, d), y[n_ctx:].reshape(dec_batch, dec_seq, d), new_state)
```

```python
import functools

import jax
import jax.numpy as jnp
from jax import lax
from jax.experimental import pallas as pl
from jax.experimental.pallas import tpu as pltpu

F32 = jnp.float32
BF16 = jnp.bfloat16

D_MODEL = 1024
A_HEADS = 16
A_HEAD = 64
A_LORA = 384
A_COLS = 3 * D_MODEL + A_LORA
B_WIDTH = 512
B_GROUP = 128
POOL_WINDOWS = (2, 4, 8, 16)
POOL_PAD = 16
C_WIDTH = 512
C_CONV = 31
CONV_PAD = 16
GRID_W = 64
P_HEADS = 8
P_KEYS = 128
P_TOPK = 16
P_HALF = 128
P_EXPERTS = P_KEYS * P_KEYS
EPS = 1e-6
GN_EPS = 64e-5
LN_EPS = 1e-5

LANES = 128
VMEM_LIMIT = 56 * 1024 * 1024

TM_PROJ = 512
TM_PEER = 512
PEER_EB = 1024
SCAN_TB = 64


def _cparams(*sem):
    return pltpu.CompilerParams(dimension_semantics=sem, vmem_limit_bytes=VMEM_LIMIT)


def _sigmoid(x):
    return 1.0 / (1.0 + jnp.exp(-x))


def _dot(a, b):
    return jnp.dot(a, b, preferred_element_type=F32)


def _segsum(x, ones_bd):
    hi = x.astype(BF16)
    lo = (x - hi.astype(F32)).astype(BF16)
    return _dot(hi, ones_bd) + _dot(lo, ones_bd)


def _mod_kernel(c_ref, w_ref, b_ref, o_ref):
    c = c_ref[...]
    s = c * _sigmoid(c)
    o_ref[...] = jnp.dot(s, w_ref[...], preferred_element_type=F32,
                         precision=lax.Precision.HIGHEST) + b_ref[...]


def _modulation(cond8, ada_w, ada_b):
    L = ada_w.shape[0]
    nb = 6
    return pl.pallas_call(
        _mod_kernel,
        grid=(L, nb),
        in_specs=[pl.BlockSpec((8, D_MODEL), lambda l, j: (0, 0)),
                  pl.BlockSpec((None, D_MODEL, D_MODEL), lambda l, j: (l, 0, j)),
                  pl.BlockSpec((None, 1, D_MODEL), lambda l, j: (l, 0, j))],
        out_specs=pl.BlockSpec((None, 8, D_MODEL), lambda l, j: (l, 0, j)),
        out_shape=jax.ShapeDtypeStruct((L, 8, 6 * D_MODEL), F32),
        compiler_params=_cparams("parallel", "parallel"),
        name="adaln_mod",
    )(cond8, ada_w, ada_b.reshape(L, 1, 6 * D_MODEL))


def _mod_row_map(n_ctx_tiles, tiles_per_dec_seq):
    def index_map(i):
        row = jnp.where(i < n_ctx_tiles, 0, 1 + (i - n_ctx_tiles) // tiles_per_dec_seq)
        return (row, 0, 0)
    return index_map


def _inproj_kernel(x_ref, mod_ref, g_ref, w_ref, *o_refs):
    x = x_ref[...]
    ms = jnp.mean(x * x, axis=-1, keepdims=True)
    y = x * lax.rsqrt(ms + EPS) * g_ref[...]
    sh = mod_ref[0, :, 0:D_MODEL]
    sc = mod_ref[0, :, D_MODEL:2 * D_MODEL]
    h = (y * (1.0 + sc) + sh).astype(BF16)
    off = 0
    for o_ref in o_refs:
        n = o_ref.shape[1]
        o_ref[...] = _dot(h, w_ref[:, off:off + n])
        off += n


def _inproj(x, mod3, g, w, splits, mod_map, name):
    n_tok = x.shape[0]
    n_out = w.shape[1]
    return pl.pallas_call(
        _inproj_kernel,
        grid=(n_tok // TM_PROJ,),
        in_specs=[pl.BlockSpec((TM_PROJ, D_MODEL), lambda i: (i, 0)),
                  pl.BlockSpec((1, 1, 6 * D_MODEL), mod_map),
                  pl.BlockSpec((1, D_MODEL), lambda i: (0, 0)),
                  pl.BlockSpec((D_MODEL, n_out), lambda i: (0, 0))],
        out_specs=[pl.BlockSpec((TM_PROJ, n), lambda i: (i, 0)) for n in splits],
        out_shape=[jax.ShapeDtypeStruct((n_tok, n), F32) for n in splits],
        compiler_params=_cparams("parallel"),
        name=name,
    )(x, mod3, g, w)


def _prep_kernel(pa_ref, hp_ref, hn_ref, mu_ref, w0_ref, w2_ref, a0_ref, a2_ref, g2_ref,
                 kkw_ref, ka_ref, rk_ref, ones_ref,
                 r_o, v_o, kk_o, dec_o, kd_o, bb_o, bv_o, g_o):
    tm = pa_ref.shape[0]

    def shifted(c0, c1):
        x = pa_ref[:, c0:c1]
        row = lax.broadcasted_iota(jnp.int32, x.shape, 0)
        prev = jnp.where(row == 0, hp_ref[0, :, c0:c1], pltpu.roll(x, 1, 0))
        nxt = jnp.where(row == tm - 1, hn_ref[0, :, c0:c1], pltpu.roll(x, tm - 1, 0))
        return x + (0.5 * (prev + nxt) - x) * mu_ref[:, c0:c1]

    d = D_MODEL
    r = shifted(0, d)
    k = shifted(d, 2 * d)
    v = shifted(2 * d, 3 * d)
    lo = shifted(3 * d, 3 * d + A_LORA)
    w_lo = lo[:, 0:128]
    a_lo = lo[:, 128:256]
    g_lo = lo[:, 256:384]
    ones_bd = ones_ref[...]

    r_o[...] = r
    v_o[...] = v

    kkraw = k * kkw_ref[...]
    nrm = jnp.sqrt(_segsum(kkraw * kkraw, ones_bd))
    kk = kkraw / jnp.maximum(nrm, 1e-12)
    kk_o[...] = kk

    z = -(w0_ref[...] + _dot(jnp.tanh(w_lo).astype(BF16), w2_ref[...]))
    softplus = jnp.maximum(z, 0.0) + jnp.log(1.0 + jnp.exp(-jnp.abs(z)))
    dec_o[...] = jnp.exp(-jnp.exp(-softplus - 0.5))

    a = _sigmoid(a0_ref[...] + _dot(a_lo.astype(BF16), a2_ref[...]))
    ka = ka_ref[...]
    kd0 = k * (1.0 + (a[:, 0:d] - 1.0) * ka)
    kd1 = k * (1.0 + (a[:, d:2 * d] - 1.0) * ka)
    kd_o[:, 0:d] = kd0
    kd_o[:, d:2 * d] = kd1
    bb_o[:, 0:d] = kk * a[:, 0:d]
    bb_o[:, d:2 * d] = kk * a[:, d:2 * d]

    bonus = _segsum(r * (kd0 + kd1) * rk_ref[...], ones_bd)
    bv_o[...] = bonus * v
    g_o[...] = _dot(_sigmoid(g_lo).astype(BF16), g2_ref[...])


def _rwkv_prep(pa, halo_prev, halo_next, tm, params):
    n_tok = pa.shape[0]
    d = D_MODEL
    row = lambda n: pl.BlockSpec((1, n), lambda i: (0, 0))
    full = lambda a, b: pl.BlockSpec((a, b), lambda i: (0, 0))
    tile = lambda n: pl.BlockSpec((tm, n), lambda i: (i, 0))
    halo = pl.BlockSpec((1, 1, A_COLS), lambda i: (i, 0, 0))
    outs = [d, d, d, 2 * d, 2 * d, 2 * d, d, d]
    return pl.pallas_call(
        _prep_kernel,
        grid=(n_tok // tm,),
        in_specs=[tile(A_COLS), halo, halo, row(A_COLS), row(2 * d), full(128, 2 * d), row(2 * d),
                  full(128, 2 * d), full(128, d), row(d), row(d), row(d), full(d, d)],
        out_specs=[tile(n) for n in outs],
        out_shape=[jax.ShapeDtypeStruct((n_tok, n), F32) for n in outs],
        compiler_params=_cparams("parallel"),
        name="rwkv_prep",
    )(pa, halo_prev, halo_next, *params)


def _scan_kernel(kk_ref, r_ref, w_ref, b_ref, kd_ref, v_ref, s0_ref, y_ref, sfin_ref, s_scr):
    nk, tb = kk_ref.shape[0], kk_ref.shape[1]
    rev = pl.program_id(0) == 1
    j = pl.program_id(2)

    @pl.when(j == 0)
    def _():
        s_scr[...] = s0_ref[...]

    def step(i, carry):
        t = jnp.where(rev, tb - 1 - i, i)
        parts = [None] * 4
        for k in range(nk):
            term = s_scr[k] * kk_ref[k, pl.ds(t, 1), :]
            parts[k % 4] = term if parts[k % 4] is None else parts[k % 4] + term
        sa = -((parts[0] + parts[1]) + (parts[2] + parts[3]))
        vt = v_ref[t]
        ys = [None] * 2
        for k in range(nk):
            s_new = (s_scr[k] * w_ref[k, pl.ds(t, 1), :] + sa * b_ref[k, pl.ds(t, 1), :]
                     + vt * kd_ref[k, pl.ds(t, 1), :])
            s_scr[k] = s_new
            term = s_new * r_ref[k, pl.ds(t, 1), :]
            ys[k % 2] = term if ys[k % 2] is None else ys[k % 2] + term
        y_ref[t] = ys[0] + ys[1]
        return carry

    lax.fori_loop(0, tb, step, 0)

    @pl.when(j == pl.num_programs(2) - 1)
    def _():
        sfin_ref[...] = s_scr[...]


def _rwkv_scan(kk_c, r_c, w_c, b_c, kd_c, v_c, s0_c):
    nk, T, L = kk_c.shape
    vq = v_c.shape[1]
    tb = min(SCAN_TB, T)
    nt = T // tb
    tj = lambda d, j: jnp.where(d == 0, j, nt - 1 - j)
    shared = pl.BlockSpec((nk, tb, LANES), lambda d, g, j: (0, tj(d, j), g))
    perdir = pl.BlockSpec((None, nk, tb, LANES), lambda d, g, j: (d, 0, tj(d, j), g))
    vspec = pl.BlockSpec((tb, vq, LANES), lambda d, g, j: (tj(d, j), 0, g))
    yspec = pl.BlockSpec((None, tb, vq, LANES), lambda d, g, j: (d, tj(d, j), 0, g))
    sspec = pl.BlockSpec((None, nk, vq, LANES), lambda d, g, j: (d, 0, 0, g))
    return pl.pallas_call(
        _scan_kernel,
        grid=(2, L // LANES, nt),
        in_specs=[shared, shared, perdir, perdir, perdir, vspec, sspec],
        out_specs=[yspec, sspec],
        out_shape=[jax.ShapeDtypeStruct((2, T, vq, L), F32), jax.ShapeDtypeStruct((2, nk, vq, L), F32)],
        scratch_shapes=[pltpu.VMEM((nk, vq, LANES), F32)],
        compiler_params=_cparams("parallel", "parallel", "arbitrary"),
        name="rwkv_scan",
    )(kk_c, r_c, w_c, b_c, kd_c, v_c, s0_c)


def _lane_plan(n_batch):
    full = LANES // A_HEADS
    if n_batch >= full:
        assert n_batch % full == 0
        return full, 1
    assert full % n_batch == 0
    return n_batch, full // n_batch


def _to_chain_kernel(x_ref, o_ref, scr, *, vs, value_rows):
    nb, tc = x_ref.shape[0], x_ref.shape[1]
    for b in range(nb):
        scr[b * A_HEADS:(b + 1) * A_HEADS] = x_ref[b].T.reshape(A_HEADS, A_HEAD, tc)
    vq = A_HEAD // vs
    if value_rows:
        for v in range(vq):
            rows = [scr[:, vh * vq + v, :] for vh in range(vs)]
            rows = rows[0] if vs == 1 else jnp.concatenate(rows, axis=0)
            o_ref[:, v, :] = rows.T
    else:
        for k in range(A_HEAD):
            rows = scr[:, k, :]
            if vs > 1:
                rows = jnp.concatenate([rows] * vs, axis=0)
            o_ref[k] = rows.T


def _to_chain(x, seq_len, batch0, n_batch, value_rows):
    n_col = x.shape[1] // D_MODEL
    nb, vs = _lane_plan(n_batch)
    assert batch0 % nb == 0 and seq_len % LANES == 0
    groups = n_batch // nb
    tc = LANES
    x3 = x.reshape(x.shape[0] // seq_len, seq_len, n_col * D_MODEL)
    vq = A_HEAD // vs
    if value_rows:
        out_spec = pl.BlockSpec((None, tc, vq, LANES), lambda c, g, j: (c, j, 0, g))
        out_shape = jax.ShapeDtypeStruct((n_col, seq_len, vq, groups * LANES), F32)
    else:
        out_spec = pl.BlockSpec((None, A_HEAD, tc, LANES), lambda c, g, j: (c, 0, j, g))
        out_shape = jax.ShapeDtypeStruct((n_col, A_HEAD, seq_len, groups * LANES), F32)
    return pl.pallas_call(
        functools.partial(_to_chain_kernel, vs=vs, value_rows=value_rows),
        grid=(n_col, groups, seq_len // tc),
        in_specs=[pl.BlockSpec((nb, tc, D_MODEL), lambda c, g, j: (batch0 // nb + g, j, c))],
        out_specs=out_spec,
        out_shape=out_shape,
        scratch_shapes=[pltpu.VMEM((nb * A_HEADS, A_HEAD, tc), F32)],
        compiler_params=_cparams("parallel", "parallel", "parallel"),
        name="to_chain_v" if value_rows else "to_chain_k",
    )(x3)


def _from_chain_kernel(*refs, vs):
    y_ref, o_ref, scr = refs[-3:]
    nb, tc = o_ref.shape[0], o_ref.shape[1]
    vq = A_HEAD // vs
    rows = nb * A_HEADS
    for v in range(vq):
        yv = (y_ref[0, :, v, :] + y_ref[1, :, v, :]).T
        for vh in range(vs):
            scr[:, vh * vq + v, :] = yv[vh * rows:(vh + 1) * rows]
    for b in range(nb):
        o_ref[b] = scr[b * A_HEADS:(b + 1) * A_HEADS].reshape(D_MODEL, tc).T


def _from_chain(y_c, prev_rows, n_tok, seq_len, batch0, n_batch):
    nb, vs = _lane_plan(n_batch)
    groups = n_batch // nb
    tc = LANES
    vq = A_HEAD // vs
    in_specs = [pl.BlockSpec((2, tc, vq, LANES), lambda g, j: (0, j, 0, g))]
    args = [y_c]
    aliases = {}
    if prev_rows is not None:
        in_specs = [pl.BlockSpec(memory_space=pl.ANY)] + in_specs
        args = [prev_rows.reshape(n_tok // seq_len, seq_len, D_MODEL)] + args
        aliases = {0: 0}
    out = pl.pallas_call(
        functools.partial(_from_chain_kernel, vs=vs),
        grid=(groups, seq_len // tc),
        in_specs=in_specs,
        out_specs=pl.BlockSpec((nb, tc, D_MODEL), lambda g, j: (batch0 // nb + g, j, 0)),
        out_shape=jax.ShapeDtypeStruct((n_tok // seq_len, seq_len, D_MODEL), F32),
        scratch_shapes=[pltpu.VMEM((nb * A_HEADS, A_HEAD, tc), F32)],
        input_output_aliases=aliases,
        compiler_params=_cparams("parallel", "parallel"),
        name="from_chain",
    )(*args)
    return out.reshape(n_tok, D_MODEL)


def _state_to_chain(s, vs):
    B = s.shape[0]
    vq = A_HEAD // vs
    s = jnp.transpose(s, (1, 4, 3, 0, 2)).reshape(2, A_HEAD, vs, vq, B * A_HEADS)
    return jnp.transpose(s, (0, 1, 3, 2, 4)).reshape(2, A_HEAD, vq, vs * B * A_HEADS)


def _state_from_chain(s_c, n_batch, vs):
    vq = A_HEAD // vs
    s = s_c.reshape(2, A_HEAD, vq, vs, n_batch, A_HEADS)
    return jnp.transpose(s, (4, 0, 5, 3, 2, 1)).reshape(n_batch, 2, A_HEADS, A_HEAD, A_HEAD)


def _pool_tail(d2, g, pw_ref, pb_ref, ps_ref):
    sl = slice(g * B_GROUP, (g + 1) * B_GROUP)
    y = _dot(d2.astype(BF16), pw_ref[g]) + pb_ref[:, sl]
    return y * ps_ref[:, sl]


def _window_count(t, w, n):
    lo = jnp.clip(t - w // 2, 0, n)
    hi = jnp.clip(t - w // 2 + w, 0, n)
    return (hi - lo).astype(F32)


def _pool_seq_kernel(z_ref, pw_ref, pb_ref, ps_ref, o_ref):
    n = z_ref.shape[0]
    pad = jnp.zeros((POOL_PAD, B_GROUP), F32)
    t = lax.broadcasted_iota(jnp.int32, (n, B_GROUP), 0)
    for g, w in enumerate(POOL_WINDOWS):
        sl = slice(g * B_GROUP, (g + 1) * B_GROUP)
        z = z_ref[:, sl]
        zp = jnp.concatenate([pad, z, pad], axis=0)
        acc = None
        for j in range(w):
            off = j - w // 2
            term = zp if off == 0 else pltpu.roll(zp, (-off) % (n + 2 * POOL_PAD), 0)
            acc = term if acc is None else acc + term
        d = acc[POOL_PAD:POOL_PAD + n] / _window_count(t, w, n) - z
        o_ref[:, sl] = _pool_tail(d, g, pw_ref, pb_ref, ps_ref)


def _pool_grid_kernel(z_ref, pw_ref, pb_ref, ps_ref, _, o_ref):
    rows, cb = z_ref.shape[0], z_ref.shape[1]
    pad = jnp.zeros((POOL_PAD, cb, B_GROUP), F32)
    t = lax.broadcasted_iota(jnp.int32, (rows, cb, B_GROUP), 0)
    for g, w in enumerate(POOL_WINDOWS):
        sl = slice(g * B_GROUP, (g + 1) * B_GROUP)
        z = z_ref[:, :, sl]
        zp = jnp.concatenate([pad, z, pad], axis=0)
        acc = None
        for j in range(w):
            s = POOL_PAD + j - w // 2
            term = zp[s:s + rows]
            acc = term if acc is None else acc + term
        d = acc / _window_count(t, w, rows) - z
        y = _pool_tail(d.reshape(rows * cb, B_GROUP), g, pw_ref, pb_ref, ps_ref)
        o_ref[:, :, sl] = y.reshape(rows, cb, B_GROUP)


def _pool_mix(pb, n_ctx, seq, dec_seq, pw, pbias, pscale):
    n_tok = pb.shape[0]
    wspecs = lambda nd: [pl.BlockSpec((4, B_GROUP, B_GROUP), lambda *a: (0, 0, 0)),
                         pl.BlockSpec((1, B_WIDTH), lambda *a: (0, 0)),
                         pl.BlockSpec((1, B_WIDTH), lambda *a: (0, 0))]
    y_ctx = pl.pallas_call(
        _pool_seq_kernel,
        grid=(n_ctx // seq,),
        in_specs=[pl.BlockSpec((seq, B_WIDTH), lambda i: (i, 0))] + wspecs(1),
        out_specs=pl.BlockSpec((seq, B_WIDTH), lambda i: (i, 0)),
        out_shape=jax.ShapeDtypeStruct((n_tok, B_WIDTH), F32),
        compiler_params=_cparams("parallel"),
        name="pool_seq",
    )(pb, pw, pbias, pscale)
    rows = dec_seq // GRID_W
    n_dec = (n_tok - n_ctx) // dec_seq
    ctx_blocks = n_ctx // dec_seq
    cb = 8
    shape4 = (n_tok // dec_seq, rows, GRID_W, B_WIDTH)
    blk = pl.BlockSpec((None, rows, cb, B_WIDTH), lambda b, c: (b + ctx_blocks, 0, c, 0))
    y_all = pl.pallas_call(
        _pool_grid_kernel,
        grid=(n_dec, GRID_W // cb),
        in_specs=[blk] + wspecs(2) + [pl.BlockSpec(memory_space=pl.ANY)],
        out_specs=blk,
        out_shape=jax.ShapeDtypeStruct(shape4, F32),
        input_output_aliases={4: 0},
        compiler_params=_cparams("parallel", "parallel"),
        name="pool_grid",
    )(pb.reshape(shape4), pw, pbias, pscale, y_ctx.reshape(shape4))
    return y_all.reshape(n_tok, B_WIDTH)


def _conv_kernel(pc_ref, cw_ref, cb_ref, lg_ref, lb_ref, o_ref, *, n_ctx_tiles):
    tm = pc_ref.shape[0]
    u = pc_ref[:, 0:C_WIDTH] * _sigmoid(pc_ref[:, C_WIDTH:2 * C_WIDTH])
    pad = jnp.zeros((CONV_PAD, C_WIDTH), F32)
    up = jnp.concatenate([pad, u, pad], axis=0)
    is_ctx = pl.program_id(0) < n_ctx_tiles
    t = lax.broadcasted_iota(jnp.int32, (tm, C_WIDTH), 0)
    pos = jnp.where(is_ctx, t, t & (GRID_W - 1))
    seg = jnp.where(is_ctx, tm, GRID_W)
    acc = jnp.zeros((tm, C_WIDTH), F32)
    for j in range(C_CONV):
        off = j - C_CONV // 2
        rolled = up if off == 0 else pltpu.roll(up, (-off) % (tm + 2 * CONV_PAD), 0)
        tap = rolled[CONV_PAD:CONV_PAD + tm]
        q = pos + off
        tap = jnp.where(q >= 0, jnp.where(q < seg, tap, 0.0), 0.0)
        acc = acc + tap * cw_ref[pl.ds(j, 1), :]
    c = acc + cb_ref[...]
    mu = jnp.mean(c, axis=-1, keepdims=True)
    cc = c - mu
    var = jnp.mean(cc * cc, axis=-1, keepdims=True)
    y = cc * lax.rsqrt(var + LN_EPS) * lg_ref[...] + lb_ref[...]
    o_ref[...] = y * _sigmoid(y)


def _conv_mix(pc, tm, n_ctx_tiles, cw, cb, lg, lb):
    n_tok = pc.shape[0]
    row = pl.BlockSpec((1, C_WIDTH), lambda i: (0, 0))
    return pl.pallas_call(
        functools.partial(_conv_kernel, n_ctx_tiles=n_ctx_tiles),
        grid=(n_tok // tm,),
        in_specs=[pl.BlockSpec((tm, 2 * C_WIDTH), lambda i: (i, 0)),
                  pl.BlockSpec((C_CONV, C_WIDTH), lambda i: (0, 0)), row, row, row],
        out_specs=pl.BlockSpec((tm, C_WIDTH), lambda i: (i, 0)),
        out_shape=jax.ShapeDtypeStruct((n_tok, C_WIDTH), F32),
        compiler_params=_cparams("parallel"),
        name="conv_mix",
    )(pc, cw, cb, lg, lb)


def _merge_kernel(y_ref, bv_ref, g_ref, yb_ref, yc_ref, pg_ref, x_ref, mod_ref, gng_ref, gnb_ref,
                  n2g_ref, wa_ref, wb_ref, wc_ref, wo_ref, ones_ref, xo_ref, h2_ref):
    d = D_MODEL
    ones_bd = ones_ref[...]
    y = y_ref[...]
    mean = _segsum(y, ones_bd) * (1.0 / A_HEAD)
    yc = y - mean
    var = _segsum(yc * yc, ones_bd) * (1.0 / A_HEAD)
    yn = yc * lax.rsqrt(var + GN_EPS) * gng_ref[...] + gnb_ref[...]
    ya = (yn + bv_ref[...]) * g_ref[...]
    merged = (_sigmoid(pg_ref[:, 0:d]) * _dot(ya.astype(BF16), wa_ref[...])
              + _sigmoid(pg_ref[:, d:2 * d]) * _dot(yb_ref[...].astype(BF16), wb_ref[...])
              + _sigmoid(pg_ref[:, 2 * d:3 * d]) * _dot(yc_ref[...].astype(BF16), wc_ref[...]))
    gt1 = mod_ref[0, :, 2 * d:3 * d]
    xn = x_ref[...] + gt1 * _dot(merged.astype(BF16), wo_ref[...])
    xo_ref[...] = xn
    ms = jnp.mean(xn * xn, axis=-1, keepdims=True)
    h2 = xn * lax.rsqrt(ms + EPS) * n2g_ref[...]
    sh2 = mod_ref[0, :, 3 * d:4 * d]
    sc2 = mod_ref[0, :, 4 * d:5 * d]
    h2_ref[...] = (h2 * (1.0 + sc2) + sh2).astype(BF16)


def _merge(y, bv, g, yb, yc, pg, x, mod3, mod_map, tm, gng, gnb, n2g, wa, wb, wc, wo, ones_bd):
    n_tok = x.shape[0]
    d = D_MODEL
    tile = lambda n: pl.BlockSpec((tm, n), lambda i: (i, 0))
    row = pl.BlockSpec((1, d), lambda i: (0, 0))
    full = lambda a: pl.BlockSpec((a, d), lambda i: (0, 0))
    return pl.pallas_call(
        _merge_kernel,
        grid=(n_tok // tm,),
        in_specs=[tile(d), tile(d), tile(d), tile(B_WIDTH), tile(C_WIDTH), tile(3 * d), tile(d),
                  pl.BlockSpec((1, 1, 6 * d), mod_map), row, row, row,
                  full(d), full(B_WIDTH), full(C_WIDTH), full(d), full(d)],
        out_specs=[tile(d), tile(d)],
        out_shape=[jax.ShapeDtypeStruct((n_tok, d), F32), jax.ShapeDtypeStruct((n_tok, d), BF16)],
        compiler_params=_cparams("parallel"),
        name="merge_outproj",
    )(y, bv, g, yb, yc, pg, x, mod3, gng, gnb, n2g, wa, wb, wc, wo, ones_bd)


def _peer_score_kernel(h_ref, wq_ref, keys_ref, s0_o, s1_o, e0_o, e1_o, tau_o):
    tmp = h_ref.shape[0]
    q_t = lax.dot_general(wq_ref[...], h_ref[...], (((1,), (1,)), ((), ())),
                          preferred_element_type=F32).astype(BF16)
    neg = jnp.float32(-jnp.inf)
    row16 = lax.broadcasted_iota(jnp.int32, (P_TOPK, tmp), 0)

    def top16(s):
        out = jnp.full((P_TOPK, tmp), neg, F32)
        for it in range(P_TOPK):
            m = jnp.max(s, axis=0, keepdims=True)
            out = jnp.where(row16 == it, m, out)
            s = jnp.where(s >= m, neg, s)
        return out

    for h in range(P_HEADS):
        s = []
        for p in range(2):
            hp = 2 * h + p
            s.append(_dot(keys_ref[hp], q_t[hp * P_HALF:(hp + 1) * P_HALF, :]))
        a16 = top16(s[0])
        b16 = top16(s[1])
        cands = [a16[r:r + 1, :] + b16 for r in range(P_TOPK)]
        cmax = a16[0:1, :] + b16[0:1, :]
        work = list(cands)
        tau = cmax
        for it in range(P_TOPK):
            m = work[0]
            for c in work[1:]:
                m = jnp.maximum(m, c)
            tau = jnp.max(m, axis=0, keepdims=True)
            if it + 1 < P_TOPK:
                work = [jnp.where(c >= tau, neg, c) for c in work]
        z = jnp.zeros((1, tmp), F32)
        for c in cands:
            z = z + jnp.sum(jnp.where(c >= tau, jnp.exp(c - cmax), 0.0), axis=0, keepdims=True)
        s0_o[h] = s[0]
        s1_o[h] = s[1]
        e0_o[h] = jnp.exp(s[0] - a16[0:1, :]) / z
        e1_o[h] = jnp.exp(s[1] - b16[0:1, :])
        tau_o[pl.ds(h, 1), :] = tau


def _peer_scores(h2, wq_t, keys):
    n_tok = h2.shape[0]
    tmp = TM_PEER
    big = pl.BlockSpec((P_HEADS, P_KEYS, tmp), lambda i: (0, 0, i))
    big_shape = jax.ShapeDtypeStruct((P_HEADS, P_KEYS, n_tok), F32)
    return pl.pallas_call(
        _peer_score_kernel,
        grid=(n_tok // tmp,),
        in_specs=[pl.BlockSpec((tmp, D_MODEL), lambda i: (i, 0)),
                  pl.BlockSpec((2 * P_HEADS * P_HALF, D_MODEL), lambda i: (0, 0)),
                  pl.BlockSpec((2 * P_HEADS, P_KEYS, P_HALF), lambda i: (0, 0, 0))],
        out_specs=[big, big, big, big, pl.BlockSpec((P_HEADS, tmp), lambda i: (0, i))],
        out_shape=[big_shape, big_shape, big_shape, big_shape,
                   jax.ShapeDtypeStruct((P_HEADS, n_tok), F32)],
        compiler_params=_cparams("parallel"),
        name="peer_scores",
    )(h2, wq_t, keys)


def _peer_expert_kernel(h_ref, u_ref, vt_ref, s0_ref, e0_ref, s1_ref, e1_ref, tau_ref, x_ref,
                        mod_ref, o_ref, acc_ref):
    e = pl.program_id(1)

    @pl.when(e == 0)
    def _():
        acc_ref[...] = jnp.zeros_like(acc_ref)

    hid = lax.dot_general(u_ref[...], h_ref[...], (((1,), (1,)), ((), ())),
                          preferred_element_type=F32)
    act = 0.5 * hid * (1.0 + lax.erf(hid * (2.0 ** -0.5)))
    n_i = PEER_EB // P_KEYS
    blocks = []
    for il in range(n_i):
        w = None
        for h in range(P_HEADS):
            c = s0_ref[h, pl.ds(il, 1), :] + s1_ref[h]
            p = e0_ref[h, pl.ds(il, 1), :] * e1_ref[h]
            term = jnp.where(c >= tau_ref[pl.ds(h, 1), :], p, 0.0)
            w = term if w is None else w + term
        blocks.append((w * act[il * P_KEYS:(il + 1) * P_KEYS, :]).astype(BF16))
    gated = jnp.concatenate(blocks, axis=0)
    acc_ref[...] += _dot(vt_ref[...], gated)

    @pl.when(e == pl.num_programs(1) - 1)
    def _():
        gt2 = mod_ref[0, :, 5 * D_MODEL:6 * D_MODEL]
        o_ref[...] = x_ref[...] + gt2 * acc_ref[...].T


def _peer_experts(h2, u_bf, vt_bf, s0, s1, e0, e1, tau, x, mod3, mod_map):
    n_tok = h2.shape[0]
    tmp = TM_PEER
    n_i = PEER_EB // P_KEYS
    sel = pl.BlockSpec((P_HEADS, n_i, tmp), lambda i, e: (0, e, i))
    full = pl.BlockSpec((P_HEADS, P_KEYS, tmp), lambda i, e: (0, 0, i))
    mod_map2 = lambda i, e: mod_map(i)
    return pl.pallas_call(
        _peer_expert_kernel,
        grid=(n_tok // tmp, P_EXPERTS // PEER_EB),
        in_specs=[pl.BlockSpec((tmp, D_MODEL), lambda i, e: (i, 0)),
                  pl.BlockSpec((PEER_EB, D_MODEL), lambda i, e: (e, 0)),
                  pl.BlockSpec((D_MODEL, PEER_EB), lambda i, e: (0, e)),
                  sel, sel, full, full,
                  pl.BlockSpec((P_HEADS, tmp), lambda i, e: (0, i)),
                  pl.BlockSpec((tmp, D_MODEL), lambda i, e: (i, 0)),
                  pl.BlockSpec((1, 1, 6 * D_MODEL), mod_map2)],
        out_specs=pl.BlockSpec((tmp, D_MODEL), lambda i, e: (i, 0)),
        out_shape=jax.ShapeDtypeStruct((n_tok, D_MODEL), F32),
        scratch_shapes=[pltpu.VMEM((D_MODEL, tmp), F32)],
        compiler_params=_cparams("parallel", "arbitrary"),
        name="peer_experts",
    )(h2, u_bf, vt_bf, s0, e0, s1, e1, tau, x, mod3)


def _final_norm_kernel(x_ref, g_ref, o_ref):
    x = x_ref[...]
    ms = jnp.mean(x * x, axis=-1, keepdims=True)
    o_ref[...] = x * lax.rsqrt(ms + EPS) * g_ref[...]


def _final_norm(x, g):
    n_tok = x.shape[0]
    return pl.pallas_call(
        _final_norm_kernel,
        grid=(n_tok // TM_PROJ,),
        in_specs=[pl.BlockSpec((TM_PROJ, D_MODEL), lambda i: (i, 0)),
                  pl.BlockSpec((1, D_MODEL), lambda i: (0, 0))],
        out_specs=pl.BlockSpec((TM_PROJ, D_MODEL), lambda i: (i, 0)),
        out_shape=jax.ShapeDtypeStruct((n_tok, D_MODEL), F32),
        compiler_params=_cparams("parallel"),
        name="final_norm",
    )(x, g)


def _block_diag2(w):
    z = jnp.zeros_like(w[0])
    return jnp.concatenate([jnp.concatenate([w[0], z], axis=1),
                            jnp.concatenate([z, w[1]], axis=1)], axis=0)


def kernel(x_prompt, x_sample, state_rwkv, c, c_ctx, ada_w, ada_b, norm1_g, norm2_g, w_in, shift_mu,
           decay_w0, decay_w2, iclr_a0, iclr_a2, gate_g2, k_k, k_a, r_k, gn_g, gn_b, w_a_out,
           pool_w, pool_b, pool_scale, w_b_out, conv_w, conv_b, cln_g, cln_b, w_c_out, w_out,
           peer_wq, peer_keys, peer_u, peer_v, final_g):
    d = D_MODEL
    batch, seq, _ = x_prompt.shape
    dec_batch, dec_seq, _ = x_sample.shape
    depth = ada_w.shape[0]
    n_ctx = batch * seq
    n_dec = dec_batch * dec_seq
    tm = seq
    assert n_ctx % dec_seq == 0 and dec_seq % tm == 0 and dec_seq % GRID_W == 0
    assert n_ctx % TM_PROJ == 0 and dec_seq % TM_PROJ == 0 and dec_seq % TM_PEER == 0
    assert n_ctx % TM_PEER == 0 and tm % GRID_W == 0 and dec_batch <= 7

    x = jnp.concatenate([x_prompt.reshape(n_ctx, d), x_sample.reshape(n_dec, d)], axis=0)
    n_tok = n_ctx + n_dec

    cond8 = jnp.zeros((8, d), F32).at[0].set(c_ctx).at[1:1 + dec_batch].set(c)
    mod_all = _modulation(cond8, ada_w, ada_b)

    seg_ids = jnp.arange(d) // A_HEAD
    ones_bd = (seg_ids[:, None] == seg_ids[None, :]).astype(BF16)

    map_proj = _mod_row_map(n_ctx // TM_PROJ, dec_seq // TM_PROJ)
    map_tm = _mod_row_map(n_ctx // tm, dec_seq // tm)
    map_peer = _mod_row_map(n_ctx // TM_PEER, dec_seq // TM_PEER)

    n_tiles = n_tok // tm
    tile_idx = jnp.arange(n_tiles)
    dec_tile = (tile_idx - n_ctx // tm) % (dec_seq // tm)
    is_start = jnp.where(tile_idx < n_ctx // tm, True, dec_tile == 0)[:, None]
    is_end = jnp.where(tile_idx < n_ctx // tm, True, dec_tile == dec_seq // tm - 1)[:, None]

    ctx_states = []
    for l in range(depth):
        mod3 = mod_all[l][:, None, :]
        w_in_bf = w_in[l].astype(BF16)
        (pa,) = _inproj(x, mod3, norm1_g[l][None], w_in_bf[:, :A_COLS], (A_COLS,), map_proj, "inproj_a")
        pb, pc, pg = _inproj(x, mod3, norm1_g[l][None], w_in_bf[:, A_COLS:],
                             (B_WIDTH, 2 * C_WIDTH, 3 * d), map_proj, "inproj_bcg")

        pa3 = pa.reshape(n_tiles, tm, A_COLS)
        zero_row = jnp.zeros((1, A_COLS), F32)
        halo_prev = jnp.concatenate([zero_row, pa3[:-1, tm - 1, :]], axis=0)
        halo_next = jnp.concatenate([pa3[1:, 0, :], zero_row], axis=0)
        halo_prev = jnp.where(is_start, 0.0, halo_prev)[:, None, :]
        halo_next = jnp.where(is_end, 0.0, halo_next)[:, None, :]
        prep_params = (shift_mu[l][None], decay_w0[l].reshape(1, 2 * d),
                       _block_diag2(decay_w2[l]).astype(BF16), iclr_a0[l].reshape(1, 2 * d),
                       _block_diag2(iclr_a2[l]).astype(BF16), gate_g2[l].astype(BF16),
                       k_k[l][None], k_a[l][None], r_k[l].reshape(1, d), ones_bd)
        r, v, kk, dec, kd, bb, bv, g = _rwkv_prep(pa, halo_prev, halo_next, tm, prep_params)

        y_scan = None
        for seq_len, batch0, nb, s0 in ((seq, 0, batch, None),
                                        (dec_seq, n_ctx // dec_seq, dec_batch, state_rwkv[:, l])):
            vs = _lane_plan(nb)[1]
            if s0 is None:
                s0_c = jnp.zeros((2, A_HEAD, A_HEAD // vs, vs * nb * A_HEADS), F32)
            else:
                s0_c = _state_to_chain(s0, vs)
            lay = lambda a, value_rows=False: _to_chain(a, seq_len, batch0, nb, value_rows)
            y_c, sfin_c = _rwkv_scan(lay(kk)[0], lay(r)[0], lay(dec), lay(bb), lay(kd),
                                     lay(v, True)[0], s0_c)
            y_scan = _from_chain(y_c, y_scan, n_tok, seq_len, batch0, nb)
            if s0 is None:
                ctx_states.append(_state_from_chain(sfin_c, nb, vs))

        yb = _pool_mix(pb, n_ctx, seq, dec_seq, pool_w[l].astype(BF16), pool_b[l].reshape(1, B_WIDTH),
                       pool_scale[l][None])
        yc = _conv_mix(pc, tm, n_ctx // tm, conv_w[l], conv_b[l][None], cln_g[l][None], cln_b[l][None])

        x, h2 = _merge(y_scan, bv, g, yb, yc, pg, x, mod3, map_tm, tm, gn_g[l][None], gn_b[l][None],
                       norm2_g[l][None], w_a_out[l].astype(BF16), w_b_out[l].astype(BF16),
                       w_c_out[l].astype(BF16), w_out[l].astype(BF16), ones_bd)

        wq_t = peer_wq[l].T.astype(BF16)
        keys = peer_keys[l].reshape(2 * P_HEADS, P_KEYS, P_HALF).astype(BF16)
        s0p, s1p, e0p, e1p, tau = _peer_scores(h2, wq_t, keys)
        x = _peer_experts(h2, peer_u[l].astype(BF16), peer_v[l].T.astype(BF16), s0p, s1p, e0p, e1p,
                          tau, x, mod3, map_peer)

    y = _final_norm(x, final_g[None])
    new_state = jnp.stack(ctx_states, axis=1)
    return (y[:n_ctx].reshape(batch, seq, d), y[n_ctx:].reshape(dec_batch, dec_seq, d), new_state)
```

```python
import functools

import jax
import jax.numpy as jnp
from jax import lax
from jax.experimental import pallas as pl
from jax.experimental.pallas import tpu as pltpu

F32 = jnp.float32
BF16 = jnp.bfloat16

D_MODEL = 1024
A_HEADS = 16
A_HEAD = 64
A_LORA = 384
A_COLS = 3 * D_MODEL + A_LORA
B_WIDTH = 512
B_GROUP = 128
POOL_WINDOWS = (2, 4, 8, 16)
POOL_PAD = 16
C_WIDTH = 512
C_CONV = 31
CONV_PAD = 16
GRID_W = 64
P_HEADS = 8
P_KEYS = 128
P_TOPK = 16
P_HALF = 128
P_EXPERTS = P_KEYS * P_KEYS
EPS = 1e-6
GN_EPS = 64e-5
LN_EPS = 1e-5

LANES = 128
VMEM_LIMIT = 56 * 1024 * 1024

TM_PROJ = 512
TM_PEER = 512
PEER_EB = 1024
SCAN_TB = 16


def _cparams(*sem):
    return pltpu.CompilerParams(dimension_semantics=sem, vmem_limit_bytes=VMEM_LIMIT)


def _sigmoid(x):
    return 1.0 / (1.0 + jnp.exp(-x))


def _dot(a, b):
    return jnp.dot(a, b, preferred_element_type=F32)


def _segsum(x, ones_bd):
    hi = x.astype(BF16)
    lo = (x - hi.astype(F32)).astype(BF16)
    return _dot(hi, ones_bd) + _dot(lo, ones_bd)


def _mod_kernel(c_ref, w_ref, b_ref, o_ref):
    c = c_ref[...]
    s = c * _sigmoid(c)
    o_ref[...] = jnp.dot(s, w_ref[...], preferred_element_type=F32,
                         precision=lax.Precision.HIGHEST) + b_ref[...]


def _modulation(cond8, ada_w, ada_b):
    L = ada_w.shape[0]
    nb = 6
    return pl.pallas_call(
        _mod_kernel,
        grid=(L, nb),
        in_specs=[pl.BlockSpec((8, D_MODEL), lambda l, j: (0, 0)),
                  pl.BlockSpec((None, D_MODEL, D_MODEL), lambda l, j: (l, 0, j)),
                  pl.BlockSpec((None, 1, D_MODEL), lambda l, j: (l, 0, j))],
        out_specs=pl.BlockSpec((None, 8, D_MODEL), lambda l, j: (l, 0, j)),
        out_shape=jax.ShapeDtypeStruct((L, 8, 6 * D_MODEL), F32),
        compiler_params=_cparams("parallel", "parallel"),
        name="adaln_mod",
    )(cond8, ada_w, ada_b.reshape(L, 1, 6 * D_MODEL))


def _mod_row_map(n_ctx_tiles, tiles_per_dec_seq):
    def index_map(i):
        row = jnp.where(i < n_ctx_tiles, 0, 1 + (i - n_ctx_tiles) // tiles_per_dec_seq)
        return (row, 0, 0)
    return index_map


def _inproj_kernel(x_ref, mod_ref, g_ref, w_ref, *o_refs):
    x = x_ref[...]
    ms = jnp.mean(x * x, axis=-1, keepdims=True)
    y = x * lax.rsqrt(ms + EPS) * g_ref[...]
    sh = mod_ref[0, :, 0:D_MODEL]
    sc = mod_ref[0, :, D_MODEL:2 * D_MODEL]
    h = (y * (1.0 + sc) + sh).astype(BF16)
    off = 0
    for o_ref in o_refs:
        n = o_ref.shape[1]
        o_ref[...] = _dot(h, w_ref[:, off:off + n])
        off += n


def _inproj(x, mod3, g, w, splits, mod_map, name):
    n_tok = x.shape[0]
    n_out = w.shape[1]
    return pl.pallas_call(
        _inproj_kernel,
        grid=(n_tok // TM_PROJ,),
        in_specs=[pl.BlockSpec((TM_PROJ, D_MODEL), lambda i: (i, 0)),
                  pl.BlockSpec((1, 1, 6 * D_MODEL), mod_map),
                  pl.BlockSpec((1, D_MODEL), lambda i: (0, 0)),
                  pl.BlockSpec((D_MODEL, n_out), lambda i: (0, 0))],
        out_specs=[pl.BlockSpec((TM_PROJ, n), lambda i: (i, 0)) for n in splits],
        out_shape=[jax.ShapeDtypeStruct((n_tok, n), F32) for n in splits],
        compiler_params=_cparams("parallel"),
        name=name,
    )(x, mod3, g, w)


def _prep_kernel(pa_ref, hp_ref, hn_ref, mu_ref, w0_ref, w2_ref, a0_ref, a2_ref, g2_ref,
                 kkw_ref, ka_ref, rk_ref, ones_ref,
                 r_o, v_o, kk_o, dec_o, kd_o, bb_o, bv_o, g_o):
    tm = pa_ref.shape[0]

    def shifted(c0, c1):
        x = pa_ref[:, c0:c1]
        row = lax.broadcasted_iota(jnp.int32, x.shape, 0)
        prev = jnp.where(row == 0, hp_ref[0, :, c0:c1], pltpu.roll(x, 1, 0))
        nxt = jnp.where(row == tm - 1, hn_ref[0, :, c0:c1], pltpu.roll(x, tm - 1, 0))
        return x + (0.5 * (prev + nxt) - x) * mu_ref[:, c0:c1]

    d = D_MODEL
    r = shifted(0, d)
    k = shifted(d, 2 * d)
    v = shifted(2 * d, 3 * d)
    lo = shifted(3 * d, 3 * d + A_LORA)
    w_lo = lo[:, 0:128]
    a_lo = lo[:, 128:256]
    g_lo = lo[:, 256:384]
    ones_bd = ones_ref[...]

    r_o[...] = r
    v_o[...] = v

    kkraw = k * kkw_ref[...]
    nrm = jnp.sqrt(_segsum(kkraw * kkraw, ones_bd))
    kk = kkraw / jnp.maximum(nrm, 1e-12)
    kk_o[...] = kk

    z = -(w0_ref[...] + _dot(jnp.tanh(w_lo).astype(BF16), w2_ref[...]))
    softplus = jnp.maximum(z, 0.0) + jnp.log(1.0 + jnp.exp(-jnp.abs(z)))
    dec_o[...] = jnp.exp(-jnp.exp(-softplus - 0.5))

    a = _sigmoid(a0_ref[...] + _dot(a_lo.astype(BF16), a2_ref[...]))
    ka = ka_ref[...]
    kd0 = k * (1.0 + (a[:, 0:d] - 1.0) * ka)
    kd1 = k * (1.0 + (a[:, d:2 * d] - 1.0) * ka)
    kd_o[:, 0:d] = kd0
    kd_o[:, d:2 * d] = kd1
    bb_o[:, 0:d] = kk * a[:, 0:d]
    bb_o[:, d:2 * d] = kk * a[:, d:2 * d]

    bonus = _segsum(r * (kd0 + kd1) * rk_ref[...], ones_bd)
    bv_o[...] = bonus * v
    g_o[...] = _dot(_sigmoid(g_lo).astype(BF16), g2_ref[...])


def _rwkv_prep(pa, halo_prev, halo_next, tm, params):
    n_tok = pa.shape[0]
    d = D_MODEL
    row = lambda n: pl.BlockSpec((1, n), lambda i: (0, 0))
    full = lambda a, b: pl.BlockSpec((a, b), lambda i: (0, 0))
    tile = lambda n: pl.BlockSpec((tm, n), lambda i: (i, 0))
    halo = pl.BlockSpec((1, 1, A_COLS), lambda i: (i, 0, 0))
    outs = [d, d, d, 2 * d, 2 * d, 2 * d, d, d]
    return pl.pallas_call(
        _prep_kernel,
        grid=(n_tok // tm,),
        in_specs=[tile(A_COLS), halo, halo, row(A_COLS), row(2 * d), full(128, 2 * d), row(2 * d),
                  full(128, 2 * d), full(128, d), row(d), row(d), row(d), full(d, d)],
        out_specs=[tile(n) for n in outs],
        out_shape=[jax.ShapeDtypeStruct((n_tok, n), F32) for n in outs],
        compiler_params=_cparams("parallel"),
        name="rwkv_prep",
    )(pa, halo_prev, halo_next, *params)


SCAN_ROWS = 32


def _scan_kernel(kkf, kkb, rf, rb, wf, wb, bf, bb, kdf, kdb, vf, vb, s0_ref,
                 yf_ref, yb_ref, sfin_ref, s_scr):
    nk, tb = kkf.shape[0], kkf.shape[1]
    vq = vf.shape[1]
    j = pl.program_id(1)

    @pl.when(j == 0)
    def _():
        s_scr[...] = s0_ref[...]

    def advance(d, t, rs, kk_ref, r_ref, w_ref, b_ref, kd_ref, v_ref, y_ref):
        parts = [None] * 4
        for k in range(nk):
            term = s_scr[d, k, rs, :] * kk_ref[k, pl.ds(t, 1), :]
            parts[k % 4] = term if parts[k % 4] is None else parts[k % 4] + term
        sa = -((parts[0] + parts[1]) + (parts[2] + parts[3]))
        vt = v_ref[t, rs, :]
        ys = [None] * 2
        for k in range(nk):
            s_new = (s_scr[d, k, rs, :] * w_ref[k, pl.ds(t, 1), :] + sa * b_ref[k, pl.ds(t, 1), :]
                     + vt * kd_ref[k, pl.ds(t, 1), :])
            s_scr[d, k, rs, :] = s_new
            term = s_new * r_ref[k, pl.ds(t, 1), :]
            ys[k % 2] = term if ys[k % 2] is None else ys[k % 2] + term
        y_ref[t, rs, :] = ys[0] + ys[1]

    nrows = min(vq, SCAN_ROWS)

    def step(i, carry):
        for p in range(vq // nrows):
            rs = slice(p * nrows, (p + 1) * nrows)
            advance(0, i, rs, kkf, rf, wf, bf, kdf, vf, yf_ref)
            advance(1, tb - 1 - i, rs, kkb, rb, wb, bb, kdb, vb, yb_ref)
        return carry

    lax.fori_loop(0, tb, step, 0)

    @pl.when(j == pl.num_programs(1) - 1)
    def _():
        sfin_ref[...] = s_scr[...]


def _rwkv_scan(kk_c, r_c, w_c, b_c, kd_c, v_c, s0_c):
    nk, T, L = kk_c.shape
    vq = v_c.shape[1]
    tb = min(SCAN_TB, T)
    nt = T // tb
    fwd = lambda j: j
    bwd = lambda j: nt - 1 - j
    shared = lambda tj: pl.BlockSpec((nk, tb, LANES), lambda g, j: (0, tj(j), g))
    perdir = lambda d, tj: pl.BlockSpec((None, nk, tb, LANES), lambda g, j: (d, 0, tj(j), g))
    vspec = lambda tj: pl.BlockSpec((tb, vq, LANES), lambda g, j: (tj(j), 0, g))
    sspec = pl.BlockSpec((2, nk, vq, LANES), lambda g, j: (0, 0, 0, g))
    y_shape = jax.ShapeDtypeStruct((T, vq, L), F32)
    return pl.pallas_call(
        _scan_kernel,
        grid=(L // LANES, nt),
        in_specs=[shared(fwd), shared(bwd), shared(fwd), shared(bwd),
                  perdir(0, fwd), perdir(1, bwd), perdir(0, fwd), perdir(1, bwd),
                  perdir(0, fwd), perdir(1, bwd), vspec(fwd), vspec(bwd), sspec],
        out_specs=[vspec(fwd), vspec(bwd), sspec],
        out_shape=[y_shape, y_shape, jax.ShapeDtypeStruct((2, nk, vq, L), F32)],
        scratch_shapes=[pltpu.VMEM((2, nk, vq, LANES), F32)],
        compiler_params=_cparams("parallel", "arbitrary"),
        name="rwkv_scan",
    )(kk_c, kk_c, r_c, r_c, w_c, w_c, b_c, b_c, kd_c, kd_c, v_c, v_c, s0_c)


def _lane_plan(n_batch):
    full = LANES // A_HEADS
    if n_batch >= full:
        assert n_batch % full == 0
        return full, 1
    assert full % n_batch == 0
    return n_batch, full // n_batch


def _to_chain_kernel(x_ref, o_ref, scr, *, vs, value_rows):
    nb, tc = x_ref.shape[0], x_ref.shape[1]
    for b in range(nb):
        scr[b * A_HEADS:(b + 1) * A_HEADS] = x_ref[b].T.reshape(A_HEADS, A_HEAD, tc)
    vq = A_HEAD // vs
    if value_rows:
        for v in range(vq):
            rows = [scr[:, vh * vq + v, :] for vh in range(vs)]
            rows = rows[0] if vs == 1 else jnp.concatenate(rows, axis=0)
            o_ref[:, v, :] = rows.T
    else:
        for k in range(A_HEAD):
            rows = scr[:, k, :]
            if vs > 1:
                rows = jnp.concatenate([rows] * vs, axis=0)
            o_ref[k] = rows.T


def _to_chain(x, seq_len, batch0, n_batch, value_rows):
    n_col = x.shape[1] // D_MODEL
    nb, vs = _lane_plan(n_batch)
    assert batch0 % nb == 0 and seq_len % LANES == 0
    groups = n_batch // nb
    tc = LANES
    x3 = x.reshape(x.shape[0] // seq_len, seq_len, n_col * D_MODEL)
    vq = A_HEAD // vs
    if value_rows:
        out_spec = pl.BlockSpec((None, tc, vq, LANES), lambda c, g, j: (c, j, 0, g))
        out_shape = jax.ShapeDtypeStruct((n_col, seq_len, vq, groups * LANES), F32)
    else:
        out_spec = pl.BlockSpec((None, A_HEAD, tc, LANES), lambda c, g, j: (c, 0, j, g))
        out_shape = jax.ShapeDtypeStruct((n_col, A_HEAD, seq_len, groups * LANES), F32)
    return pl.pallas_call(
        functools.partial(_to_chain_kernel, vs=vs, value_rows=value_rows),
        grid=(n_col, groups, seq_len // tc),
        in_specs=[pl.BlockSpec((nb, tc, D_MODEL), lambda c, g, j: (batch0 // nb + g, j, c))],
        out_specs=out_spec,
        out_shape=out_shape,
        scratch_shapes=[pltpu.VMEM((nb * A_HEADS, A_HEAD, tc), F32)],
        compiler_params=_cparams("parallel", "parallel", "parallel"),
        name="to_chain_v" if value_rows else "to_chain_k",
    )(x3)


def _from_chain_kernel(*refs, vs):
    yf_ref, yb_ref, o_ref, scr = refs[-4:]
    nb, tc = o_ref.shape[0], o_ref.shape[1]
    vq = A_HEAD // vs
    rows = nb * A_HEADS
    for v in range(vq):
        yv = (yf_ref[:, v, :] + yb_ref[:, v, :]).T
        for vh in range(vs):
            scr[:, vh * vq + v, :] = yv[vh * rows:(vh + 1) * rows]
    for b in range(nb):
        o_ref[b] = scr[b * A_HEADS:(b + 1) * A_HEADS].reshape(D_MODEL, tc).T


def _from_chain(y_fwd, y_bwd, prev_rows, n_tok, seq_len, batch0, n_batch):
    nb, vs = _lane_plan(n_batch)
    groups = n_batch // nb
    tc = LANES
    vq = A_HEAD // vs
    in_specs = [pl.BlockSpec((tc, vq, LANES), lambda g, j: (j, 0, g))] * 2
    args = [y_fwd, y_bwd]
    aliases = {}
    if prev_rows is not None:
        in_specs = [pl.BlockSpec(memory_space=pl.ANY)] + in_specs
        args = [prev_rows.reshape(n_tok // seq_len, seq_len, D_MODEL)] + args
        aliases = {0: 0}
    out = pl.pallas_call(
        functools.partial(_from_chain_kernel, vs=vs),
        grid=(groups, seq_len // tc),
        in_specs=in_specs,
        out_specs=pl.BlockSpec((nb, tc, D_MODEL), lambda g, j: (batch0 // nb + g, j, 0)),
        out_shape=jax.ShapeDtypeStruct((n_tok // seq_len, seq_len, D_MODEL), F32),
        scratch_shapes=[pltpu.VMEM((nb * A_HEADS, A_HEAD, tc), F32)],
        input_output_aliases=aliases,
        compiler_params=_cparams("parallel", "parallel"),
        name="from_chain",
    )(*args)
    return out.reshape(n_tok, D_MODEL)


def _state_to_chain(s, vs):
    B = s.shape[0]
    vq = A_HEAD // vs
    s = jnp.transpose(s, (1, 4, 3, 0, 2)).reshape(2, A_HEAD, vs, vq, B * A_HEADS)
    return jnp.transpose(s, (0, 1, 3, 2, 4)).reshape(2, A_HEAD, vq, vs * B * A_HEADS)


def _state_from_chain(s_c, n_batch, vs):
    vq = A_HEAD // vs
    s = s_c.reshape(2, A_HEAD, vq, vs, n_batch, A_HEADS)
    return jnp.transpose(s, (4, 0, 5, 3, 2, 1)).reshape(n_batch, 2, A_HEADS, A_HEAD, A_HEAD)


def _pool_tail(d2, g, pw_ref, pb_ref, ps_ref):
    sl = slice(g * B_GROUP, (g + 1) * B_GROUP)
    y = _dot(d2.astype(BF16), pw_ref[g]) + pb_ref[:, sl]
    return y * ps_ref[:, sl]


def _window_count(t, w, n):
    lo = jnp.clip(t - w // 2, 0, n)
    hi = jnp.clip(t - w // 2 + w, 0, n)
    return (hi - lo).astype(F32)


def _pool_seq_kernel(z_ref, pw_ref, pb_ref, ps_ref, o_ref):
    n = z_ref.shape[0]
    pad = jnp.zeros((POOL_PAD, B_GROUP), F32)
    t = lax.broadcasted_iota(jnp.int32, (n, B_GROUP), 0)
    for g, w in enumerate(POOL_WINDOWS):
        sl = slice(g * B_GROUP, (g + 1) * B_GROUP)
        z = z_ref[:, sl]
        zp = jnp.concatenate([pad, z, pad], axis=0)
        acc = None
        for j in range(w):
            off = j - w // 2
            term = zp if off == 0 else pltpu.roll(zp, (-off) % (n + 2 * POOL_PAD), 0)
            acc = term if acc is None else acc + term
        d = acc[POOL_PAD:POOL_PAD + n] / _window_count(t, w, n) - z
        o_ref[:, sl] = _pool_tail(d, g, pw_ref, pb_ref, ps_ref)


def _pool_grid_kernel(z_ref, pw_ref, pb_ref, ps_ref, _, o_ref):
    rows, cb = z_ref.shape[0], z_ref.shape[1]
    pad = jnp.zeros((POOL_PAD, cb, B_GROUP), F32)
    t = lax.broadcasted_iota(jnp.int32, (rows, cb, B_GROUP), 0)
    for g, w in enumerate(POOL_WINDOWS):
        sl = slice(g * B_GROUP, (g + 1) * B_GROUP)
        z = z_ref[:, :, sl]
        zp = jnp.concatenate([pad, z, pad], axis=0)
        acc = None
        for j in range(w):
            s = POOL_PAD + j - w // 2
            term = zp[s:s + rows]
            acc = term if acc is None else acc + term
        d = acc / _window_count(t, w, rows) - z
        y = _pool_tail(d.reshape(rows * cb, B_GROUP), g, pw_ref, pb_ref, ps_ref)
        o_ref[:, :, sl] = y.reshape(rows, cb, B_GROUP)


def _pool_mix(pb, n_ctx, seq, dec_seq, pw, pbias, pscale):
    n_tok = pb.shape[0]
    wspecs = lambda nd: [pl.BlockSpec((4, B_GROUP, B_GROUP), lambda *a: (0, 0, 0)),
                         pl.BlockSpec((1, B_WIDTH), lambda *a: (0, 0)),
                         pl.BlockSpec((1, B_WIDTH), lambda *a: (0, 0))]
    y_ctx = pl.pallas_call(
        _pool_seq_kernel,
        grid=(n_ctx // seq,),
        in_specs=[pl.BlockSpec((seq, B_WIDTH), lambda i: (i, 0))] + wspecs(1),
        out_specs=pl.BlockSpec((seq, B_WIDTH), lambda i: (i, 0)),
        out_shape=jax.ShapeDtypeStruct((n_tok, B_WIDTH), F32),
        compiler_params=_cparams("parallel"),
        name="pool_seq",
    )(pb, pw, pbias, pscale)
    rows = dec_seq // GRID_W
    n_dec = (n_tok - n_ctx) // dec_seq
    ctx_blocks = n_ctx // dec_seq
    cb = 8
    shape4 = (n_tok // dec_seq, rows, GRID_W, B_WIDTH)
    blk = pl.BlockSpec((None, rows, cb, B_WIDTH), lambda b, c: (b + ctx_blocks, 0, c, 0))
    y_all = pl.pallas_call(
        _pool_grid_kernel,
        grid=(n_dec, GRID_W // cb),
        in_specs=[blk] + wspecs(2) + [pl.BlockSpec(memory_space=pl.ANY)],
        out_specs=blk,
        out_shape=jax.ShapeDtypeStruct(shape4, F32),
        input_output_aliases={4: 0},
        compiler_params=_cparams("parallel", "parallel"),
        name="pool_grid",
    )(pb.reshape(shape4), pw, pbias, pscale, y_ctx.reshape(shape4))
    return y_all.reshape(n_tok, B_WIDTH)


def _conv_kernel(pc_ref, cw_ref, cb_ref, lg_ref, lb_ref, o_ref, *, n_ctx_tiles):
    tm = pc_ref.shape[0]
    u = pc_ref[:, 0:C_WIDTH] * _sigmoid(pc_ref[:, C_WIDTH:2 * C_WIDTH])
    pad = jnp.zeros((CONV_PAD, C_WIDTH), F32)
    up = jnp.concatenate([pad, u, pad], axis=0)
    is_ctx = pl.program_id(0) < n_ctx_tiles
    t = lax.broadcasted_iota(jnp.int32, (tm, C_WIDTH), 0)
    pos = jnp.where(is_ctx, t, t & (GRID_W - 1))
    seg = jnp.where(is_ctx, tm, GRID_W)
    acc = jnp.zeros((tm, C_WIDTH), F32)
    for j in range(C_CONV):
        off = j - C_CONV // 2
        rolled = up if off == 0 else pltpu.roll(up, (-off) % (tm + 2 * CONV_PAD), 0)
        tap = rolled[CONV_PAD:CONV_PAD + tm]
        q = pos + off
        tap = jnp.where(q >= 0, jnp.where(q < seg, tap, 0.0), 0.0)
        acc = acc + tap * cw_ref[pl.ds(j, 1), :]
    c = acc + cb_ref[...]
    mu = jnp.mean(c, axis=-1, keepdims=True)
    cc = c - mu
    var = jnp.mean(cc * cc, axis=-1, keepdims=True)
    y = cc * lax.rsqrt(var + LN_EPS) * lg_ref[...] + lb_ref[...]
    o_ref[...] = y * _sigmoid(y)


def _conv_mix(pc, tm, n_ctx_tiles, cw, cb, lg, lb):
    n_tok = pc.shape[0]
    row = pl.BlockSpec((1, C_WIDTH), lambda i: (0, 0))
    return pl.pallas_call(
        functools.partial(_conv_kernel, n_ctx_tiles=n_ctx_tiles),
        grid=(n_tok // tm,),
        in_specs=[pl.BlockSpec((tm, 2 * C_WIDTH), lambda i: (i, 0)),
                  pl.BlockSpec((C_CONV, C_WIDTH), lambda i: (0, 0)), row, row, row],
        out_specs=pl.BlockSpec((tm, C_WIDTH), lambda i: (i, 0)),
        out_shape=jax.ShapeDtypeStruct((n_tok, C_WIDTH), F32),
        compiler_params=_cparams("parallel"),
        name="conv_mix",
    )(pc, cw, cb, lg, lb)


def _merge_kernel(y_ref, bv_ref, g_ref, yb_ref, yc_ref, pg_ref, x_ref, mod_ref, gng_ref, gnb_ref,
                  n2g_ref, wa_ref, wb_ref, wc_ref, wo_ref, ones_ref, xo_ref, h2_ref):
    d = D_MODEL
    ones_bd = ones_ref[...]
    y = y_ref[...]
    mean = _segsum(y, ones_bd) * (1.0 / A_HEAD)
    yc = y - mean
    var = _segsum(yc * yc, ones_bd) * (1.0 / A_HEAD)
    yn = yc * lax.rsqrt(var + GN_EPS) * gng_ref[...] + gnb_ref[...]
    ya = (yn + bv_ref[...]) * g_ref[...]
    merged = (_sigmoid(pg_ref[:, 0:d]) * _dot(ya.astype(BF16), wa_ref[...])
              + _sigmoid(pg_ref[:, d:2 * d]) * _dot(yb_ref[...].astype(BF16), wb_ref[...])
              + _sigmoid(pg_ref[:, 2 * d:3 * d]) * _dot(yc_ref[...].astype(BF16), wc_ref[...]))
    gt1 = mod_ref[0, :, 2 * d:3 * d]
    xn = x_ref[...] + gt1 * _dot(merged.astype(BF16), wo_ref[...])
    xo_ref[...] = xn
    ms = jnp.mean(xn * xn, axis=-1, keepdims=True)
    h2 = xn * lax.rsqrt(ms + EPS) * n2g_ref[...]
    sh2 = mod_ref[0, :, 3 * d:4 * d]
    sc2 = mod_ref[0, :, 4 * d:5 * d]
    h2_ref[...] = (h2 * (1.0 + sc2) + sh2).astype(BF16)


def _merge(y, bv, g, yb, yc, pg, x, mod3, mod_map, tm, gng, gnb, n2g, wa, wb, wc, wo, ones_bd):
    n_tok = x.shape[0]
    d = D_MODEL
    tile = lambda n: pl.BlockSpec((tm, n), lambda i: (i, 0))
    row = pl.BlockSpec((1, d), lambda i: (0, 0))
    full = lambda a: pl.BlockSpec((a, d), lambda i: (0, 0))
    return pl.pallas_call(
        _merge_kernel,
        grid=(n_tok // tm,),
        in_specs=[tile(d), tile(d), tile(d), tile(B_WIDTH), tile(C_WIDTH), tile(3 * d), tile(d),
                  pl.BlockSpec((1, 1, 6 * d), mod_map), row, row, row,
                  full(d), full(B_WIDTH), full(C_WIDTH), full(d), full(d)],
        out_specs=[tile(d), tile(d)],
        out_shape=[jax.ShapeDtypeStruct((n_tok, d), F32), jax.ShapeDtypeStruct((n_tok, d), BF16)],
        compiler_params=_cparams("parallel"),
        name="merge_outproj",
    )(y, bv, g, yb, yc, pg, x, mod3, gng, gnb, n2g, wa, wb, wc, wo, ones_bd)


NO_RANK = 1e9


def _peer_score_kernel(h_ref, wq_ref, keys_ref, rk0_o, e0_o, th1_o, e1_o):
    tmp = h_ref.shape[0]
    q_t = lax.dot_general(wq_ref[...], h_ref[...], (((1,), (1,)), ((), ())),
                          preferred_element_type=F32).astype(BF16)
    neg = jnp.float32(-jnp.inf)
    row16 = lax.broadcasted_iota(jnp.int32, (P_TOPK, tmp), 0)

    def top16(s):
        vals = jnp.full((P_TOPK, tmp), neg, F32)
        rank = jnp.full(s.shape, NO_RANK, F32)
        for it in range(P_TOPK):
            m = jnp.max(s, axis=0, keepdims=True)
            vals = jnp.where(row16 == it, m, vals)
            hit = s >= m
            rank = jnp.where(hit, float(it), rank)
            s = jnp.where(hit, neg, s)
        return vals, rank

    for h in range(P_HEADS):
        s = []
        for p in range(2):
            hp = 2 * h + p
            s.append(_dot(keys_ref[hp], q_t[hp * P_HALF:(hp + 1) * P_HALF, :]))
        a16, rank0 = top16(s[0])
        b16, rank1 = top16(s[1])
        cands = [a16[r:r + 1, :] + b16 for r in range(P_TOPK)]
        cmax = a16[0:1, :] + b16[0:1, :]
        work = list(cands)
        tau = cmax
        for it in range(P_TOPK):
            m = work[0]
            for c in work[1:]:
                m = jnp.maximum(m, c)
            tau = jnp.max(m, axis=0, keepdims=True)
            if it + 1 < P_TOPK:
                work = [jnp.where(c >= tau, neg, c) for c in work]
        z = jnp.zeros((1, tmp), F32)
        count = jnp.zeros((P_TOPK, tmp), F32)
        for c in cands:
            sel = c >= tau
            z = z + jnp.sum(jnp.where(sel, jnp.exp(c - cmax), 0.0), axis=0, keepdims=True)
            count = count + jnp.where(sel, 1.0, 0.0)
        th1 = jnp.zeros(rank1.shape, F32)
        for q in range(P_TOPK):
            th1 = jnp.where(rank1 == float(q), count[q:q + 1, :], th1)
        rk0_o[h] = rank0
        th1_o[h] = th1.astype(BF16)
        e0_o[h] = jnp.exp(s[0] - a16[0:1, :]) / z
        e1_o[h] = jnp.exp(s[1] - b16[0:1, :]).astype(BF16)


def _peer_scores(h2, wq_t, keys):
    n_tok = h2.shape[0]
    tmp = TM_PEER
    big = pl.BlockSpec((P_HEADS, P_KEYS, tmp), lambda i: (0, 0, i))
    big_shape = jax.ShapeDtypeStruct((P_HEADS, P_KEYS, n_tok), F32)
    narrow_shape = jax.ShapeDtypeStruct((P_HEADS, P_KEYS, n_tok), BF16)
    return pl.pallas_call(
        _peer_score_kernel,
        grid=(n_tok // tmp,),
        in_specs=[pl.BlockSpec((tmp, D_MODEL), lambda i: (i, 0)),
                  pl.BlockSpec((2 * P_HEADS * P_HALF, D_MODEL), lambda i: (0, 0)),
                  pl.BlockSpec((2 * P_HEADS, P_KEYS, P_HALF), lambda i: (0, 0, 0))],
        out_specs=[big, big, big, big],
        out_shape=[big_shape, big_shape, narrow_shape, narrow_shape],
        compiler_params=_cparams("parallel"),
        name="peer_scores",
    )(h2, wq_t, keys)


def _peer_expert_kernel(h_ref, u_ref, vt_ref, rk0_ref, e0_ref, th1_ref, e1_ref, x_ref,
                        mod_ref, o_ref, acc_ref, hid_ref, gated_ref):
    e = pl.program_id(1)
    tmp = h_ref.shape[0]

    @pl.when(e == 0)
    def _():
        acc_ref[...] = jnp.zeros_like(acc_ref)

    hid_ref[...] = lax.dot_general(u_ref[...], h_ref[...], (((1,), (1,)), ((), ())),
                                   preferred_element_type=F32)
    for il in range(PEER_EB // P_KEYS):
        rows = slice(il * P_KEYS, (il + 1) * P_KEYS)
        for lc in range(tmp // LANES):
            ls = slice(lc * LANES, (lc + 1) * LANES)
            w = None
            for h in range(P_HEADS):
                rk0 = jnp.broadcast_to(rk0_ref[h, pl.ds(il, 1), ls], (P_KEYS, LANES)).astype(BF16)
                e0 = jnp.broadcast_to(e0_ref[h, pl.ds(il, 1), ls], (P_KEYS, LANES)).astype(BF16)
                term = jnp.where(rk0 < th1_ref[h, :, ls], e0 * e1_ref[h, :, ls], jnp.zeros((), BF16))
                w = term if w is None else w + term
            hid = hid_ref[rows, ls]
            act = 0.5 * hid * (1.0 + lax.erf(hid * (2.0 ** -0.5)))
            gated_ref[rows, ls] = w * act.astype(BF16)
    acc_ref[...] += _dot(vt_ref[...], gated_ref[...])

    @pl.when(e == pl.num_programs(1) - 1)
    def _():
        gt2 = mod_ref[0, :, 5 * D_MODEL:6 * D_MODEL]
        o_ref[...] = x_ref[...] + gt2 * acc_ref[...].T


def _peer_experts(h2, u_bf, vt_bf, rk0, e0, th1, e1, x, mod3, mod_map):
    n_tok = h2.shape[0]
    tmp = TM_PEER
    n_i = PEER_EB // P_KEYS
    sel = pl.BlockSpec((P_HEADS, n_i, tmp), lambda i, e: (0, e, i))
    full = pl.BlockSpec((P_HEADS, P_KEYS, tmp), lambda i, e: (0, 0, i))
    mod_map2 = lambda i, e: mod_map(i)
    return pl.pallas_call(
        _peer_expert_kernel,
        grid=(n_tok // tmp, P_EXPERTS // PEER_EB),
        in_specs=[pl.BlockSpec((tmp, D_MODEL), lambda i, e: (i, 0)),
                  pl.BlockSpec((PEER_EB, D_MODEL), lambda i, e: (e, 0)),
                  pl.BlockSpec((D_MODEL, PEER_EB), lambda i, e: (0, e)),
                  sel, sel, full, full,
                  pl.BlockSpec((tmp, D_MODEL), lambda i, e: (i, 0)),
                  pl.BlockSpec((1, 1, 6 * D_MODEL), mod_map2)],
        out_specs=pl.BlockSpec((tmp, D_MODEL), lambda i, e: (i, 0)),
        out_shape=jax.ShapeDtypeStruct((n_tok, D_MODEL), F32),
        scratch_shapes=[pltpu.VMEM((D_MODEL, tmp), F32), pltpu.VMEM((PEER_EB, tmp), F32),
                        pltpu.VMEM((PEER_EB, tmp), BF16)],
        compiler_params=_cparams("parallel", "arbitrary"),
        name="peer_experts",
    )(h2, u_bf, vt_bf, rk0, e0, th1, e1, x, mod3)


def _final_norm_kernel(x_ref, g_ref, o_ref):
    x = x_ref[...]
    ms = jnp.mean(x * x, axis=-1, keepdims=True)
    o_ref[...] = x * lax.rsqrt(ms + EPS) * g_ref[...]


def _final_norm(x, g):
    n_tok = x.shape[0]
    return pl.pallas_call(
        _final_norm_kernel,
        grid=(n_tok // TM_PROJ,),
        in_specs=[pl.BlockSpec((TM_PROJ, D_MODEL), lambda i: (i, 0)),
                  pl.BlockSpec((1, D_MODEL), lambda i: (0, 0))],
        out_specs=pl.BlockSpec((TM_PROJ, D_MODEL), lambda i: (i, 0)),
        out_shape=jax.ShapeDtypeStruct((n_tok, D_MODEL), F32),
        compiler_params=_cparams("parallel"),
        name="final_norm",
    )(x, g)


def _block_diag2(w):
    z = jnp.zeros_like(w[0])
    return jnp.concatenate([jnp.concatenate([w[0], z], axis=1),
                            jnp.concatenate([z, w[1]], axis=1)], axis=0)


def kernel(x_prompt, x_sample, state_rwkv, c, c_ctx, ada_w, ada_b, norm1_g, norm2_g, w_in, shift_mu,
           decay_w0, decay_w2, iclr_a0, iclr_a2, gate_g2, k_k, k_a, r_k, gn_g, gn_b, w_a_out,
           pool_w, pool_b, pool_scale, w_b_out, conv_w, conv_b, cln_g, cln_b, w_c_out, w_out,
           peer_wq, peer_keys, peer_u, peer_v, final_g):
    d = D_MODEL
    batch, seq, _ = x_prompt.shape
    dec_batch, dec_seq, _ = x_sample.shape
    depth = ada_w.shape[0]
    n_ctx = batch * seq
    n_dec = dec_batch * dec_seq
    tm = seq
    assert n_ctx % dec_seq == 0 and dec_seq % tm == 0 and dec_seq % GRID_W == 0
    assert n_ctx % TM_PROJ == 0 and dec_seq % TM_PROJ == 0 and dec_seq % TM_PEER == 0
    assert n_ctx % TM_PEER == 0 and tm % GRID_W == 0 and dec_batch <= 7

    x = jnp.concatenate([x_prompt.reshape(n_ctx, d), x_sample.reshape(n_dec, d)], axis=0)
    n_tok = n_ctx + n_dec

    cond8 = jnp.zeros((8, d), F32).at[0].set(c_ctx).at[1:1 + dec_batch].set(c)
    mod_all = _modulation(cond8, ada_w, ada_b)

    seg_ids = jnp.arange(d) // A_HEAD
    ones_bd = (seg_ids[:, None] == seg_ids[None, :]).astype(BF16)

    map_proj = _mod_row_map(n_ctx // TM_PROJ, dec_seq // TM_PROJ)
    map_tm = _mod_row_map(n_ctx // tm, dec_seq // tm)
    map_peer = _mod_row_map(n_ctx // TM_PEER, dec_seq // TM_PEER)

    n_tiles = n_tok // tm
    tile_idx = jnp.arange(n_tiles)
    dec_tile = (tile_idx - n_ctx // tm) % (dec_seq // tm)
    is_start = jnp.where(tile_idx < n_ctx // tm, True, dec_tile == 0)[:, None]
    is_end = jnp.where(tile_idx < n_ctx // tm, True, dec_tile == dec_seq // tm - 1)[:, None]

    ctx_states = []
    for l in range(depth):
        mod3 = mod_all[l][:, None, :]
        w_in_bf = w_in[l].astype(BF16)
        (pa,) = _inproj(x, mod3, norm1_g[l][None], w_in_bf[:, :A_COLS], (A_COLS,), map_proj, "inproj_a")
        pb, pc, pg = _inproj(x, mod3, norm1_g[l][None], w_in_bf[:, A_COLS:],
                             (B_WIDTH, 2 * C_WIDTH, 3 * d), map_proj, "inproj_bcg")

        pa3 = pa.reshape(n_tiles, tm, A_COLS)
        zero_row = jnp.zeros((1, A_COLS), F32)
        halo_prev = jnp.concatenate([zero_row, pa3[:-1, tm - 1, :]], axis=0)
        halo_next = jnp.concatenate([pa3[1:, 0, :], zero_row], axis=0)
        halo_prev = jnp.where(is_start, 0.0, halo_prev)[:, None, :]
        halo_next = jnp.where(is_end, 0.0, halo_next)[:, None, :]
        prep_params = (shift_mu[l][None], decay_w0[l].reshape(1, 2 * d),
                       _block_diag2(decay_w2[l]).astype(BF16), iclr_a0[l].reshape(1, 2 * d),
                       _block_diag2(iclr_a2[l]).astype(BF16), gate_g2[l].astype(BF16),
                       k_k[l][None], k_a[l][None], r_k[l].reshape(1, d), ones_bd)
        r, v, kk, dec, kd, bb, bv, g = _rwkv_prep(pa, halo_prev, halo_next, tm, prep_params)

        y_scan = None
        for seq_len, batch0, nb, s0 in ((seq, 0, batch, None),
                                        (dec_seq, n_ctx // dec_seq, dec_batch, state_rwkv[:, l])):
            vs = _lane_plan(nb)[1]
            if s0 is None:
                s0_c = jnp.zeros((2, A_HEAD, A_HEAD // vs, vs * nb * A_HEADS), F32)
            else:
                s0_c = _state_to_chain(s0, vs)
            lay = lambda a, value_rows=False: _to_chain(a, seq_len, batch0, nb, value_rows)
            y_fwd, y_bwd, sfin_c = _rwkv_scan(lay(kk)[0], lay(r)[0], lay(dec), lay(bb), lay(kd),
                                              lay(v, True)[0], s0_c)
            y_scan = _from_chain(y_fwd, y_bwd, y_scan, n_tok, seq_len, batch0, nb)
            if s0 is None:
                ctx_states.append(_state_from_chain(sfin_c, nb, vs))

        yb = _pool_mix(pb, n_ctx, seq, dec_seq, pool_w[l].astype(BF16), pool_b[l].reshape(1, B_WIDTH),
                       pool_scale[l][None])
        yc = _conv_mix(pc, tm, n_ctx // tm, conv_w[l], conv_b[l][None], cln_g[l][None], cln_b[l][None])

        x, h2 = _merge(y_scan, bv, g, yb, yc, pg, x, mod3, map_tm, tm, gn_g[l][None], gn_b[l][None],
                       norm2_g[l][None], w_a_out[l].astype(BF16), w_b_out[l].astype(BF16),
                       w_c_out[l].astype(BF16), w_out[l].astype(BF16), ones_bd)

        wq_t = peer_wq[l].T.astype(BF16)
        keys = peer_keys[l].reshape(2 * P_HEADS, P_KEYS, P_HALF).astype(BF16)
        rk0, e0p, th1, e1p = _peer_scores(h2, wq_t, keys)
        x = _peer_experts(h2, peer_u[l].astype(BF16), peer_v[l].T.astype(BF16), rk0, e0p, th1, e1p,
                          x, mod3, map_peer)

    y = _final_norm(x, final_g[None])
    new_state = jnp.stack(ctx_states, axis=1)
    return (y[:n_ctx].reshape(batch, seq, d), y[n_ctx:].reshape(dec_batch, dec_seq, d), new_state)
```

```python
import functools

import jax
import jax.numpy as jnp
from jax import lax
from jax.experimental import pallas as pl
from jax.experimental.pallas import tpu as pltpu

F32 = jnp.float32
BF16 = jnp.bfloat16

D_MODEL = 1024
A_HEADS = 16
A_HEAD = 64
A_LORA = 384
A_COLS = 3 * D_MODEL + A_LORA
B_WIDTH = 512
B_GROUP = 128
POOL_WINDOWS = (2, 4, 8, 16)
POOL_PAD = 16
C_WIDTH = 512
C_CONV = 31
CONV_PAD = 16
GRID_W = 64
P_HEADS = 8
P_KEYS = 128
P_TOPK = 16
P_HALF = 128
P_EXPERTS = P_KEYS * P_KEYS
EPS = 1e-6
GN_EPS = 64e-5
LN_EPS = 1e-5

LANES = 128
VMEM_LIMIT = 56 * 1024 * 1024

TM_PROJ = 512
TM_PEER = 512
PEER_EB = 1024
SCAN_TB = 16


def _cparams(*sem):
    return pltpu.CompilerParams(dimension_semantics=sem, vmem_limit_bytes=VMEM_LIMIT)


def _sigmoid(x):
    return 1.0 / (1.0 + jnp.exp(-x))


def _dot(a, b):
    return jnp.dot(a, b, preferred_element_type=F32)


def _segsum(x, ones_bd):
    hi = x.astype(BF16)
    lo = (x - hi.astype(F32)).astype(BF16)
    return _dot(hi, ones_bd) + _dot(lo, ones_bd)


def _mod_kernel(c_ref, w_ref, b_ref, o_ref):
    c = c_ref[...]
    s = c * _sigmoid(c)
    o_ref[...] = jnp.dot(s, w_ref[...], preferred_element_type=F32,
                         precision=lax.Precision.HIGHEST) + b_ref[...]


def _modulation(cond8, ada_w, ada_b):
    L = ada_w.shape[0]
    nb = 6
    return pl.pallas_call(
        _mod_kernel,
        grid=(L, nb),
        in_specs=[pl.BlockSpec((8, D_MODEL), lambda l, j: (0, 0)),
                  pl.BlockSpec((None, D_MODEL, D_MODEL), lambda l, j: (l, 0, j)),
                  pl.BlockSpec((None, 1, D_MODEL), lambda l, j: (l, 0, j))],
        out_specs=pl.BlockSpec((None, 8, D_MODEL), lambda l, j: (l, 0, j)),
        out_shape=jax.ShapeDtypeStruct((L, 8, 6 * D_MODEL), F32),
        compiler_params=_cparams("parallel", "parallel"),
        name="adaln_mod",
    )(cond8, ada_w, ada_b.reshape(L, 1, 6 * D_MODEL))


def _mod_row_map(n_ctx_tiles, tiles_per_dec_seq):
    def index_map(i):
        row = jnp.where(i < n_ctx_tiles, 0, 1 + (i - n_ctx_tiles) // tiles_per_dec_seq)
        return (row, 0, 0)
    return index_map


def _inproj_kernel(x_ref, mod_ref, g_ref, w_ref, *o_refs):
    x = x_ref[...]
    ms = jnp.mean(x * x, axis=-1, keepdims=True)
    y = x * lax.rsqrt(ms + EPS) * g_ref[...]
    sh = mod_ref[0, :, 0:D_MODEL]
    sc = mod_ref[0, :, D_MODEL:2 * D_MODEL]
    h = (y * (1.0 + sc) + sh).astype(BF16)
    off = 0
    for o_ref in o_refs:
        n = o_ref.shape[1]
        o_ref[...] = _dot(h, w_ref[:, off:off + n])
        off += n


def _inproj(x, mod3, g, w, splits, mod_map, name):
    n_tok = x.shape[0]
    n_out = w.shape[1]
    return pl.pallas_call(
        _inproj_kernel,
        grid=(n_tok // TM_PROJ,),
        in_specs=[pl.BlockSpec((TM_PROJ, D_MODEL), lambda i: (i, 0)),
                  pl.BlockSpec((1, 1, 6 * D_MODEL), mod_map),
                  pl.BlockSpec((1, D_MODEL), lambda i: (0, 0)),
                  pl.BlockSpec((D_MODEL, n_out), lambda i: (0, 0))],
        out_specs=[pl.BlockSpec((TM_PROJ, n), lambda i: (i, 0)) for n in splits],
        out_shape=[jax.ShapeDtypeStruct((n_tok, n), F32) for n in splits],
        compiler_params=_cparams("parallel"),
        name=name,
    )(x, mod3, g, w)


def _prep_kernel(pa_ref, hp_ref, hn_ref, mu_ref, w0_ref, w2_ref, a0_ref, a2_ref, g2_ref,
                 kkw_ref, ka_ref, rk_ref, ones_ref,
                 r_o, v_o, kk_o, dec_o, kd_o, bb_o, bv_o, g_o):
    tm = pa_ref.shape[0]

    def shifted(c0, c1):
        x = pa_ref[:, c0:c1]
        row = lax.broadcasted_iota(jnp.int32, x.shape, 0)
        prev = jnp.where(row == 0, hp_ref[0, :, c0:c1], pltpu.roll(x, 1, 0))
        nxt = jnp.where(row == tm - 1, hn_ref[0, :, c0:c1], pltpu.roll(x, tm - 1, 0))
        return x + (0.5 * (prev + nxt) - x) * mu_ref[:, c0:c1]

    d = D_MODEL
    r = shifted(0, d)
    k = shifted(d, 2 * d)
    v = shifted(2 * d, 3 * d)
    lo = shifted(3 * d, 3 * d + A_LORA)
    w_lo = lo[:, 0:128]
    a_lo = lo[:, 128:256]
    g_lo = lo[:, 256:384]
    ones_bd = ones_ref[...]

    r_o[...] = r
    v_o[...] = v

    kkraw = k * kkw_ref[...]
    nrm = jnp.sqrt(_segsum(kkraw * kkraw, ones_bd))
    kk = kkraw / jnp.maximum(nrm, 1e-12)
    kk_o[...] = kk

    z = -(w0_ref[...] + _dot(jnp.tanh(w_lo).astype(BF16), w2_ref[...]))
    softplus = jnp.maximum(z, 0.0) + jnp.log(1.0 + jnp.exp(-jnp.abs(z)))
    dec_o[...] = jnp.exp(-jnp.exp(-softplus - 0.5))

    a = _sigmoid(a0_ref[...] + _dot(a_lo.astype(BF16), a2_ref[...]))
    ka = ka_ref[...]
    kd0 = k * (1.0 + (a[:, 0:d] - 1.0) * ka)
    kd1 = k * (1.0 + (a[:, d:2 * d] - 1.0) * ka)
    kd_o[:, 0:d] = kd0
    kd_o[:, d:2 * d] = kd1
    bb_o[:, 0:d] = kk * a[:, 0:d]
    bb_o[:, d:2 * d] = kk * a[:, d:2 * d]

    bonus = _segsum(r * (kd0 + kd1) * rk_ref[...], ones_bd)
    bv_o[...] = bonus * v
    g_o[...] = _dot(_sigmoid(g_lo).astype(BF16), g2_ref[...])


def _rwkv_prep(pa, halo_prev, halo_next, tm, params):
    n_tok = pa.shape[0]
    d = D_MODEL
    row = lambda n: pl.BlockSpec((1, n), lambda i: (0, 0))
    full = lambda a, b: pl.BlockSpec((a, b), lambda i: (0, 0))
    tile = lambda n: pl.BlockSpec((tm, n), lambda i: (i, 0))
    halo = pl.BlockSpec((1, 1, A_COLS), lambda i: (i, 0, 0))
    outs = [d, d, d, 2 * d, 2 * d, 2 * d, d, d]
    return pl.pallas_call(
        _prep_kernel,
        grid=(n_tok // tm,),
        in_specs=[tile(A_COLS), halo, halo, row(A_COLS), row(2 * d), full(128, 2 * d), row(2 * d),
                  full(128, 2 * d), full(128, d), row(d), row(d), row(d), full(d, d)],
        out_specs=[tile(n) for n in outs],
        out_shape=[jax.ShapeDtypeStruct((n_tok, n), F32) for n in outs],
        compiler_params=_cparams("parallel"),
        name="rwkv_prep",
    )(pa, halo_prev, halo_next, *params)


SCAN_ROWS = 16


def _scan_kernel(kkf, kkb, rf, rb, wf, wb, bf, bb, kdf, kdb, vf, vb, s0_ref,
                 yf_ref, yb_ref, sfin_ref, s_scr):
    nk, tb = kkf.shape[0], kkf.shape[1]
    j = pl.program_id(2)

    @pl.when(j == 0)
    def _():
        s_scr[...] = s0_ref[...]

    def advance(d, t, rs, kk_ref, r_ref, w_ref, b_ref, kd_ref, v_ref, y_ref):
        parts = [None] * 4
        for k in range(nk):
            term = s_scr[d, k, rs, :] * kk_ref[k, pl.ds(t, 1), :]
            parts[k % 4] = term if parts[k % 4] is None else parts[k % 4] + term
        sa = -((parts[0] + parts[1]) + (parts[2] + parts[3]))
        vt = v_ref[t, rs, :]
        ys = [None] * 2
        for k in range(nk):
            s_new = (s_scr[d, k, rs, :] * w_ref[k, pl.ds(t, 1), :] + sa * b_ref[k, pl.ds(t, 1), :]
                     + vt * kd_ref[k, pl.ds(t, 1), :])
            s_scr[d, k, rs, :] = s_new
            term = s_new * r_ref[k, pl.ds(t, 1), :]
            ys[k % 2] = term if ys[k % 2] is None else ys[k % 2] + term
        y_ref[t, rs, :] = ys[0] + ys[1]

    rs = slice(None)

    def step(i, carry):
        advance(0, i, rs, kkf, rf, wf, bf, kdf, vf, yf_ref)
        advance(1, tb - 1 - i, rs, kkb, rb, wb, bb, kdb, vb, yb_ref)
        return carry

    lax.fori_loop(0, tb, step, 0)

    @pl.when(j == pl.num_programs(2) - 1)
    def _():
        sfin_ref[...] = s_scr[...]


def _rwkv_scan(kk_c, r_c, w_c, b_c, kd_c, v_c, s0_c):
    nk, T, L = kk_c.shape
    vq = v_c.shape[1]
    tb = min(SCAN_TB, T)
    nt = T // tb
    fwd = lambda j: j
    bwd = lambda j: nt - 1 - j
    rows = min(vq, SCAN_ROWS)
    shared = lambda tj: pl.BlockSpec((nk, tb, LANES), lambda g, q, j: (0, tj(j), g))
    perdir = lambda d, tj: pl.BlockSpec((None, nk, tb, LANES), lambda g, q, j: (d, 0, tj(j), g))
    vspec = lambda tj: pl.BlockSpec((tb, rows, LANES), lambda g, q, j: (tj(j), q, g))
    sspec = pl.BlockSpec((2, nk, rows, LANES), lambda g, q, j: (0, 0, q, g))
    y_shape = jax.ShapeDtypeStruct((T, vq, L), F32)
    return pl.pallas_call(
        _scan_kernel,
        grid=(L // LANES, vq // rows, nt),
        in_specs=[shared(fwd), shared(bwd), shared(fwd), shared(bwd),
                  perdir(0, fwd), perdir(1, bwd), perdir(0, fwd), perdir(1, bwd),
                  perdir(0, fwd), perdir(1, bwd), vspec(fwd), vspec(bwd), sspec],
        out_specs=[vspec(fwd), vspec(bwd), sspec],
        out_shape=[y_shape, y_shape, jax.ShapeDtypeStruct((2, nk, vq, L), F32)],
        scratch_shapes=[pltpu.VMEM((2, nk, rows, LANES), F32)],
        compiler_params=_cparams("parallel", "parallel", "arbitrary"),
        name="rwkv_scan",
    )(kk_c, kk_c, r_c, r_c, w_c, w_c, b_c, b_c, kd_c, kd_c, v_c, v_c, s0_c)


def _lane_plan(n_batch):
    full = LANES // A_HEADS
    if n_batch >= full:
        assert n_batch % full == 0
        return full, 1
    assert full % n_batch == 0
    return n_batch, full // n_batch


def _to_chain_kernel(x_ref, o_ref, scr, *, vs, value_rows):
    nb, tc = x_ref.shape[0], x_ref.shape[1]
    for b in range(nb):
        scr[b * A_HEADS:(b + 1) * A_HEADS] = x_ref[b].T.reshape(A_HEADS, A_HEAD, tc)
    vq = A_HEAD // vs
    if value_rows:
        for v in range(vq):
            rows = [scr[:, vh * vq + v, :] for vh in range(vs)]
            rows = rows[0] if vs == 1 else jnp.concatenate(rows, axis=0)
            o_ref[:, v, :] = rows.T
    else:
        for k in range(A_HEAD):
            rows = scr[:, k, :]
            if vs > 1:
                rows = jnp.concatenate([rows] * vs, axis=0)
            o_ref[k] = rows.T


def _to_chain(x, seq_len, batch0, n_batch, value_rows):
    n_col = x.shape[1] // D_MODEL
    nb, vs = _lane_plan(n_batch)
    assert batch0 % nb == 0 and seq_len % LANES == 0
    groups = n_batch // nb
    tc = LANES
    x3 = x.reshape(x.shape[0] // seq_len, seq_len, n_col * D_MODEL)
    vq = A_HEAD // vs
    if value_rows:
        out_spec = pl.BlockSpec((None, tc, vq, LANES), lambda c, g, j: (c, j, 0, g))
        out_shape = jax.ShapeDtypeStruct((n_col, seq_len, vq, groups * LANES), F32)
    else:
        out_spec = pl.BlockSpec((None, A_HEAD, tc, LANES), lambda c, g, j: (c, 0, j, g))
        out_shape = jax.ShapeDtypeStruct((n_col, A_HEAD, seq_len, groups * LANES), F32)
    return pl.pallas_call(
        functools.partial(_to_chain_kernel, vs=vs, value_rows=value_rows),
        grid=(n_col, groups, seq_len // tc),
        in_specs=[pl.BlockSpec((nb, tc, D_MODEL), lambda c, g, j: (batch0 // nb + g, j, c))],
        out_specs=out_spec,
        out_shape=out_shape,
        scratch_shapes=[pltpu.VMEM((nb * A_HEADS, A_HEAD, tc), F32)],
        compiler_params=_cparams("parallel", "parallel", "parallel"),
        name="to_chain_v" if value_rows else "to_chain_k",
    )(x3)


def _from_chain_kernel(*refs, vs):
    yf_ref, yb_ref, o_ref, scr = refs[-4:]
    nb, tc = o_ref.shape[0], o_ref.shape[1]
    vq = A_HEAD // vs
    rows = nb * A_HEADS
    for v in range(vq):
        yv = (yf_ref[:, v, :] + yb_ref[:, v, :]).T
        for vh in range(vs):
            scr[:, vh * vq + v, :] = yv[vh * rows:(vh + 1) * rows]
    for b in range(nb):
        o_ref[b] = scr[b * A_HEADS:(b + 1) * A_HEADS].reshape(D_MODEL, tc).T


def _from_chain(y_fwd, y_bwd, prev_rows, n_tok, seq_len, batch0, n_batch):
    nb, vs = _lane_plan(n_batch)
    groups = n_batch // nb
    tc = LANES
    vq = A_HEAD // vs
    in_specs = [pl.BlockSpec((tc, vq, LANES), lambda g, j: (j, 0, g))] * 2
    args = [y_fwd, y_bwd]
    aliases = {}
    if prev_rows is not None:
        in_specs = [pl.BlockSpec(memory_space=pl.ANY)] + in_specs
        args = [prev_rows.reshape(n_tok // seq_len, seq_len, D_MODEL)] + args
        aliases = {0: 0}
    out = pl.pallas_call(
        functools.partial(_from_chain_kernel, vs=vs),
        grid=(groups, seq_len // tc),
        in_specs=in_specs,
        out_specs=pl.BlockSpec((nb, tc, D_MODEL), lambda g, j: (batch0 // nb + g, j, 0)),
        out_shape=jax.ShapeDtypeStruct((n_tok // seq_len, seq_len, D_MODEL), F32),
        scratch_shapes=[pltpu.VMEM((nb * A_HEADS, A_HEAD, tc), F32)],
        input_output_aliases=aliases,
        compiler_params=_cparams("parallel", "parallel"),
        name="from_chain",
    )(*args)
    return out.reshape(n_tok, D_MODEL)


def _state_to_chain(s, vs):
    B = s.shape[0]
    vq = A_HEAD // vs
    s = jnp.transpose(s, (1, 4, 3, 0, 2)).reshape(2, A_HEAD, vs, vq, B * A_HEADS)
    return jnp.transpose(s, (0, 1, 3, 2, 4)).reshape(2, A_HEAD, vq, vs * B * A_HEADS)


def _state_from_chain(s_c, n_batch, vs):
    vq = A_HEAD // vs
    s = s_c.reshape(2, A_HEAD, vq, vs, n_batch, A_HEADS)
    return jnp.transpose(s, (4, 0, 5, 3, 2, 1)).reshape(n_batch, 2, A_HEADS, A_HEAD, A_HEAD)


def _pool_tail(d2, g, pw_ref, pb_ref, ps_ref):
    sl = slice(g * B_GROUP, (g + 1) * B_GROUP)
    y = _dot(d2.astype(BF16), pw_ref[g]) + pb_ref[:, sl]
    return y * ps_ref[:, sl]


def _window_count(t, w, n):
    lo = jnp.clip(t - w // 2, 0, n)
    hi = jnp.clip(t - w // 2 + w, 0, n)
    return (hi - lo).astype(F32)


def _pool_seq_kernel(z_ref, pw_ref, pb_ref, ps_ref, o_ref):
    n = z_ref.shape[0]
    pad = jnp.zeros((POOL_PAD, B_GROUP), F32)
    t = lax.broadcasted_iota(jnp.int32, (n, B_GROUP), 0)
    for g, w in enumerate(POOL_WINDOWS):
        sl = slice(g * B_GROUP, (g + 1) * B_GROUP)
        z = z_ref[:, sl]
        zp = jnp.concatenate([pad, z, pad], axis=0)
        acc = None
        for j in range(w):
            off = j - w // 2
            term = zp if off == 0 else pltpu.roll(zp, (-off) % (n + 2 * POOL_PAD), 0)
            acc = term if acc is None else acc + term
        d = acc[POOL_PAD:POOL_PAD + n] / _window_count(t, w, n) - z
        o_ref[:, sl] = _pool_tail(d, g, pw_ref, pb_ref, ps_ref)


def _pool_grid_kernel(z_ref, pw_ref, pb_ref, ps_ref, _, o_ref):
    rows, cb = z_ref.shape[0], z_ref.shape[1]
    pad = jnp.zeros((POOL_PAD, cb, B_GROUP), F32)
    t = lax.broadcasted_iota(jnp.int32, (rows, cb, B_GROUP), 0)
    for g, w in enumerate(POOL_WINDOWS):
        sl = slice(g * B_GROUP, (g + 1) * B_GROUP)
        z = z_ref[:, :, sl]
        zp = jnp.concatenate([pad, z, pad], axis=0)
        acc = None
        for j in range(w):
            s = POOL_PAD + j - w // 2
            term = zp[s:s + rows]
            acc = term if acc is None else acc + term
        d = acc / _window_count(t, w, rows) - z
        y = _pool_tail(d.reshape(rows * cb, B_GROUP), g, pw_ref, pb_ref, ps_ref)
        o_ref[:, :, sl] = y.reshape(rows, cb, B_GROUP)


def _pool_mix(pb, n_ctx, seq, dec_seq, pw, pbias, pscale):
    n_tok = pb.shape[0]
    wspecs = lambda nd: [pl.BlockSpec((4, B_GROUP, B_GROUP), lambda *a: (0, 0, 0)),
                         pl.BlockSpec((1, B_WIDTH), lambda *a: (0, 0)),
                         pl.BlockSpec((1, B_WIDTH), lambda *a: (0, 0))]
    y_ctx = pl.pallas_call(
        _pool_seq_kernel,
        grid=(n_ctx // seq,),
        in_specs=[pl.BlockSpec((seq, B_WIDTH), lambda i: (i, 0))] + wspecs(1),
        out_specs=pl.BlockSpec((seq, B_WIDTH), lambda i: (i, 0)),
        out_shape=jax.ShapeDtypeStruct((n_tok, B_WIDTH), F32),
        compiler_params=_cparams("parallel"),
        name="pool_seq",
    )(pb, pw, pbias, pscale)
    rows = dec_seq // GRID_W
    n_dec = (n_tok - n_ctx) // dec_seq
    ctx_blocks = n_ctx // dec_seq
    cb = 8
    shape4 = (n_tok // dec_seq, rows, GRID_W, B_WIDTH)
    blk = pl.BlockSpec((None, rows, cb, B_WIDTH), lambda b, c: (b + ctx_blocks, 0, c, 0))
    y_all = pl.pallas_call(
        _pool_grid_kernel,
        grid=(n_dec, GRID_W // cb),
        in_specs=[blk] + wspecs(2) + [pl.BlockSpec(memory_space=pl.ANY)],
        out_specs=blk,
        out_shape=jax.ShapeDtypeStruct(shape4, F32),
        input_output_aliases={4: 0},
        compiler_params=_cparams("parallel", "parallel"),
        name="pool_grid",
    )(pb.reshape(shape4), pw, pbias, pscale, y_ctx.reshape(shape4))
    return y_all.reshape(n_tok, B_WIDTH)


def _conv_kernel(pc_ref, cw_ref, cb_ref, lg_ref, lb_ref, o_ref, *, n_ctx_tiles):
    tm = pc_ref.shape[0]
    u = pc_ref[:, 0:C_WIDTH] * _sigmoid(pc_ref[:, C_WIDTH:2 * C_WIDTH])
    pad = jnp.zeros((CONV_PAD, C_WIDTH), F32)
    up = jnp.concatenate([pad, u, pad], axis=0)
    is_ctx = pl.program_id(0) < n_ctx_tiles
    t = lax.broadcasted_iota(jnp.int32, (tm, C_WIDTH), 0)
    pos = jnp.where(is_ctx, t, t & (GRID_W - 1))
    seg = jnp.where(is_ctx, tm, GRID_W)
    acc = jnp.zeros((tm, C_WIDTH), F32)
    for j in range(C_CONV):
        off = j - C_CONV // 2
        rolled = up if off == 0 else pltpu.roll(up, (-off) % (tm + 2 * CONV_PAD), 0)
        tap = rolled[CONV_PAD:CONV_PAD + tm]
        q = pos + off
        tap = jnp.where(q >= 0, jnp.where(q < seg, tap, 0.0), 0.0)
        acc = acc + tap * cw_ref[pl.ds(j, 1), :]
    c = acc + cb_ref[...]
    mu = jnp.mean(c, axis=-1, keepdims=True)
    cc = c - mu
    var = jnp.mean(cc * cc, axis=-1, keepdims=True)
    y = cc * lax.rsqrt(var + LN_EPS) * lg_ref[...] + lb_ref[...]
    o_ref[...] = y * _sigmoid(y)


def _conv_mix(pc, tm, n_ctx_tiles, cw, cb, lg, lb):
    n_tok = pc.shape[0]
    row = pl.BlockSpec((1, C_WIDTH), lambda i: (0, 0))
    return pl.pallas_call(
        functools.partial(_conv_kernel, n_ctx_tiles=n_ctx_tiles),
        grid=(n_tok // tm,),
        in_specs=[pl.BlockSpec((tm, 2 * C_WIDTH), lambda i: (i, 0)),
                  pl.BlockSpec((C_CONV, C_WIDTH), lambda i: (0, 0)), row, row, row],
        out_specs=pl.BlockSpec((tm, C_WIDTH), lambda i: (i, 0)),
        out_shape=jax.ShapeDtypeStruct((n_tok, C_WIDTH), F32),
        compiler_params=_cparams("parallel"),
        name="conv_mix",
    )(pc, cw, cb, lg, lb)


def _merge_kernel(y_ref, bv_ref, g_ref, yb_ref, yc_ref, pg_ref, x_ref, mod_ref, gng_ref, gnb_ref,
                  n2g_ref, wa_ref, wb_ref, wc_ref, wo_ref, ones_ref, xo_ref, h2_ref):
    d = D_MODEL
    ones_bd = ones_ref[...]
    y = y_ref[...]
    mean = _segsum(y, ones_bd) * (1.0 / A_HEAD)
    yc = y - mean
    var = _segsum(yc * yc, ones_bd) * (1.0 / A_HEAD)
    yn = yc * lax.rsqrt(var + GN_EPS) * gng_ref[...] + gnb_ref[...]
    ya = (yn + bv_ref[...]) * g_ref[...]
    merged = (_sigmoid(pg_ref[:, 0:d]) * _dot(ya.astype(BF16), wa_ref[...])
              + _sigmoid(pg_ref[:, d:2 * d]) * _dot(yb_ref[...].astype(BF16), wb_ref[...])
              + _sigmoid(pg_ref[:, 2 * d:3 * d]) * _dot(yc_ref[...].astype(BF16), wc_ref[...]))
    gt1 = mod_ref[0, :, 2 * d:3 * d]
    xn = x_ref[...] + gt1 * _dot(merged.astype(BF16), wo_ref[...])
    xo_ref[...] = xn
    ms = jnp.mean(xn * xn, axis=-1, keepdims=True)
    h2 = xn * lax.rsqrt(ms + EPS) * n2g_ref[...]
    sh2 = mod_ref[0, :, 3 * d:4 * d]
    sc2 = mod_ref[0, :, 4 * d:5 * d]
    h2_ref[...] = (h2 * (1.0 + sc2) + sh2).astype(BF16)


def _merge(y, bv, g, yb, yc, pg, x, mod3, mod_map, tm, gng, gnb, n2g, wa, wb, wc, wo, ones_bd):
    n_tok = x.shape[0]
    d = D_MODEL
    tile = lambda n: pl.BlockSpec((tm, n), lambda i: (i, 0))
    row = pl.BlockSpec((1, d), lambda i: (0, 0))
    full = lambda a: pl.BlockSpec((a, d), lambda i: (0, 0))
    return pl.pallas_call(
        _merge_kernel,
        grid=(n_tok // tm,),
        in_specs=[tile(d), tile(d), tile(d), tile(B_WIDTH), tile(C_WIDTH), tile(3 * d), tile(d),
                  pl.BlockSpec((1, 1, 6 * d), mod_map), row, row, row,
                  full(d), full(B_WIDTH), full(C_WIDTH), full(d), full(d)],
        out_specs=[tile(d), tile(d)],
        out_shape=[jax.ShapeDtypeStruct((n_tok, d), F32), jax.ShapeDtypeStruct((n_tok, d), BF16)],
        compiler_params=_cparams("parallel"),
        name="merge_outproj",
    )(y, bv, g, yb, yc, pg, x, mod3, gng, gnb, n2g, wa, wb, wc, wo, ones_bd)


NO_RANK = 1e9


def _peer_score_kernel(h_ref, wq_ref, keys_ref, rk0_o, e0_o, th1_o, e1_o, s_scr, top_scr, rank_scr):
    tmp = h_ref.shape[0]
    q_t = lax.dot_general(wq_ref[...], h_ref[...], (((1,), (1,)), ((), ())),
                          preferred_element_type=F32).astype(BF16)
    for hp in range(2 * P_HEADS):
        s_scr[hp] = _dot(keys_ref[hp], q_t[hp * P_HALF:(hp + 1) * P_HALF, :])

    neg = jnp.float32(-jnp.inf)
    half = P_TOPK // 2
    row16 = lax.broadcasted_iota(jnp.int32, (P_TOPK, LANES), 0)
    row8 = lax.broadcasted_iota(jnp.int32, (half, LANES), 0)
    n_groups = tmp // LANES

    def top16_pair(sets):
        vals = [jnp.full((P_TOPK, LANES), neg, F32) for _ in sets]
        ranks = [jnp.full(s.shape, NO_RANK, F32) for s in sets]
        sets = list(sets)
        for it in range(P_TOPK):
            for n in range(len(sets)):
                m = jnp.max(sets[n], axis=0, keepdims=True)
                vals[n] = jnp.where(row16 == it, m, vals[n])
                hit = sets[n] >= m
                ranks[n] = jnp.where(hit, float(it), ranks[n])
                sets[n] = jnp.where(hit, neg, sets[n])
        return vals, ranks

    n_items = P_HEADS * n_groups

    def item(idx):
        return idx // n_groups, pl.ds(pl.multiple_of((idx % n_groups) * LANES, LANES), LANES)

    def extract(idx):
        h, ls = item(idx)
        (a16, b16), (rank0, rank1) = top16_pair((s_scr[2 * h, :, ls], s_scr[2 * h + 1, :, ls]))
        rk0_o[h, :, ls] = rank0
        top_scr[0] = a16
        top_scr[1] = b16
        rank_scr[...] = rank1

    def combine(idx):
        h, ls = item(idx)
        a16 = top_scr[0]
        b16 = top_scr[1]
        rank1 = rank_scr[...]
        s0 = s_scr[2 * h, :, ls]
        s1 = s_scr[2 * h + 1, :, ls]
        b_lo = b16[0:half, :]
        cands = [a16[0:1, :] + b16, a16[1:2, :] + b_lo]
        for r in range(2, half):
            cands.append(jnp.where(row8 < P_TOPK // (r + 1), a16[r:r + 1, :] + b_lo, neg))
        cands.append(a16[half:P_TOPK, :] + b16[0:1, :])
        cmax = a16[0:1, :] + b16[0:1, :]
        work = list(cands)
        tau = cmax
        for it in range(P_TOPK):
            m = jnp.maximum(work[0][0:half, :], work[0][half:P_TOPK, :])
            for c in work[1:]:
                m = jnp.maximum(m, c)
            tau = jnp.max(m, axis=0, keepdims=True)
            if it + 1 < P_TOPK:
                work = [jnp.where(c >= tau, neg, c) for c in work]
        z = jnp.zeros((1, LANES), F32)
        sels = []
        for c in cands:
            sel = c >= tau
            z = z + jnp.sum(jnp.where(sel, jnp.exp(c - cmax), 0.0), axis=0, keepdims=True)
            sels.append(jnp.where(sel, 1.0, 0.0))
        cnt_hi = sels[0][half:P_TOPK, :]
        cnt_lo = sels[0][0:half, :]
        for c in sels[1:half]:
            cnt_lo = cnt_lo + c
        cnt_lo = cnt_lo + jnp.where(row8 == 0, jnp.sum(sels[half], axis=0, keepdims=True), 0.0)
        th1 = jnp.zeros(rank1.shape, F32)
        for q in range(P_TOPK):
            cnt = cnt_lo[q:q + 1, :] if q < half else cnt_hi[q - half:q - half + 1, :]
            th1 = jnp.where(rank1 == float(q), cnt, th1)
        th1_o[h, :, ls] = th1.astype(BF16)
        e0_o[h, :, ls] = jnp.exp(s0 - a16[0:1, :]) / z
        e1_o[h, :, ls] = jnp.exp(s1 - b16[0:1, :]).astype(BF16)

    top_scr[...] = jnp.zeros_like(top_scr)
    rank_scr[...] = jnp.zeros_like(rank_scr)

    def body(n, carry):
        combine(jnp.maximum(n - 1, 0))
        extract(jnp.minimum(n, n_items - 1))
        return carry

    lax.fori_loop(0, n_items + 1, body, 0)


def _peer_scores(h2, wq_t, keys):
    n_tok = h2.shape[0]
    tmp = TM_PEER
    big = pl.BlockSpec((P_HEADS, P_KEYS, tmp), lambda i: (0, 0, i))
    big_shape = jax.ShapeDtypeStruct((P_HEADS, P_KEYS, n_tok), F32)
    narrow_shape = jax.ShapeDtypeStruct((P_HEADS, P_KEYS, n_tok), BF16)
    return pl.pallas_call(
        _peer_score_kernel,
        grid=(n_tok // tmp,),
        in_specs=[pl.BlockSpec((tmp, D_MODEL), lambda i: (i, 0)),
                  pl.BlockSpec((2 * P_HEADS * P_HALF, D_MODEL), lambda i: (0, 0)),
                  pl.BlockSpec((2 * P_HEADS, P_KEYS, P_HALF), lambda i: (0, 0, 0))],
        out_specs=[big, big, big, big],
        out_shape=[big_shape, big_shape, narrow_shape, narrow_shape],
        scratch_shapes=[pltpu.VMEM((2 * P_HEADS, P_KEYS, tmp), F32),
                        pltpu.VMEM((2, P_TOPK, LANES), F32), pltpu.VMEM((P_KEYS, LANES), F32)],
        compiler_params=_cparams("parallel"),
        name="peer_scores",
    )(h2, wq_t, keys)


def _peer_expert_kernel(h_ref, u_ref, vt_ref, rk0_ref, e0_ref, th1_ref, e1_ref, x_ref,
                        mod_ref, o_ref, acc_ref, hid_ref, gated_ref):
    e = pl.program_id(1)
    tmp = h_ref.shape[0]

    @pl.when(e == 0)
    def _():
        acc_ref[...] = jnp.zeros_like(acc_ref)

    hid_ref[...] = lax.dot_general(u_ref[...], h_ref[...], (((1,), (1,)), ((), ())),
                                   preferred_element_type=F32)
    for il in range(PEER_EB // P_KEYS):
        rows = slice(il * P_KEYS, (il + 1) * P_KEYS)
        for lc in range(tmp // LANES):
            ls = slice(lc * LANES, (lc + 1) * LANES)
            w = None
            for h in range(P_HEADS):
                rk0 = jnp.broadcast_to(rk0_ref[h, pl.ds(il, 1), ls], (P_KEYS, LANES)).astype(BF16)
                e0 = jnp.broadcast_to(e0_ref[h, pl.ds(il, 1), ls], (P_KEYS, LANES)).astype(BF16)
                term = jnp.where(rk0 < th1_ref[h, :, ls], e0 * e1_ref[h, :, ls], jnp.zeros((), BF16))
                w = term if w is None else w + term
            hid = hid_ref[rows, ls]
            act = 0.5 * hid * (1.0 + lax.erf(hid * (2.0 ** -0.5)))
            gated_ref[rows, ls] = w * act.astype(BF16)
    acc_ref[...] += _dot(vt_ref[...], gated_ref[...])

    @pl.when(e == pl.num_programs(1) - 1)
    def _():
        gt2 = mod_ref[0, :, 5 * D_MODEL:6 * D_MODEL]
        o_ref[...] = x_ref[...] + gt2 * acc_ref[...].T


def _peer_experts(h2, u_bf, vt_bf, rk0, e0, th1, e1, x, mod3, mod_map):
    n_tok = h2.shape[0]
    tmp = TM_PEER
    n_i = PEER_EB // P_KEYS
    sel = pl.BlockSpec((P_HEADS, n_i, tmp), lambda i, e: (0, e, i))
    full = pl.BlockSpec((P_HEADS, P_KEYS, tmp), lambda i, e: (0, 0, i))
    mod_map2 = lambda i, e: mod_map(i)
    return pl.pallas_call(
        _peer_expert_kernel,
        grid=(n_tok // tmp, P_EXPERTS // PEER_EB),
        in_specs=[pl.BlockSpec((tmp, D_MODEL), lambda i, e: (i, 0)),
                  pl.BlockSpec((PEER_EB, D_MODEL), lambda i, e: (e, 0)),
                  pl.BlockSpec((D_MODEL, PEER_EB), lambda i, e: (0, e)),
                  sel, sel, full, full,
                  pl.BlockSpec((tmp, D_MODEL), lambda i, e: (i, 0)),
                  pl.BlockSpec((1, 1, 6 * D_MODEL), mod_map2)],
        out_specs=pl.BlockSpec((tmp, D_MODEL), lambda i, e: (i, 0)),
        out_shape=jax.ShapeDtypeStruct((n_tok, D_MODEL), F32),
        scratch_shapes=[pltpu.VMEM((D_MODEL, tmp), F32), pltpu.VMEM((PEER_EB, tmp), F32),
                        pltpu.VMEM((PEER_EB, tmp), BF16)],
        compiler_params=_cparams("parallel", "arbitrary"),
        name="peer_experts",
    )(h2, u_bf, vt_bf, rk0, e0, th1, e1, x, mod3)


def _final_norm_kernel(x_ref, g_ref, o_ref):
    x = x_ref[...]
    ms = jnp.mean(x * x, axis=-1, keepdims=True)
    o_ref[...] = x * lax.rsqrt(ms + EPS) * g_ref[...]


def _final_norm(x, g):
    n_tok = x.shape[0]
    return pl.pallas_call(
        _final_norm_kernel,
        grid=(n_tok // TM_PROJ,),
        in_specs=[pl.BlockSpec((TM_PROJ, D_MODEL), lambda i: (i, 0)),
                  pl.BlockSpec((1, D_MODEL), lambda i: (0, 0))],
        out_specs=pl.BlockSpec((TM_PROJ, D_MODEL), lambda i: (i, 0)),
        out_shape=jax.ShapeDtypeStruct((n_tok, D_MODEL), F32),
        compiler_params=_cparams("parallel"),
        name="final_norm",
    )(x, g)


def _block_diag2(w):
    z = jnp.zeros_like(w[0])
    return jnp.concatenate([jnp.concatenate([w[0], z], axis=1),
                            jnp.concatenate([z, w[1]], axis=1)], axis=0)


def kernel(x_prompt, x_sample, state_rwkv, c, c_ctx, ada_w, ada_b, norm1_g, norm2_g, w_in, shift_mu,
           decay_w0, decay_w2, iclr_a0, iclr_a2, gate_g2, k_k, k_a, r_k, gn_g, gn_b, w_a_out,
           pool_w, pool_b, pool_scale, w_b_out, conv_w, conv_b, cln_g, cln_b, w_c_out, w_out,
           peer_wq, peer_keys, peer_u, peer_v, final_g):
    d = D_MODEL
    batch, seq, _ = x_prompt.shape
    dec_batch, dec_seq, _ = x_sample.shape
    depth = ada_w.shape[0]
    n_ctx = batch * seq
    n_dec = dec_batch * dec_seq
    tm = seq
    assert n_ctx % dec_seq == 0 and dec_seq % tm == 0 and dec_seq % GRID_W == 0
    assert n_ctx % TM_PROJ == 0 and dec_seq % TM_PROJ == 0 and dec_seq % TM_PEER == 0
    assert n_ctx % TM_PEER == 0 and tm % GRID_W == 0 and dec_batch <= 7

    x = jnp.concatenate([x_prompt.reshape(n_ctx, d), x_sample.reshape(n_dec, d)], axis=0)
    n_tok = n_ctx + n_dec

    cond8 = jnp.zeros((8, d), F32).at[0].set(c_ctx).at[1:1 + dec_batch].set(c)
    mod_all = _modulation(cond8, ada_w, ada_b)

    seg_ids = jnp.arange(d) // A_HEAD
    ones_bd = (seg_ids[:, None] == seg_ids[None, :]).astype(BF16)

    map_proj = _mod_row_map(n_ctx // TM_PROJ, dec_seq // TM_PROJ)
    map_tm = _mod_row_map(n_ctx // tm, dec_seq // tm)
    map_peer = _mod_row_map(n_ctx // TM_PEER, dec_seq // TM_PEER)

    n_tiles = n_tok // tm
    tile_idx = jnp.arange(n_tiles)
    dec_tile = (tile_idx - n_ctx // tm) % (dec_seq // tm)
    is_start = jnp.where(tile_idx < n_ctx // tm, True, dec_tile == 0)[:, None]
    is_end = jnp.where(tile_idx < n_ctx // tm, True, dec_tile == dec_seq // tm - 1)[:, None]

    ctx_states = []
    for l in range(depth):
        mod3 = mod_all[l][:, None, :]
        w_in_bf = w_in[l].astype(BF16)
        (pa,) = _inproj(x, mod3, norm1_g[l][None], w_in_bf[:, :A_COLS], (A_COLS,), map_proj, "inproj_a")
        pb, pc, pg = _inproj(x, mod3, norm1_g[l][None], w_in_bf[:, A_COLS:],
                             (B_WIDTH, 2 * C_WIDTH, 3 * d), map_proj, "inproj_bcg")

        pa3 = pa.reshape(n_tiles, tm, A_COLS)
        zero_row = jnp.zeros((1, A_COLS), F32)
        halo_prev = jnp.concatenate([zero_row, pa3[:-1, tm - 1, :]], axis=0)
        halo_next = jnp.concatenate([pa3[1:, 0, :], zero_row], axis=0)
        halo_prev = jnp.where(is_start, 0.0, halo_prev)[:, None, :]
        halo_next = jnp.where(is_end, 0.0, halo_next)[:, None, :]
        prep_params = (shift_mu[l][None], decay_w0[l].reshape(1, 2 * d),
                       _block_diag2(decay_w2[l]).astype(BF16), iclr_a0[l].reshape(1, 2 * d),
                       _block_diag2(iclr_a2[l]).astype(BF16), gate_g2[l].astype(BF16),
                       k_k[l][None], k_a[l][None], r_k[l].reshape(1, d), ones_bd)
        r, v, kk, dec, kd, bb, bv, g = _rwkv_prep(pa, halo_prev, halo_next, tm, prep_params)

        y_scan = None
        for seq_len, batch0, nb, s0 in ((seq, 0, batch, None),
                                        (dec_seq, n_ctx // dec_seq, dec_batch, state_rwkv[:, l])):
            vs = _lane_plan(nb)[1]
            if s0 is None:
                s0_c = jnp.zeros((2, A_HEAD, A_HEAD // vs, vs * nb * A_HEADS), F32)
            else:
                s0_c = _state_to_chain(s0, vs)
            lay = lambda a, value_rows=False: _to_chain(a, seq_len, batch0, nb, value_rows)
            y_fwd, y_bwd, sfin_c = _rwkv_scan(lay(kk)[0], lay(r)[0], lay(dec), lay(bb), lay(kd),
                                              lay(v, True)[0], s0_c)
            y_scan = _from_chain(y_fwd, y_bwd, y_scan, n_tok, seq_len, batch0, nb)
            if s0 is None:
                ctx_states.append(_state_from_chain(sfin_c, nb, vs))

        yb = _pool_mix(pb, n_ctx, seq, dec_seq, pool_w[l].astype(BF16), pool_b[l].reshape(1, B_WIDTH),
                       pool_scale[l][None])
        yc = _conv_mix(pc, tm, n_ctx // tm, conv_w[l], conv_b[l][None], cln_g[l][None], cln_b[l][None])

        x, h2 = _merge(y_scan, bv, g, yb, yc, pg, x, mod3, map_tm, tm, gn_g[l][None], gn_b[l][None],
                       norm2_g[l][None], w_a_out[l].astype(BF16), w_b_out[l].astype(BF16),
                       w_c_out[l].astype(BF16), w_out[l].astype(BF16), ones_bd)

        wq_t = peer_wq[l].T.astype(BF16)
        keys = peer_keys[l].reshape(2 * P_HEADS, P_KEYS, P_HALF).astype(BF16)
        rk0, e0p, th1, e1p = _peer_scores(h2, wq_t, keys)
        x = _peer_experts(h2, peer_u[l].astype(BF16), peer_v[l].T.astype(BF16), rk0, e0p, th1, e1p,
                          x, mod3, map_peer)

    y = _final_norm(x, final_g[None])
    new_state = jnp.stack(ctx_states, axis=1)
    return (y[:n_ctx].reshape(batch, seq, d), y[n_ctx:].reshape(dec_batch, dec_seq, d), new_state)
```

```python
import functools

import jax
import jax.numpy as jnp
from jax import lax
from jax.experimental import pallas as pl
from jax.experimental.pallas import tpu as pltpu

F32 = jnp.float32
BF16 = jnp.bfloat16

D_MODEL = 1024
A_HEADS = 16
A_HEAD = 64
A_LORA = 384
A_COLS = 3 * D_MODEL + A_LORA
B_WIDTH = 512
B_GROUP = 128
POOL_WINDOWS = (2, 4, 8, 16)
POOL_PAD = 16
C_WIDTH = 512
C_CONV = 31
CONV_PAD = 16
GRID_W = 64
P_HEADS = 8
P_KEYS = 128
P_TOPK = 16
P_HALF = 128
P_EXPERTS = P_KEYS * P_KEYS
EPS = 1e-6
GN_EPS = 64e-5
LN_EPS = 1e-5

LANES = 128
VMEM_LIMIT = 56 * 1024 * 1024

TM_PROJ = 512
TM_PEER = 512
PEER_EB = 1024
SCAN_TB = 16


def _cparams(*sem):
    return pltpu.CompilerParams(dimension_semantics=sem, vmem_limit_bytes=VMEM_LIMIT)


def _sigmoid(x):
    return 1.0 / (1.0 + jnp.exp(-x))


def _dot(a, b):
    return jnp.dot(a, b, preferred_element_type=F32)


def _segsum(x, ones_bd):
    hi = x.astype(BF16)
    lo = (x - hi.astype(F32)).astype(BF16)
    return _dot(hi, ones_bd) + _dot(lo, ones_bd)


def _mod_kernel(c_ref, w_ref, b_ref, o_ref):
    c = c_ref[...]
    s = c * _sigmoid(c)
    o_ref[...] = jnp.dot(s, w_ref[...], preferred_element_type=F32,
                         precision=lax.Precision.HIGHEST) + b_ref[...]


def _modulation(cond8, ada_w, ada_b):
    L = ada_w.shape[0]
    nb = 6
    return pl.pallas_call(
        _mod_kernel,
        grid=(L, nb),
        in_specs=[pl.BlockSpec((8, D_MODEL), lambda l, j: (0, 0)),
                  pl.BlockSpec((None, D_MODEL, D_MODEL), lambda l, j: (l, 0, j)),
                  pl.BlockSpec((None, 1, D_MODEL), lambda l, j: (l, 0, j))],
        out_specs=pl.BlockSpec((None, 8, D_MODEL), lambda l, j: (l, 0, j)),
        out_shape=jax.ShapeDtypeStruct((L, 8, 6 * D_MODEL), F32),
        compiler_params=_cparams("parallel", "parallel"),
        name="adaln_mod",
    )(cond8, ada_w, ada_b.reshape(L, 1, 6 * D_MODEL))


def _mod_row_map(n_ctx_tiles, tiles_per_dec_seq):
    def index_map(i):
        row = jnp.where(i < n_ctx_tiles, 0, 1 + (i - n_ctx_tiles) // tiles_per_dec_seq)
        return (row, 0, 0)
    return index_map


def _inproj_kernel(x_ref, mod_ref, g_ref, w_ref, *o_refs, edge_tile):
    x = x_ref[...]
    ms = jnp.mean(x * x, axis=-1, keepdims=True)
    y = x * lax.rsqrt(ms + EPS) * g_ref[...]
    sh = mod_ref[0, :, 0:D_MODEL]
    sc = mod_ref[0, :, D_MODEL:2 * D_MODEL]
    h = (y * (1.0 + sc) + sh).astype(BF16)
    if edge_tile:
        *o_refs, edge_ref = o_refs
    off = 0
    for o_ref in o_refs:
        n = o_ref.shape[1]
        o_ref[...] = _dot(h, w_ref[:, off:off + n])
        off += n
    if edge_tile:
        first = o_refs[0]
        for t in range(x.shape[0] // edge_tile):
            edge_ref[t, 0:1, :] = first[t * edge_tile:t * edge_tile + 1, :]
            edge_ref[t, 1:2, :] = first[(t + 1) * edge_tile - 1:(t + 1) * edge_tile, :]


def _inproj(x, mod3, g, w, splits, mod_map, name, edge_tile=0):
    n_tok = x.shape[0]
    n_out = w.shape[1]
    out_specs = [pl.BlockSpec((TM_PROJ, n), lambda i: (i, 0)) for n in splits]
    out_shape = [jax.ShapeDtypeStruct((n_tok, n), F32) for n in splits]
    if edge_tile:
        per = TM_PROJ // edge_tile
        out_specs.append(pl.BlockSpec((per, 2, splits[0]), lambda i: (i, 0, 0)))
        out_shape.append(jax.ShapeDtypeStruct((n_tok // edge_tile, 2, splits[0]), F32))
    return pl.pallas_call(
        functools.partial(_inproj_kernel, edge_tile=edge_tile),
        grid=(n_tok // TM_PROJ,),
        in_specs=[pl.BlockSpec((TM_PROJ, D_MODEL), lambda i: (i, 0)),
                  pl.BlockSpec((1, 1, 6 * D_MODEL), mod_map),
                  pl.BlockSpec((1, D_MODEL), lambda i: (0, 0)),
                  pl.BlockSpec((D_MODEL, n_out), lambda i: (0, 0))],
        out_specs=out_specs,
        out_shape=out_shape,
        compiler_params=_cparams("parallel"),
        name=name,
    )(x, mod3, g, w)


def _prep_kernel(pa_ref, hp_ref, hn_ref, mu_ref, w0_ref, w2_ref, a0_ref, a2_ref, g2_ref,
                 kkw_ref, ka_ref, rk_ref, ones_ref,
                 r_o, v_o, kk_o, dec_o, kd_o, bb_o, bv_o, g_o):
    tm = pa_ref.shape[0]

    def shifted(c0, c1):
        x = pa_ref[:, c0:c1]
        row = lax.broadcasted_iota(jnp.int32, x.shape, 0)
        prev = jnp.where(row == 0, hp_ref[0, :, c0:c1], pltpu.roll(x, 1, 0))
        nxt = jnp.where(row == tm - 1, hn_ref[0, :, c0:c1], pltpu.roll(x, tm - 1, 0))
        return x + (0.5 * (prev + nxt) - x) * mu_ref[:, c0:c1]

    d = D_MODEL
    r = shifted(0, d)
    k = shifted(d, 2 * d)
    v = shifted(2 * d, 3 * d)
    lo = shifted(3 * d, 3 * d + A_LORA)
    w_lo = lo[:, 0:128]
    a_lo = lo[:, 128:256]
    g_lo = lo[:, 256:384]
    ones_bd = ones_ref[...]

    r_o[...] = r
    v_o[...] = v

    kkraw = k * kkw_ref[...]
    nrm = jnp.sqrt(_segsum(kkraw * kkraw, ones_bd))
    kk = kkraw / jnp.maximum(nrm, 1e-12)
    kk_o[...] = kk

    z = -(w0_ref[...] + _dot(jnp.tanh(w_lo).astype(BF16), w2_ref[...]))
    softplus = jnp.maximum(z, 0.0) + jnp.log(1.0 + jnp.exp(-jnp.abs(z)))
    dec_o[...] = jnp.exp(-jnp.exp(-softplus - 0.5))

    a = _sigmoid(a0_ref[...] + _dot(a_lo.astype(BF16), a2_ref[...]))
    ka = ka_ref[...]
    kd0 = k * (1.0 + (a[:, 0:d] - 1.0) * ka)
    kd1 = k * (1.0 + (a[:, d:2 * d] - 1.0) * ka)
    kd_o[:, 0:d] = kd0
    kd_o[:, d:2 * d] = kd1
    bb_o[:, 0:d] = kk * a[:, 0:d]
    bb_o[:, d:2 * d] = kk * a[:, d:2 * d]

    bonus = _segsum(r * (kd0 + kd1) * rk_ref[...], ones_bd)
    bv_o[...] = bonus * v
    g_o[...] = _dot(_sigmoid(g_lo).astype(BF16), g2_ref[...])


def _rwkv_prep(pa, halo_prev, halo_next, tm, params):
    n_tok = pa.shape[0]
    d = D_MODEL
    row = lambda n: pl.BlockSpec((1, n), lambda i: (0, 0))
    full = lambda a, b: pl.BlockSpec((a, b), lambda i: (0, 0))
    tile = lambda n: pl.BlockSpec((tm, n), lambda i: (i, 0))
    halo = pl.BlockSpec((1, 1, A_COLS), lambda i: (i, 0, 0))
    outs = [d, d, d, 2 * d, 2 * d, 2 * d, d, d]
    return pl.pallas_call(
        _prep_kernel,
        grid=(n_tok // tm,),
        in_specs=[tile(A_COLS), halo, halo, row(A_COLS), row(2 * d), full(128, 2 * d), row(2 * d),
                  full(128, 2 * d), full(128, d), row(d), row(d), row(d), full(d, d)],
        out_specs=[tile(n) for n in outs],
        out_shape=[jax.ShapeDtypeStruct((n_tok, n), F32) for n in outs],
        compiler_params=_cparams("parallel"),
        name="rwkv_prep",
    )(pa, halo_prev, halo_next, *params)


SCAN_ROWS = 16


def _scan_kernel(kkf, kkb, rf, rb, wf, wb, bf, bb, kdf, kdb, vf, vb, s0_ref,
                 yf_ref, yb_ref, sfin_ref, s_scr):
    nk, tb = kkf.shape[0], kkf.shape[1]
    j = pl.program_id(2)

    @pl.when(j == 0)
    def _():
        s_scr[...] = s0_ref[...]

    def advance(d, t, rs, kk_ref, r_ref, w_ref, b_ref, kd_ref, v_ref, y_ref):
        parts = [None] * 4
        for k in range(nk):
            term = s_scr[d, k, rs, :] * kk_ref[k, pl.ds(t, 1), :]
            parts[k % 4] = term if parts[k % 4] is None else parts[k % 4] + term
        sa = -((parts[0] + parts[1]) + (parts[2] + parts[3]))
        vt = v_ref[t, rs, :]
        ys = [None] * 2
        for k in range(nk):
            s_new = (s_scr[d, k, rs, :] * w_ref[k, pl.ds(t, 1), :] + sa * b_ref[k, pl.ds(t, 1), :]
                     + vt * kd_ref[k, pl.ds(t, 1), :])
            s_scr[d, k, rs, :] = s_new
            term = s_new * r_ref[k, pl.ds(t, 1), :]
            ys[k % 2] = term if ys[k % 2] is None else ys[k % 2] + term
        y_ref[t, rs, :] = ys[0] + ys[1]

    rs = slice(None)

    def step(i, carry):
        advance(0, i, rs, kkf, rf, wf, bf, kdf, vf, yf_ref)
        advance(1, tb - 1 - i, rs, kkb, rb, wb, bb, kdb, vb, yb_ref)
        return carry

    lax.fori_loop(0, tb, step, 0)

    @pl.when(j == pl.num_programs(2) - 1)
    def _():
        sfin_ref[...] = s_scr[...]


def _rwkv_scan(kk_c, r_c, w_c, b_c, kd_c, v_c, s0_c):
    nk, T, L = kk_c.shape
    vq = v_c.shape[1]
    tb = min(SCAN_TB, T)
    nt = T // tb
    fwd = lambda j: j
    bwd = lambda j: nt - 1 - j
    rows = min(vq, SCAN_ROWS)
    shared = lambda tj: pl.BlockSpec((nk, tb, LANES), lambda g, q, j: (0, tj(j), g))
    perdir = lambda d, tj: pl.BlockSpec((None, nk, tb, LANES), lambda g, q, j: (d, 0, tj(j), g))
    vspec = lambda tj: pl.BlockSpec((tb, rows, LANES), lambda g, q, j: (tj(j), q, g))
    sspec = pl.BlockSpec((2, nk, rows, LANES), lambda g, q, j: (0, 0, q, g))
    y_shape = jax.ShapeDtypeStruct((T, vq, L), F32)
    return pl.pallas_call(
        _scan_kernel,
        grid=(L // LANES, vq // rows, nt),
        in_specs=[shared(fwd), shared(bwd), shared(fwd), shared(bwd),
                  perdir(0, fwd), perdir(1, bwd), perdir(0, fwd), perdir(1, bwd),
                  perdir(0, fwd), perdir(1, bwd), vspec(fwd), vspec(bwd), sspec],
        out_specs=[vspec(fwd), vspec(bwd), sspec],
        out_shape=[y_shape, y_shape, jax.ShapeDtypeStruct((2, nk, vq, L), F32)],
        scratch_shapes=[pltpu.VMEM((2, nk, rows, LANES), F32)],
        compiler_params=_cparams("parallel", "parallel", "arbitrary"),
        name="rwkv_scan",
    )(kk_c, kk_c, r_c, r_c, w_c, w_c, b_c, b_c, kd_c, kd_c, v_c, v_c, s0_c)


def _lane_plan(n_batch):
    full = LANES // A_HEADS
    if n_batch >= full:
        assert n_batch % full == 0
        return full, 1
    assert full % n_batch == 0
    return n_batch, full // n_batch


def _to_chain_kernel(x_ref, o_ref, scr, *, vs, value_rows):
    nb, tc = x_ref.shape[0], x_ref.shape[1]
    for b in range(nb):
        scr[b * A_HEADS:(b + 1) * A_HEADS] = x_ref[b].T.reshape(A_HEADS, A_HEAD, tc)
    vq = A_HEAD // vs
    if value_rows:
        for v in range(vq):
            rows = [scr[:, vh * vq + v, :] for vh in range(vs)]
            rows = rows[0] if vs == 1 else jnp.concatenate(rows, axis=0)
            o_ref[:, v, :] = rows.T
    else:
        for k in range(A_HEAD):
            rows = scr[:, k, :]
            if vs > 1:
                rows = jnp.concatenate([rows] * vs, axis=0)
            o_ref[k] = rows.T


def _to_chain(x, seq_len, batch0, n_batch, value_rows):
    n_col = x.shape[1] // D_MODEL
    nb, vs = _lane_plan(n_batch)
    assert batch0 % nb == 0 and seq_len % LANES == 0
    groups = n_batch // nb
    tc = LANES
    x3 = x.reshape(x.shape[0] // seq_len, seq_len, n_col * D_MODEL)
    vq = A_HEAD // vs
    if value_rows:
        out_spec = pl.BlockSpec((None, tc, vq, LANES), lambda c, g, j: (c, j, 0, g))
        out_shape = jax.ShapeDtypeStruct((n_col, seq_len, vq, groups * LANES), F32)
    else:
        out_spec = pl.BlockSpec((None, A_HEAD, tc, LANES), lambda c, g, j: (c, 0, j, g))
        out_shape = jax.ShapeDtypeStruct((n_col, A_HEAD, seq_len, groups * LANES), F32)
    return pl.pallas_call(
        functools.partial(_to_chain_kernel, vs=vs, value_rows=value_rows),
        grid=(n_col, groups, seq_len // tc),
        in_specs=[pl.BlockSpec((nb, tc, D_MODEL), lambda c, g, j: (batch0 // nb + g, j, c))],
        out_specs=out_spec,
        out_shape=out_shape,
        scratch_shapes=[pltpu.VMEM((nb * A_HEADS, A_HEAD, tc), F32)],
        compiler_params=_cparams("parallel", "parallel", "parallel"),
        name="to_chain_v" if value_rows else "to_chain_k",
    )(x3)


def _from_chain_kernel(*refs, vs):
    yf_ref, yb_ref, o_ref, scr = refs[-4:]
    nb, tc = o_ref.shape[0], o_ref.shape[1]
    vq = A_HEAD // vs
    rows = nb * A_HEADS
    for v in range(vq):
        yv = (yf_ref[:, v, :] + yb_ref[:, v, :]).T
        for vh in range(vs):
            scr[:, vh * vq + v, :] = yv[vh * rows:(vh + 1) * rows]
    for b in range(nb):
        o_ref[b] = scr[b * A_HEADS:(b + 1) * A_HEADS].reshape(D_MODEL, tc).T


def _from_chain(y_fwd, y_bwd, prev_rows, n_tok, seq_len, batch0, n_batch):
    nb, vs = _lane_plan(n_batch)
    groups = n_batch // nb
    tc = LANES
    vq = A_HEAD // vs
    in_specs = [pl.BlockSpec((tc, vq, LANES), lambda g, j: (j, 0, g))] * 2
    args = [y_fwd, y_bwd]
    aliases = {}
    if prev_rows is not None:
        in_specs = [pl.BlockSpec(memory_space=pl.ANY)] + in_specs
        args = [prev_rows.reshape(n_tok // seq_len, seq_len, D_MODEL)] + args
        aliases = {0: 0}
    out = pl.pallas_call(
        functools.partial(_from_chain_kernel, vs=vs),
        grid=(groups, seq_len // tc),
        in_specs=in_specs,
        out_specs=pl.BlockSpec((nb, tc, D_MODEL), lambda g, j: (batch0 // nb + g, j, 0)),
        out_shape=jax.ShapeDtypeStruct((n_tok // seq_len, seq_len, D_MODEL), F32),
        scratch_shapes=[pltpu.VMEM((nb * A_HEADS, A_HEAD, tc), F32)],
        input_output_aliases=aliases,
        compiler_params=_cparams("parallel", "parallel"),
        name="from_chain",
    )(*args)
    return out.reshape(n_tok, D_MODEL)


def _state_to_chain(s, vs):
    B = s.shape[0]
    vq = A_HEAD // vs
    s = jnp.transpose(s, (1, 4, 3, 0, 2)).reshape(2, A_HEAD, vs, vq, B * A_HEADS)
    return jnp.transpose(s, (0, 1, 3, 2, 4)).reshape(2, A_HEAD, vq, vs * B * A_HEADS)


def _state_from_chain(s_c, n_batch, vs):
    vq = A_HEAD // vs
    s = s_c.reshape(2, A_HEAD, vq, vs, n_batch, A_HEADS)
    return jnp.transpose(s, (4, 0, 5, 3, 2, 1)).reshape(n_batch, 2, A_HEADS, A_HEAD, A_HEAD)


def _pool_tail(d2, g, pw_ref, pb_ref, ps_ref):
    sl = slice(g * B_GROUP, (g + 1) * B_GROUP)
    y = _dot(d2.astype(BF16), pw_ref[g]) + pb_ref[:, sl]
    return y * ps_ref[:, sl]


def _window_count(t, w, n):
    lo = jnp.clip(t - w // 2, 0, n)
    hi = jnp.clip(t - w // 2 + w, 0, n)
    return (hi - lo).astype(F32)


def _pool_seq_kernel(z_ref, pw_ref, pb_ref, ps_ref, o_ref):
    n = z_ref.shape[0]
    pad = jnp.zeros((POOL_PAD, B_GROUP), F32)
    t = lax.broadcasted_iota(jnp.int32, (n, B_GROUP), 0)
    for g, w in enumerate(POOL_WINDOWS):
        sl = slice(g * B_GROUP, (g + 1) * B_GROUP)
        z = z_ref[:, sl]
        zp = jnp.concatenate([pad, z, pad], axis=0)
        acc = None
        for j in range(w):
            off = j - w // 2
            term = zp if off == 0 else pltpu.roll(zp, (-off) % (n + 2 * POOL_PAD), 0)
            acc = term if acc is None else acc + term
        d = acc[POOL_PAD:POOL_PAD + n] / _window_count(t, w, n) - z
        o_ref[:, sl] = _pool_tail(d, g, pw_ref, pb_ref, ps_ref)


def _pool_grid_kernel(z_ref, pw_ref, pb_ref, ps_ref, _, o_ref):
    rows, cb = z_ref.shape[0], z_ref.shape[1]
    pad = jnp.zeros((POOL_PAD, cb, B_GROUP), F32)
    t = lax.broadcasted_iota(jnp.int32, (rows, cb, B_GROUP), 0)
    for g, w in enumerate(POOL_WINDOWS):
        sl = slice(g * B_GROUP, (g + 1) * B_GROUP)
        z = z_ref[:, :, sl]
        zp = jnp.concatenate([pad, z, pad], axis=0)
        acc = None
        for j in range(w):
            s = POOL_PAD + j - w // 2
            term = zp[s:s + rows]
            acc = term if acc is None else acc + term
        d = acc / _window_count(t, w, rows) - z
        y = _pool_tail(d.reshape(rows * cb, B_GROUP), g, pw_ref, pb_ref, ps_ref)
        o_ref[:, :, sl] = y.reshape(rows, cb, B_GROUP)


def _pool_mix(pb, n_ctx, seq, dec_seq, pw, pbias, pscale):
    n_tok = pb.shape[0]
    wspecs = lambda nd: [pl.BlockSpec((4, B_GROUP, B_GROUP), lambda *a: (0, 0, 0)),
                         pl.BlockSpec((1, B_WIDTH), lambda *a: (0, 0)),
                         pl.BlockSpec((1, B_WIDTH), lambda *a: (0, 0))]
    y_ctx = pl.pallas_call(
        _pool_seq_kernel,
        grid=(n_ctx // seq,),
        in_specs=[pl.BlockSpec((seq, B_WIDTH), lambda i: (i, 0))] + wspecs(1),
        out_specs=pl.BlockSpec((seq, B_WIDTH), lambda i: (i, 0)),
        out_shape=jax.ShapeDtypeStruct((n_tok, B_WIDTH), F32),
        compiler_params=_cparams("parallel"),
        name="pool_seq",
    )(pb, pw, pbias, pscale)
    rows = dec_seq // GRID_W
    n_dec = (n_tok - n_ctx) // dec_seq
    ctx_blocks = n_ctx // dec_seq
    cb = 8
    shape4 = (n_tok // dec_seq, rows, GRID_W, B_WIDTH)
    blk = pl.BlockSpec((None, rows, cb, B_WIDTH), lambda b, c: (b + ctx_blocks, 0, c, 0))
    y_all = pl.pallas_call(
        _pool_grid_kernel,
        grid=(n_dec, GRID_W // cb),
        in_specs=[blk] + wspecs(2) + [pl.BlockSpec(memory_space=pl.ANY)],
        out_specs=blk,
        out_shape=jax.ShapeDtypeStruct(shape4, F32),
        input_output_aliases={4: 0},
        compiler_params=_cparams("parallel", "parallel"),
        name="pool_grid",
    )(pb.reshape(shape4), pw, pbias, pscale, y_ctx.reshape(shape4))
    return y_all.reshape(n_tok, B_WIDTH)


def _conv_kernel(pc_ref, cw_ref, cb_ref, lg_ref, lb_ref, o_ref, *, n_ctx_tiles):
    tm = pc_ref.shape[0]
    u = pc_ref[:, 0:C_WIDTH] * _sigmoid(pc_ref[:, C_WIDTH:2 * C_WIDTH])
    pad = jnp.zeros((CONV_PAD, C_WIDTH), F32)
    up = jnp.concatenate([pad, u, pad], axis=0)
    is_ctx = pl.program_id(0) < n_ctx_tiles
    t = lax.broadcasted_iota(jnp.int32, (tm, C_WIDTH), 0)
    pos = jnp.where(is_ctx, t, t & (GRID_W - 1))
    seg = jnp.where(is_ctx, tm, GRID_W)
    acc = jnp.zeros((tm, C_WIDTH), F32)
    for j in range(C_CONV):
        off = j - C_CONV // 2
        rolled = up if off == 0 else pltpu.roll(up, (-off) % (tm + 2 * CONV_PAD), 0)
        tap = rolled[CONV_PAD:CONV_PAD + tm]
        q = pos + off
        tap = jnp.where(q >= 0, jnp.where(q < seg, tap, 0.0), 0.0)
        acc = acc + tap * cw_ref[pl.ds(j, 1), :]
    c = acc + cb_ref[...]
    mu = jnp.mean(c, axis=-1, keepdims=True)
    cc = c - mu
    var = jnp.mean(cc * cc, axis=-1, keepdims=True)
    y = cc * lax.rsqrt(var + LN_EPS) * lg_ref[...] + lb_ref[...]
    o_ref[...] = y * _sigmoid(y)


def _conv_mix(pc, tm, n_ctx_tiles, cw, cb, lg, lb):
    n_tok = pc.shape[0]
    row = pl.BlockSpec((1, C_WIDTH), lambda i: (0, 0))
    return pl.pallas_call(
        functools.partial(_conv_kernel, n_ctx_tiles=n_ctx_tiles),
        grid=(n_tok // tm,),
        in_specs=[pl.BlockSpec((tm, 2 * C_WIDTH), lambda i: (i, 0)),
                  pl.BlockSpec((C_CONV, C_WIDTH), lambda i: (0, 0)), row, row, row],
        out_specs=pl.BlockSpec((tm, C_WIDTH), lambda i: (i, 0)),
        out_shape=jax.ShapeDtypeStruct((n_tok, C_WIDTH), F32),
        compiler_params=_cparams("parallel"),
        name="conv_mix",
    )(pc, cw, cb, lg, lb)


def _merge_kernel(y_ref, bv_ref, g_ref, yb_ref, yc_ref, pg_ref, x_ref, mod_ref, gng_ref, gnb_ref,
                  n2g_ref, wa_ref, wb_ref, wc_ref, wo_ref, ones_ref, xo_ref, h2_ref):
    d = D_MODEL
    ones_bd = ones_ref[...]
    y = y_ref[...]
    mean = _segsum(y, ones_bd) * (1.0 / A_HEAD)
    yc = y - mean
    var = _segsum(yc * yc, ones_bd) * (1.0 / A_HEAD)
    yn = yc * lax.rsqrt(var + GN_EPS) * gng_ref[...] + gnb_ref[...]
    ya = (yn + bv_ref[...]) * g_ref[...]
    merged = (_sigmoid(pg_ref[:, 0:d]) * _dot(ya.astype(BF16), wa_ref[...])
              + _sigmoid(pg_ref[:, d:2 * d]) * _dot(yb_ref[...].astype(BF16), wb_ref[...])
              + _sigmoid(pg_ref[:, 2 * d:3 * d]) * _dot(yc_ref[...].astype(BF16), wc_ref[...]))
    gt1 = mod_ref[0, :, 2 * d:3 * d]
    xn = x_ref[...] + gt1 * _dot(merged.astype(BF16), wo_ref[...])
    xo_ref[...] = xn
    ms = jnp.mean(xn * xn, axis=-1, keepdims=True)
    h2 = xn * lax.rsqrt(ms + EPS) * n2g_ref[...]
    sh2 = mod_ref[0, :, 3 * d:4 * d]
    sc2 = mod_ref[0, :, 4 * d:5 * d]
    h2_ref[...] = (h2 * (1.0 + sc2) + sh2).astype(BF16)


def _merge(y, bv, g, yb, yc, pg, x, mod3, mod_map, tm, gng, gnb, n2g, wa, wb, wc, wo, ones_bd):
    n_tok = x.shape[0]
    d = D_MODEL
    tile = lambda n: pl.BlockSpec((tm, n), lambda i: (i, 0))
    row = pl.BlockSpec((1, d), lambda i: (0, 0))
    full = lambda a: pl.BlockSpec((a, d), lambda i: (0, 0))
    return pl.pallas_call(
        _merge_kernel,
        grid=(n_tok // tm,),
        in_specs=[tile(d), tile(d), tile(d), tile(B_WIDTH), tile(C_WIDTH), tile(3 * d), tile(d),
                  pl.BlockSpec((1, 1, 6 * d), mod_map), row, row, row,
                  full(d), full(B_WIDTH), full(C_WIDTH), full(d), full(d)],
        out_specs=[tile(d), tile(d)],
        out_shape=[jax.ShapeDtypeStruct((n_tok, d), F32), jax.ShapeDtypeStruct((n_tok, d), BF16)],
        compiler_params=_cparams("parallel"),
        name="merge_outproj",
    )(y, bv, g, yb, yc, pg, x, mod3, gng, gnb, n2g, wa, wb, wc, wo, ones_bd)


NO_RANK = 1e9
RANK_BASE = -2.0 ** 100


def _peer_score_kernel(h_ref, wq_ref, keys_ref, rk0_o, e0_o, th1_o, e1_o, s_scr, top_scr, rank_scr):
    tmp = h_ref.shape[0]
    q_t = lax.dot_general(wq_ref[...], h_ref[...], (((1,), (1,)), ((), ())),
                          preferred_element_type=F32).astype(BF16)
    for hp in range(2 * P_HEADS):
        s_scr[hp] = _dot(keys_ref[hp], q_t[hp * P_HALF:(hp + 1) * P_HALF, :])

    neg = jnp.float32(-jnp.inf)
    half = P_TOPK // 2
    row16 = lax.broadcasted_iota(jnp.int32, (P_TOPK, LANES), 0)
    row8 = lax.broadcasted_iota(jnp.int32, (half, LANES), 0)
    n_groups = tmp // LANES

    def top16_pair(sets):
        vals = [jnp.full((P_TOPK, LANES), neg, F32) for _ in sets]
        sets = list(sets)
        for it in range(P_TOPK):
            for n in range(len(sets)):
                m = jnp.max(sets[n], axis=0, keepdims=True)
                vals[n] = jnp.where(row16 == it, m, vals[n])
                sets[n] = jnp.where(sets[n] >= m, RANK_BASE * (1.0 + it / P_TOPK), sets[n])
        ranks = [jnp.where(s <= RANK_BASE, s * (P_TOPK / RANK_BASE) - P_TOPK, NO_RANK) for s in sets]
        return vals, ranks

    n_items = P_HEADS * n_groups

    def item(idx):
        return idx // n_groups, pl.ds(pl.multiple_of((idx % n_groups) * LANES, LANES), LANES)

    def extract(idx):
        h, ls = item(idx)
        (a16, b16), (rank0, rank1) = top16_pair((s_scr[2 * h, :, ls], s_scr[2 * h + 1, :, ls]))
        rk0_o[h, :, ls] = rank0
        top_scr[0] = a16
        top_scr[1] = b16
        rank_scr[...] = rank1

    def combine(idx):
        h, ls = item(idx)
        a16 = top_scr[0]
        b16 = top_scr[1]
        rank1 = rank_scr[...]
        s0 = s_scr[2 * h, :, ls]
        s1 = s_scr[2 * h + 1, :, ls]
        b_lo = b16[0:half, :]
        cands = [a16[0:1, :] + b16, a16[1:2, :] + b_lo]
        for r in range(2, half):
            cands.append(jnp.where(row8 < P_TOPK // (r + 1), a16[r:r + 1, :] + b_lo, neg))
        cands.append(a16[half:P_TOPK, :] + b16[0:1, :])
        cmax = a16[0:1, :] + b16[0:1, :]
        work = list(cands)
        tau = cmax
        for it in range(P_TOPK):
            m = jnp.maximum(work[0][0:half, :], work[0][half:P_TOPK, :])
            for c in work[1:]:
                m = jnp.maximum(m, c)
            tau = jnp.max(m, axis=0, keepdims=True)
            if it + 1 < P_TOPK:
                work = [jnp.where(c >= tau, neg, c) for c in work]
        z = jnp.zeros((1, LANES), F32)
        sels = []
        for c in cands:
            sel = c >= tau
            z = z + jnp.sum(jnp.where(sel, jnp.exp(c - cmax), 0.0), axis=0, keepdims=True)
            sels.append(jnp.where(sel, 1.0, 0.0))
        cnt_hi = sels[0][half:P_TOPK, :]
        cnt_lo = sels[0][0:half, :]
        for c in sels[1:half]:
            cnt_lo = cnt_lo + c
        cnt_lo = cnt_lo + jnp.where(row8 == 0, jnp.sum(sels[half], axis=0, keepdims=True), 0.0)
        th1 = jnp.zeros(rank1.shape, F32)
        for q in range(P_TOPK):
            cnt = cnt_lo[q:q + 1, :] if q < half else cnt_hi[q - half:q - half + 1, :]
            th1 = jnp.where(rank1 == float(q), cnt, th1)
        th1_o[h, :, ls] = th1.astype(BF16)
        e0_o[h, :, ls] = jnp.exp(s0 - a16[0:1, :]) / z
        e1_o[h, :, ls] = jnp.exp(s1 - b16[0:1, :]).astype(BF16)

    top_scr[...] = jnp.zeros_like(top_scr)
    rank_scr[...] = jnp.zeros_like(rank_scr)

    def body(n, carry):
        combine(jnp.maximum(n - 1, 0))
        extract(jnp.minimum(n, n_items - 1))
        return carry

    lax.fori_loop(0, n_items + 1, body, 0)


def _peer_scores(h2, wq_t, keys):
    n_tok = h2.shape[0]
    tmp = TM_PEER
    big = pl.BlockSpec((P_HEADS, P_KEYS, tmp), lambda i: (0, 0, i))
    big_shape = jax.ShapeDtypeStruct((P_HEADS, P_KEYS, n_tok), F32)
    narrow_shape = jax.ShapeDtypeStruct((P_HEADS, P_KEYS, n_tok), BF16)
    return pl.pallas_call(
        _peer_score_kernel,
        grid=(n_tok // tmp,),
        in_specs=[pl.BlockSpec((tmp, D_MODEL), lambda i: (i, 0)),
                  pl.BlockSpec((2 * P_HEADS * P_HALF, D_MODEL), lambda i: (0, 0)),
                  pl.BlockSpec((2 * P_HEADS, P_KEYS, P_HALF), lambda i: (0, 0, 0))],
        out_specs=[big, big, big, big],
        out_shape=[big_shape, big_shape, narrow_shape, narrow_shape],
        scratch_shapes=[pltpu.VMEM((2 * P_HEADS, P_KEYS, tmp), F32),
                        pltpu.VMEM((2, P_TOPK, LANES), F32), pltpu.VMEM((P_KEYS, LANES), F32)],
        compiler_params=_cparams("parallel"),
        name="peer_scores",
    )(h2, wq_t, keys)


def _peer_expert_kernel(h_ref, u_ref, vt_ref, rk0_ref, e0_ref, th1_ref, e1_ref, x_ref,
                        mod_ref, *rest, final_norm):
    if final_norm:
        fg_ref, o_ref, acc_ref, gated_ref = rest
    else:
        o_ref, acc_ref, gated_ref = rest
    e = pl.program_id(1)
    tmp = h_ref.shape[0]

    @pl.when(e == 0)
    def _():
        acc_ref[...] = jnp.zeros_like(acc_ref)

    for il in range(PEER_EB // P_KEYS):
        rows = slice(il * P_KEYS, (il + 1) * P_KEYS)
        for lc in range(tmp // LANES):
            ls = slice(lc * LANES, (lc + 1) * LANES)
            w = None
            for h in range(P_HEADS):
                rk0 = jnp.broadcast_to(rk0_ref[h, pl.ds(il, 1), ls], (P_KEYS, LANES)).astype(BF16)
                e0 = jnp.broadcast_to(e0_ref[h, pl.ds(il, 1), ls], (P_KEYS, LANES)).astype(BF16)
                term = jnp.where(rk0 < th1_ref[h, :, ls], e0 * e1_ref[h, :, ls], jnp.zeros((), BF16))
                w = term if w is None else w + term
            gated_ref[rows, ls] = w
    hid = lax.dot_general(u_ref[...], h_ref[...], (((1,), (1,)), ((), ())),
                          preferred_element_type=F32)
    act = 0.5 * hid * (1.0 + lax.erf(hid * (2.0 ** -0.5)))
    acc_ref[...] += _dot(vt_ref[...], gated_ref[...] * act.astype(BF16))

    @pl.when(e == pl.num_programs(1) - 1)
    def _():
        gt2 = mod_ref[0, :, 5 * D_MODEL:6 * D_MODEL]
        xn = x_ref[...] + gt2 * acc_ref[...].T
        if final_norm:
            ms = jnp.mean(xn * xn, axis=-1, keepdims=True)
            xn = xn * lax.rsqrt(ms + EPS) * fg_ref[...]
        o_ref[...] = xn


def _peer_experts(h2, u_bf, vt_bf, rk0, e0, th1, e1, x, mod3, mod_map, final_g=None):
    n_tok = h2.shape[0]
    tmp = TM_PEER
    n_i = PEER_EB // P_KEYS
    sel = pl.BlockSpec((P_HEADS, n_i, tmp), lambda i, e: (0, e, i))
    full = pl.BlockSpec((P_HEADS, P_KEYS, tmp), lambda i, e: (0, 0, i))
    mod_map2 = lambda i, e: mod_map(i)
    tile = pl.BlockSpec((tmp, D_MODEL), lambda i, e: (i, 0))
    out = jax.ShapeDtypeStruct((n_tok, D_MODEL), F32)
    final_norm = final_g is not None
    extra_in = [pl.BlockSpec((1, D_MODEL), lambda i, e: (0, 0))] if final_norm else []
    extra_args = [final_g] if final_norm else []
    return pl.pallas_call(
        functools.partial(_peer_expert_kernel, final_norm=final_norm),
        grid=(n_tok // tmp, P_EXPERTS // PEER_EB),
        in_specs=[tile,
                  pl.BlockSpec((PEER_EB, D_MODEL), lambda i, e: (e, 0)),
                  pl.BlockSpec((D_MODEL, PEER_EB), lambda i, e: (0, e)),
                  sel, sel, full, full, tile,
                  pl.BlockSpec((1, 1, 6 * D_MODEL), mod_map2)] + extra_in,
        out_specs=tile,
        out_shape=out,
        scratch_shapes=[pltpu.VMEM((D_MODEL, tmp), F32), pltpu.VMEM((PEER_EB, tmp), BF16)],
        compiler_params=_cparams("parallel", "arbitrary"),
        name="peer_experts",
    )(h2, u_bf, vt_bf, rk0, e0, th1, e1, x, mod3, *extra_args)


def _block_diag2(w):
    z = jnp.zeros_like(w[0])
    return jnp.concatenate([jnp.concatenate([w[0], z], axis=1),
                            jnp.concatenate([z, w[1]], axis=1)], axis=0)


def kernel(x_prompt, x_sample, state_rwkv, c, c_ctx, ada_w, ada_b, norm1_g, norm2_g, w_in, shift_mu,
           decay_w0, decay_w2, iclr_a0, iclr_a2, gate_g2, k_k, k_a, r_k, gn_g, gn_b, w_a_out,
           pool_w, pool_b, pool_scale, w_b_out, conv_w, conv_b, cln_g, cln_b, w_c_out, w_out,
           peer_wq, peer_keys, peer_u, peer_v, final_g):
    d = D_MODEL
    batch, seq, _ = x_prompt.shape
    dec_batch, dec_seq, _ = x_sample.shape
    depth = ada_w.shape[0]
    n_ctx = batch * seq
    n_dec = dec_batch * dec_seq
    tm = seq
    assert n_ctx % dec_seq == 0 and dec_seq % tm == 0 and dec_seq % GRID_W == 0
    assert n_ctx % TM_PROJ == 0 and dec_seq % TM_PROJ == 0 and dec_seq % TM_PEER == 0
    assert n_ctx % TM_PEER == 0 and tm % GRID_W == 0 and dec_batch <= 7 and TM_PROJ % tm == 0

    x = jnp.concatenate([x_prompt.reshape(n_ctx, d), x_sample.reshape(n_dec, d)], axis=0)
    n_tok = n_ctx + n_dec

    cond8 = jnp.zeros((8, d), F32).at[0].set(c_ctx).at[1:1 + dec_batch].set(c)
    mod_all = _modulation(cond8, ada_w, ada_b)

    seg_ids = jnp.arange(d) // A_HEAD
    ones_bd = (seg_ids[:, None] == seg_ids[None, :]).astype(BF16)

    map_proj = _mod_row_map(n_ctx // TM_PROJ, dec_seq // TM_PROJ)
    map_tm = _mod_row_map(n_ctx // tm, dec_seq // tm)
    map_peer = _mod_row_map(n_ctx // TM_PEER, dec_seq // TM_PEER)

    n_tiles = n_tok // tm
    tile_idx = jnp.arange(n_tiles)
    dec_tile = (tile_idx - n_ctx // tm) % (dec_seq // tm)
    is_start = jnp.where(tile_idx < n_ctx // tm, True, dec_tile == 0)[:, None]
    is_end = jnp.where(tile_idx < n_ctx // tm, True, dec_tile == dec_seq // tm - 1)[:, None]

    ctx_states = []
    for l in range(depth):
        mod3 = mod_all[l][:, None, :]
        w_in_bf = w_in[l].astype(BF16)
        pa, edges = _inproj(x, mod3, norm1_g[l][None], w_in_bf[:, :A_COLS], (A_COLS,), map_proj,
                            "inproj_a", edge_tile=tm)
        pb, pc, pg = _inproj(x, mod3, norm1_g[l][None], w_in_bf[:, A_COLS:],
                             (B_WIDTH, 2 * C_WIDTH, 3 * d), map_proj, "inproj_bcg")

        zero_row = jnp.zeros((1, A_COLS), F32)
        halo_prev = jnp.concatenate([zero_row, edges[:-1, 1, :]], axis=0)
        halo_next = jnp.concatenate([edges[1:, 0, :], zero_row], axis=0)
        halo_prev = jnp.where(is_start, 0.0, halo_prev)[:, None, :]
        halo_next = jnp.where(is_end, 0.0, halo_next)[:, None, :]
        prep_params = (shift_mu[l][None], decay_w0[l].reshape(1, 2 * d),
                       _block_diag2(decay_w2[l]).astype(BF16), iclr_a0[l].reshape(1, 2 * d),
                       _block_diag2(iclr_a2[l]).astype(BF16), gate_g2[l].astype(BF16),
                       k_k[l][None], k_a[l][None], r_k[l].reshape(1, d), ones_bd)
        r, v, kk, dec, kd, bb, bv, g = _rwkv_prep(pa, halo_prev, halo_next, tm, prep_params)

        y_scan = None
        for seq_len, batch0, nb, s0 in ((seq, 0, batch, None),
                                        (dec_seq, n_ctx // dec_seq, dec_batch, state_rwkv[:, l])):
            vs = _lane_plan(nb)[1]
            if s0 is None:
                s0_c = jnp.zeros((2, A_HEAD, A_HEAD // vs, vs * nb * A_HEADS), F32)
            else:
                s0_c = _state_to_chain(s0, vs)
            lay = lambda a, value_rows=False: _to_chain(a, seq_len, batch0, nb, value_rows)
            y_fwd, y_bwd, sfin_c = _rwkv_scan(lay(kk)[0], lay(r)[0], lay(dec), lay(bb), lay(kd),
                                              lay(v, True)[0], s0_c)
            y_scan = _from_chain(y_fwd, y_bwd, y_scan, n_tok, seq_len, batch0, nb)
            if s0 is None:
                ctx_states.append(_state_from_chain(sfin_c, nb, vs))

        yb = _pool_mix(pb, n_ctx, seq, dec_seq, pool_w[l].astype(BF16), pool_b[l].reshape(1, B_WIDTH),
                       pool_scale[l][None])
        yc = _conv_mix(pc, tm, n_ctx // tm, conv_w[l], conv_b[l][None], cln_g[l][None], cln_b[l][None])

        x, h2 = _merge(y_scan, bv, g, yb, yc, pg, x, mod3, map_tm, tm, gn_g[l][None], gn_b[l][None],
                       norm2_g[l][None], w_a_out[l].astype(BF16), w_b_out[l].astype(BF16),
                       w_c_out[l].astype(BF16), w_out[l].astype(BF16), ones_bd)

        wq_t = peer_wq[l].T.astype(BF16)
        keys = peer_keys[l].reshape(2 * P_HEADS, P_KEYS, P_HALF).astype(BF16)
        rk0, e0p, th1, e1p = _peer_scores(h2, wq_t, keys)
        x = _peer_experts(h2, peer_u[l].astype(BF16), peer_v[l].T.astype(BF16), rk0, e0p, th1, e1p,
                          x, mod3, map_peer, final_g[None] if l == depth - 1 else None)

    y = x
    new_state = jnp.stack(ctx_states, axis=1)
    return (y[:n_ctx].reshape(batch, seq, d), y[n_ctx:].reshape(dec_batch, dec_seq, d), new_state)
```

```python
import functools

import jax
import jax.numpy as jnp
from jax import lax
from jax.experimental import pallas as pl
from jax.experimental.pallas import tpu as pltpu

F32 = jnp.float32
BF16 = jnp.bfloat16

D_MODEL = 1024
A_HEADS = 16
A_HEAD = 64
A_LORA = 384
A_COLS = 3 * D_MODEL + A_LORA
B_WIDTH = 512
B_GROUP = 128
POOL_WINDOWS = (2, 4, 8, 16)
POOL_PAD = 16
C_WIDTH = 512
C_CONV = 31
CONV_PAD = 16
GRID_W = 64
P_HEADS = 8
P_KEYS = 128
P_TOPK = 16
P_HALF = 128
P_EXPERTS = P_KEYS * P_KEYS
EPS = 1e-6
GN_EPS = 64e-5
LN_EPS = 1e-5

LANES = 128
VMEM_LIMIT = 56 * 1024 * 1024

TM_PROJ = 512
TM_PEER = 512
PEER_EB = 1024
SCAN_TB = 8


def _cparams(*sem):
    return pltpu.CompilerParams(dimension_semantics=sem, vmem_limit_bytes=VMEM_LIMIT)


def _sigmoid(x):
    return 1.0 / (1.0 + jnp.exp(-x))


def _dot(a, b):
    return jnp.dot(a, b, preferred_element_type=F32)


def _segsum(x, ones_bd):
    hi = x.astype(BF16)
    lo = (x - hi.astype(F32)).astype(BF16)
    return _dot(hi, ones_bd) + _dot(lo, ones_bd)


def _mod_kernel(c_ref, w_ref, b_ref, o_ref):
    c = c_ref[...]
    s = c * _sigmoid(c)
    o_ref[...] = jnp.dot(s, w_ref[...], preferred_element_type=F32,
                         precision=lax.Precision.HIGHEST) + b_ref[...]


def _modulation(cond8, ada_w, ada_b):
    L = ada_w.shape[0]
    nb = 6
    return pl.pallas_call(
        _mod_kernel,
        grid=(L, nb),
        in_specs=[pl.BlockSpec((8, D_MODEL), lambda l, j: (0, 0)),
                  pl.BlockSpec((None, D_MODEL, D_MODEL), lambda l, j: (l, 0, j)),
                  pl.BlockSpec((None, 1, D_MODEL), lambda l, j: (l, 0, j))],
        out_specs=pl.BlockSpec((None, 8, D_MODEL), lambda l, j: (l, 0, j)),
        out_shape=jax.ShapeDtypeStruct((L, 8, 6 * D_MODEL), F32),
        compiler_params=_cparams("parallel", "parallel"),
        name="adaln_mod",
    )(cond8, ada_w, ada_b.reshape(L, 1, 6 * D_MODEL))


def _mod_row_map(n_ctx_tiles, tiles_per_dec_seq):
    def index_map(i):
        row = jnp.where(i < n_ctx_tiles, 0, 1 + (i - n_ctx_tiles) // tiles_per_dec_seq)
        return (row, 0, 0)
    return index_map


def _inproj_kernel(x_ref, mod_ref, g_ref, w_ref, *o_refs, edge_tile):
    x = x_ref[...]
    ms = jnp.mean(x * x, axis=-1, keepdims=True)
    y = x * lax.rsqrt(ms + EPS) * g_ref[...]
    sh = mod_ref[0, :, 0:D_MODEL]
    sc = mod_ref[0, :, D_MODEL:2 * D_MODEL]
    h = (y * (1.0 + sc) + sh).astype(BF16)
    if edge_tile:
        *o_refs, edge_ref = o_refs
    off = 0
    for o_ref in o_refs:
        n = o_ref.shape[1]
        o_ref[...] = _dot(h, w_ref[:, off:off + n])
        off += n
    if edge_tile:
        first = o_refs[0]
        for t in range(x.shape[0] // edge_tile):
            edge_ref[t, 0:1, :] = first[t * edge_tile:t * edge_tile + 1, :]
            edge_ref[t, 1:2, :] = first[(t + 1) * edge_tile - 1:(t + 1) * edge_tile, :]


def _inproj(x, mod3, g, w, splits, mod_map, name, edge_tile=0):
    n_tok = x.shape[0]
    n_out = w.shape[1]
    out_specs = [pl.BlockSpec((TM_PROJ, n), lambda i: (i, 0)) for n in splits]
    out_shape = [jax.ShapeDtypeStruct((n_tok, n), F32) for n in splits]
    if edge_tile:
        per = TM_PROJ // edge_tile
        out_specs.append(pl.BlockSpec((per, 2, splits[0]), lambda i: (i, 0, 0)))
        out_shape.append(jax.ShapeDtypeStruct((n_tok // edge_tile, 2, splits[0]), F32))
    return pl.pallas_call(
        functools.partial(_inproj_kernel, edge_tile=edge_tile),
        grid=(n_tok // TM_PROJ,),
        in_specs=[pl.BlockSpec((TM_PROJ, D_MODEL), lambda i: (i, 0)),
                  pl.BlockSpec((1, 1, 6 * D_MODEL), mod_map),
                  pl.BlockSpec((1, D_MODEL), lambda i: (0, 0)),
                  pl.BlockSpec((D_MODEL, n_out), lambda i: (0, 0))],
        out_specs=out_specs,
        out_shape=out_shape,
        compiler_params=_cparams("parallel"),
        name=name,
    )(x, mod3, g, w)


def _prep_kernel(pa_ref, hp_ref, hn_ref, mu_ref, w0_ref, w2_ref, a0_ref, a2_ref, g2_ref,
                 kkw_ref, ka_ref, rk_ref, ones_ref,
                 r_o, v_o, kk_o, dec_o, kd_o, bb_o, bv_o, g_o):
    tm = pa_ref.shape[0]

    def shifted(c0, c1):
        x = pa_ref[:, c0:c1]
        row = lax.broadcasted_iota(jnp.int32, x.shape, 0)
        prev = jnp.where(row == 0, hp_ref[0, :, c0:c1], pltpu.roll(x, 1, 0))
        nxt = jnp.where(row == tm - 1, hn_ref[0, :, c0:c1], pltpu.roll(x, tm - 1, 0))
        return x + (0.5 * (prev + nxt) - x) * mu_ref[:, c0:c1]

    d = D_MODEL
    r = shifted(0, d)
    k = shifted(d, 2 * d)
    v = shifted(2 * d, 3 * d)
    lo = shifted(3 * d, 3 * d + A_LORA)
    w_lo = lo[:, 0:128]
    a_lo = lo[:, 128:256]
    g_lo = lo[:, 256:384]
    ones_bd = ones_ref[...]

    r_o[...] = r
    v_o[...] = v

    kkraw = k * kkw_ref[...]
    nrm = jnp.sqrt(_segsum(kkraw * kkraw, ones_bd))
    kk = kkraw / jnp.maximum(nrm, 1e-12)
    kk_o[...] = kk

    z = -(w0_ref[...] + _dot(jnp.tanh(w_lo).astype(BF16), w2_ref[...]))
    softplus = jnp.maximum(z, 0.0) + jnp.log(1.0 + jnp.exp(-jnp.abs(z)))
    dec_o[...] = jnp.exp(-jnp.exp(-softplus - 0.5))

    a = _sigmoid(a0_ref[...] + _dot(a_lo.astype(BF16), a2_ref[...]))
    ka = ka_ref[...]
    kd0 = k * (1.0 + (a[:, 0:d] - 1.0) * ka)
    kd1 = k * (1.0 + (a[:, d:2 * d] - 1.0) * ka)
    kd_o[:, 0:d] = kd0
    kd_o[:, d:2 * d] = kd1
    bb_o[:, 0:d] = kk * a[:, 0:d]
    bb_o[:, d:2 * d] = kk * a[:, d:2 * d]

    bonus = _segsum(r * (kd0 + kd1) * rk_ref[...], ones_bd)
    bv_o[...] = bonus * v
    g_o[...] = _dot(_sigmoid(g_lo).astype(BF16), g2_ref[...])


def _rwkv_prep(pa, halo_prev, halo_next, tm, params):
    n_tok = pa.shape[0]
    d = D_MODEL
    row = lambda n: pl.BlockSpec((1, n), lambda i: (0, 0))
    full = lambda a, b: pl.BlockSpec((a, b), lambda i: (0, 0))
    tile = lambda n: pl.BlockSpec((tm, n), lambda i: (i, 0))
    halo = pl.BlockSpec((1, 1, A_COLS), lambda i: (i, 0, 0))
    outs = [d, d, d, 2 * d, 2 * d, 2 * d, d, d]
    return pl.pallas_call(
        _prep_kernel,
        grid=(n_tok // tm,),
        in_specs=[tile(A_COLS), halo, halo, row(A_COLS), row(2 * d), full(128, 2 * d), row(2 * d),
                  full(128, 2 * d), full(128, d), row(d), row(d), row(d), full(d, d)],
        out_specs=[tile(n) for n in outs],
        out_shape=[jax.ShapeDtypeStruct((n_tok, n), F32) for n in outs],
        compiler_params=_cparams("parallel"),
        name="rwkv_prep",
    )(pa, halo_prev, halo_next, *params)


SCAN_ROWS = 16


def _scan_kernel(kkf, kkb, rf, rb, wf, wb, bf, bb, kdf, kdb, vf, vb, s0_ref,
                 yf_ref, yb_ref, sfin_ref, s_scr):
    nk, tb = kkf.shape[0], kkf.shape[1]
    j = pl.program_id(2)

    @pl.when(j == 0)
    def _():
        s_scr[...] = s0_ref[...]

    def advance(d, t, rs, kk_ref, r_ref, w_ref, b_ref, kd_ref, v_ref, y_ref):
        parts = [None] * 4
        for k in range(nk):
            term = s_scr[d, k, rs, :] * kk_ref[k, pl.ds(t, 1), :]
            parts[k % 4] = term if parts[k % 4] is None else parts[k % 4] + term
        sa = -((parts[0] + parts[1]) + (parts[2] + parts[3]))
        vt = v_ref[t, rs, :]
        ys = [None] * 2
        for k in range(nk):
            s_new = (s_scr[d, k, rs, :] * w_ref[k, pl.ds(t, 1), :] + sa * b_ref[k, pl.ds(t, 1), :]
                     + vt * kd_ref[k, pl.ds(t, 1), :])
            s_scr[d, k, rs, :] = s_new
            term = s_new * r_ref[k, pl.ds(t, 1), :]
            ys[k % 2] = term if ys[k % 2] is None else ys[k % 2] + term
        y_ref[t, rs, :] = ys[0] + ys[1]

    rs = slice(None)

    def step(i, carry):
        advance(0, i, rs, kkf, rf, wf, bf, kdf, vf, yf_ref)
        advance(1, tb - 1 - i, rs, kkb, rb, wb, bb, kdb, vb, yb_ref)
        return carry

    lax.fori_loop(0, tb, step, 0)

    @pl.when(j == pl.num_programs(2) - 1)
    def _():
        sfin_ref[...] = s_scr[...]


def _rwkv_scan(kk_c, r_c, w_c, b_c, kd_c, v_c, s0_c):
    nk, T, L = kk_c.shape
    vq = v_c.shape[1]
    tb = min(SCAN_TB, T)
    nt = T // tb
    fwd = lambda j: j
    bwd = lambda j: nt - 1 - j
    rows = min(vq, SCAN_ROWS)
    shared = lambda tj: pl.BlockSpec((nk, tb, LANES), lambda g, q, j: (0, tj(j), g))
    perdir = lambda d, tj: pl.BlockSpec((None, nk, tb, LANES), lambda g, q, j: (d, 0, tj(j), g))
    vspec = lambda tj: pl.BlockSpec((tb, rows, LANES), lambda g, q, j: (tj(j), q, g))
    sspec = pl.BlockSpec((2, nk, rows, LANES), lambda g, q, j: (0, 0, q, g))
    y_shape = jax.ShapeDtypeStruct((T, vq, L), F32)
    return pl.pallas_call(
        _scan_kernel,
        grid=(L // LANES, vq // rows, nt),
        in_specs=[shared(fwd), shared(bwd), shared(fwd), shared(bwd),
                  perdir(0, fwd), perdir(1, bwd), perdir(0, fwd), perdir(1, bwd),
                  perdir(0, fwd), perdir(1, bwd), vspec(fwd), vspec(bwd), sspec],
        out_specs=[vspec(fwd), vspec(bwd), sspec],
        out_shape=[y_shape, y_shape, jax.ShapeDtypeStruct((2, nk, vq, L), F32)],
        scratch_shapes=[pltpu.VMEM((2, nk, rows, LANES), F32)],
        compiler_params=_cparams("parallel", "parallel", "arbitrary"),
        name="rwkv_scan",
    )(kk_c, kk_c, r_c, r_c, w_c, w_c, b_c, b_c, kd_c, kd_c, v_c, v_c, s0_c)


def _lane_plan(n_batch):
    full = LANES // A_HEADS
    if n_batch >= full:
        assert n_batch % full == 0
        return full, 1
    assert full % n_batch == 0
    return n_batch, full // n_batch


def _to_chain_kernel(x_ref, o_ref, scr, *, vs, value_rows):
    nb, tc = x_ref.shape[0], x_ref.shape[1]
    for b in range(nb):
        scr[b * A_HEADS:(b + 1) * A_HEADS] = x_ref[b].T.reshape(A_HEADS, A_HEAD, tc)
    vq = A_HEAD // vs
    if value_rows:
        for v in range(vq):
            rows = [scr[:, vh * vq + v, :] for vh in range(vs)]
            rows = rows[0] if vs == 1 else jnp.concatenate(rows, axis=0)
            o_ref[:, v, :] = rows.T
    else:
        for k in range(A_HEAD):
            rows = scr[:, k, :]
            if vs > 1:
                rows = jnp.concatenate([rows] * vs, axis=0)
            o_ref[k] = rows.T


def _to_chain(x, seq_len, batch0, n_batch, value_rows):
    n_col = x.shape[1] // D_MODEL
    nb, vs = _lane_plan(n_batch)
    assert batch0 % nb == 0 and seq_len % LANES == 0
    groups = n_batch // nb
    tc = LANES
    x3 = x.reshape(x.shape[0] // seq_len, seq_len, n_col * D_MODEL)
    vq = A_HEAD // vs
    if value_rows:
        out_spec = pl.BlockSpec((None, tc, vq, LANES), lambda c, g, j: (c, j, 0, g))
        out_shape = jax.ShapeDtypeStruct((n_col, seq_len, vq, groups * LANES), F32)
    else:
        out_spec = pl.BlockSpec((None, A_HEAD, tc, LANES), lambda c, g, j: (c, 0, j, g))
        out_shape = jax.ShapeDtypeStruct((n_col, A_HEAD, seq_len, groups * LANES), F32)
    return pl.pallas_call(
        functools.partial(_to_chain_kernel, vs=vs, value_rows=value_rows),
        grid=(n_col, groups, seq_len // tc),
        in_specs=[pl.BlockSpec((nb, tc, D_MODEL), lambda c, g, j: (batch0 // nb + g, j, c))],
        out_specs=out_spec,
        out_shape=out_shape,
        scratch_shapes=[pltpu.VMEM((nb * A_HEADS, A_HEAD, tc), F32)],
        compiler_params=_cparams("parallel", "parallel", "parallel"),
        name="to_chain_v" if value_rows else "to_chain_k",
    )(x3)


def _from_chain_kernel(*refs, vs):
    yf_ref, yb_ref, o_ref, scr = refs[-4:]
    nb, tc = o_ref.shape[0], o_ref.shape[1]
    vq = A_HEAD // vs
    rows = nb * A_HEADS
    for v in range(vq):
        yv = (yf_ref[:, v, :] + yb_ref[:, v, :]).T
        for vh in range(vs):
            scr[:, vh * vq + v, :] = yv[vh * rows:(vh + 1) * rows]
    for b in range(nb):
        o_ref[b] = scr[b * A_HEADS:(b + 1) * A_HEADS].reshape(D_MODEL, tc).T


def _from_chain(y_fwd, y_bwd, prev_rows, n_tok, seq_len, batch0, n_batch):
    nb, vs = _lane_plan(n_batch)
    groups = n_batch // nb
    tc = LANES
    vq = A_HEAD // vs
    in_specs = [pl.BlockSpec((tc, vq, LANES), lambda g, j: (j, 0, g))] * 2
    args = [y_fwd, y_bwd]
    aliases = {}
    if prev_rows is not None:
        in_specs = [pl.BlockSpec(memory_space=pl.ANY)] + in_specs
        args = [prev_rows.reshape(n_tok // seq_len, seq_len, D_MODEL)] + args
        aliases = {0: 0}
    out = pl.pallas_call(
        functools.partial(_from_chain_kernel, vs=vs),
        grid=(groups, seq_len // tc),
        in_specs=in_specs,
        out_specs=pl.BlockSpec((nb, tc, D_MODEL), lambda g, j: (batch0 // nb + g, j, 0)),
        out_shape=jax.ShapeDtypeStruct((n_tok // seq_len, seq_len, D_MODEL), F32),
        scratch_shapes=[pltpu.VMEM((nb * A_HEADS, A_HEAD, tc), F32)],
        input_output_aliases=aliases,
        compiler_params=_cparams("parallel", "parallel"),
        name="from_chain",
    )(*args)
    return out.reshape(n_tok, D_MODEL)


def _state_to_chain(s, vs):
    B = s.shape[0]
    vq = A_HEAD // vs
    s = jnp.transpose(s, (1, 4, 3, 0, 2)).reshape(2, A_HEAD, vs, vq, B * A_HEADS)
    return jnp.transpose(s, (0, 1, 3, 2, 4)).reshape(2, A_HEAD, vq, vs * B * A_HEADS)


def _state_from_chain(s_c, n_batch, vs):
    vq = A_HEAD // vs
    s = s_c.reshape(2, A_HEAD, vq, vs, n_batch, A_HEADS)
    return jnp.transpose(s, (4, 0, 5, 3, 2, 1)).reshape(n_batch, 2, A_HEADS, A_HEAD, A_HEAD)


def _pool_tail(d2, g, pw_ref, pb_ref, ps_ref):
    sl = slice(g * B_GROUP, (g + 1) * B_GROUP)
    y = _dot(d2.astype(BF16), pw_ref[g]) + pb_ref[:, sl]
    return y * ps_ref[:, sl]


def _window_count(t, w, n):
    lo = jnp.clip(t - w // 2, 0, n)
    hi = jnp.clip(t - w // 2 + w, 0, n)
    return (hi - lo).astype(F32)


def _pool_seq_kernel(z_ref, pw_ref, pb_ref, ps_ref, o_ref):
    n = z_ref.shape[0]
    pad = jnp.zeros((POOL_PAD, B_GROUP), F32)
    t = lax.broadcasted_iota(jnp.int32, (n, B_GROUP), 0)
    for g, w in enumerate(POOL_WINDOWS):
        sl = slice(g * B_GROUP, (g + 1) * B_GROUP)
        z = z_ref[:, sl]
        zp = jnp.concatenate([pad, z, pad], axis=0)
        acc = None
        for j in range(w):
            off = j - w // 2
            term = zp if off == 0 else pltpu.roll(zp, (-off) % (n + 2 * POOL_PAD), 0)
            acc = term if acc is None else acc + term
        d = acc[POOL_PAD:POOL_PAD + n] / _window_count(t, w, n) - z
        o_ref[:, sl] = _pool_tail(d, g, pw_ref, pb_ref, ps_ref)


def _pool_grid_kernel(z_ref, pw_ref, pb_ref, ps_ref, _, o_ref):
    rows, cb = z_ref.shape[0], z_ref.shape[1]
    pad = jnp.zeros((POOL_PAD, cb, B_GROUP), F32)
    t = lax.broadcasted_iota(jnp.int32, (rows, cb, B_GROUP), 0)
    for g, w in enumerate(POOL_WINDOWS):
        sl = slice(g * B_GROUP, (g + 1) * B_GROUP)
        z = z_ref[:, :, sl]
        zp = jnp.concatenate([pad, z, pad], axis=0)
        acc = None
        for j in range(w):
            s = POOL_PAD + j - w // 2
            term = zp[s:s + rows]
            acc = term if acc is None else acc + term
        d = acc / _window_count(t, w, rows) - z
        y = _pool_tail(d.reshape(rows * cb, B_GROUP), g, pw_ref, pb_ref, ps_ref)
        o_ref[:, :, sl] = y.reshape(rows, cb, B_GROUP)


def _pool_mix(pb, n_ctx, seq, dec_seq, pw, pbias, pscale):
    n_tok = pb.shape[0]
    wspecs = lambda nd: [pl.BlockSpec((4, B_GROUP, B_GROUP), lambda *a: (0, 0, 0)),
                         pl.BlockSpec((1, B_WIDTH), lambda *a: (0, 0)),
                         pl.BlockSpec((1, B_WIDTH), lambda *a: (0, 0))]
    y_ctx = pl.pallas_call(
        _pool_seq_kernel,
        grid=(n_ctx // seq,),
        in_specs=[pl.BlockSpec((seq, B_WIDTH), lambda i: (i, 0))] + wspecs(1),
        out_specs=pl.BlockSpec((seq, B_WIDTH), lambda i: (i, 0)),
        out_shape=jax.ShapeDtypeStruct((n_tok, B_WIDTH), F32),
        compiler_params=_cparams("parallel"),
        name="pool_seq",
    )(pb, pw, pbias, pscale)
    rows = dec_seq // GRID_W
    n_dec = (n_tok - n_ctx) // dec_seq
    ctx_blocks = n_ctx // dec_seq
    cb = 8
    shape4 = (n_tok // dec_seq, rows, GRID_W, B_WIDTH)
    blk = pl.BlockSpec((None, rows, cb, B_WIDTH), lambda b, c: (b + ctx_blocks, 0, c, 0))
    y_all = pl.pallas_call(
        _pool_grid_kernel,
        grid=(n_dec, GRID_W // cb),
        in_specs=[blk] + wspecs(2) + [pl.BlockSpec(memory_space=pl.ANY)],
        out_specs=blk,
        out_shape=jax.ShapeDtypeStruct(shape4, F32),
        input_output_aliases={4: 0},
        compiler_params=_cparams("parallel", "parallel"),
        name="pool_grid",
    )(pb.reshape(shape4), pw, pbias, pscale, y_ctx.reshape(shape4))
    return y_all.reshape(n_tok, B_WIDTH)


def _conv_kernel(pc_ref, cw_ref, cb_ref, lg_ref, lb_ref, o_ref, *, n_ctx_tiles):
    tm = pc_ref.shape[0]
    u = pc_ref[:, 0:C_WIDTH] * _sigmoid(pc_ref[:, C_WIDTH:2 * C_WIDTH])
    pad = jnp.zeros((CONV_PAD, C_WIDTH), F32)
    up = jnp.concatenate([pad, u, pad], axis=0)
    is_ctx = pl.program_id(0) < n_ctx_tiles
    t = lax.broadcasted_iota(jnp.int32, (tm, C_WIDTH), 0)
    pos = jnp.where(is_ctx, t, t & (GRID_W - 1))
    seg = jnp.where(is_ctx, tm, GRID_W)
    acc = jnp.zeros((tm, C_WIDTH), F32)
    for j in range(C_CONV):
        off = j - C_CONV // 2
        rolled = up if off == 0 else pltpu.roll(up, (-off) % (tm + 2 * CONV_PAD), 0)
        tap = rolled[CONV_PAD:CONV_PAD + tm]
        if off > 0:
            tap = jnp.where(pos < seg - off, tap, 0.0)
        elif off < 0:
            tap = jnp.where(pos >= -off, tap, 0.0)
        acc = acc + tap * cw_ref[pl.ds(j, 1), :]
    c = acc + cb_ref[...]
    mu = jnp.mean(c, axis=-1, keepdims=True)
    cc = c - mu
    var = jnp.mean(cc * cc, axis=-1, keepdims=True)
    y = cc * lax.rsqrt(var + LN_EPS) * lg_ref[...] + lb_ref[...]
    o_ref[...] = y * _sigmoid(y)


def _conv_mix(pc, tm, n_ctx_tiles, cw, cb, lg, lb):
    n_tok = pc.shape[0]
    row = pl.BlockSpec((1, C_WIDTH), lambda i: (0, 0))
    return pl.pallas_call(
        functools.partial(_conv_kernel, n_ctx_tiles=n_ctx_tiles),
        grid=(n_tok // tm,),
        in_specs=[pl.BlockSpec((tm, 2 * C_WIDTH), lambda i: (i, 0)),
                  pl.BlockSpec((C_CONV, C_WIDTH), lambda i: (0, 0)), row, row, row],
        out_specs=pl.BlockSpec((tm, C_WIDTH), lambda i: (i, 0)),
        out_shape=jax.ShapeDtypeStruct((n_tok, C_WIDTH), F32),
        compiler_params=_cparams("parallel"),
        name="conv_mix",
    )(pc, cw, cb, lg, lb)


def _merge_kernel(y_ref, bv_ref, g_ref, yb_ref, yc_ref, pg_ref, x_ref, mod_ref, gng_ref, gnb_ref,
                  n2g_ref, wa_ref, wb_ref, wc_ref, wo_ref, ones_ref, xo_ref, h2_ref):
    d = D_MODEL
    ones_bd = ones_ref[...]
    y = y_ref[...]
    mean = _segsum(y, ones_bd) * (1.0 / A_HEAD)
    yc = y - mean
    var = _segsum(yc * yc, ones_bd) * (1.0 / A_HEAD)
    yn = yc * lax.rsqrt(var + GN_EPS) * gng_ref[...] + gnb_ref[...]
    ya = (yn + bv_ref[...]) * g_ref[...]
    merged = (_sigmoid(pg_ref[:, 0:d]) * _dot(ya.astype(BF16), wa_ref[...])
              + _sigmoid(pg_ref[:, d:2 * d]) * _dot(yb_ref[...].astype(BF16), wb_ref[...])
              + _sigmoid(pg_ref[:, 2 * d:3 * d]) * _dot(yc_ref[...].astype(BF16), wc_ref[...]))
    gt1 = mod_ref[0, :, 2 * d:3 * d]
    xn = x_ref[...] + gt1 * _dot(merged.astype(BF16), wo_ref[...])
    xo_ref[...] = xn
    ms = jnp.mean(xn * xn, axis=-1, keepdims=True)
    h2 = xn * lax.rsqrt(ms + EPS) * n2g_ref[...]
    sh2 = mod_ref[0, :, 3 * d:4 * d]
    sc2 = mod_ref[0, :, 4 * d:5 * d]
    h2_ref[...] = (h2 * (1.0 + sc2) + sh2).astype(BF16)


def _merge(y, bv, g, yb, yc, pg, x, mod3, mod_map, tm, gng, gnb, n2g, wa, wb, wc, wo, ones_bd):
    n_tok = x.shape[0]
    d = D_MODEL
    tile = lambda n: pl.BlockSpec((tm, n), lambda i: (i, 0))
    row = pl.BlockSpec((1, d), lambda i: (0, 0))
    full = lambda a: pl.BlockSpec((a, d), lambda i: (0, 0))
    return pl.pallas_call(
        _merge_kernel,
        grid=(n_tok // tm,),
        in_specs=[tile(d), tile(d), tile(d), tile(B_WIDTH), tile(C_WIDTH), tile(3 * d), tile(d),
                  pl.BlockSpec((1, 1, 6 * d), mod_map), row, row, row,
                  full(d), full(B_WIDTH), full(C_WIDTH), full(d), full(d)],
        out_specs=[tile(d), tile(d)],
        out_shape=[jax.ShapeDtypeStruct((n_tok, d), F32), jax.ShapeDtypeStruct((n_tok, d), BF16)],
        compiler_params=_cparams("parallel"),
        name="merge_outproj",
    )(y, bv, g, yb, yc, pg, x, mod3, gng, gnb, n2g, wa, wb, wc, wo, ones_bd)


NO_RANK = 1e9
RANK_BASE = -2.0 ** 100


def _peer_score_kernel(h_ref, wq_ref, keys_ref, rk0_o, e0_o, th1_o, e1_o, s_scr, top_scr, rank_scr):
    tmp = h_ref.shape[0]
    q_t = lax.dot_general(wq_ref[...], h_ref[...], (((1,), (1,)), ((), ())),
                          preferred_element_type=F32).astype(BF16)
    for hp in range(2 * P_HEADS):
        s_scr[hp] = _dot(keys_ref[hp], q_t[hp * P_HALF:(hp + 1) * P_HALF, :])

    neg = jnp.float32(-jnp.inf)
    half = P_TOPK // 2
    row16 = lax.broadcasted_iota(jnp.int32, (P_TOPK, LANES), 0)
    row8 = lax.broadcasted_iota(jnp.int32, (half, LANES), 0)
    n_groups = tmp // LANES

    def top16_pair(sets):
        vals = [jnp.full((P_TOPK, LANES), neg, F32) for _ in sets]
        sets = list(sets)
        for it in range(P_TOPK):
            for n in range(len(sets)):
                m = jnp.max(sets[n], axis=0, keepdims=True)
                vals[n] = jnp.where(row16 == it, m, vals[n])
                sets[n] = jnp.where(sets[n] >= m, RANK_BASE * (1.0 + it / P_TOPK), sets[n])
        ranks = [jnp.where(s <= RANK_BASE, s * (P_TOPK / RANK_BASE) - P_TOPK, NO_RANK) for s in sets]
        return vals, ranks

    n_items = P_HEADS * n_groups

    def item(idx):
        return idx // n_groups, pl.ds(pl.multiple_of((idx % n_groups) * LANES, LANES), LANES)

    def extract(idx):
        h, ls = item(idx)
        (a16, b16), (rank0, rank1) = top16_pair((s_scr[2 * h, :, ls], s_scr[2 * h + 1, :, ls]))
        rk0_o[h, :, ls] = rank0
        top_scr[0] = a16
        top_scr[1] = b16
        rank_scr[...] = rank1

    def combine(idx):
        h, ls = item(idx)
        a16 = top_scr[0]
        b16 = top_scr[1]
        rank1 = rank_scr[...]
        s0 = s_scr[2 * h, :, ls]
        s1 = s_scr[2 * h + 1, :, ls]
        b_lo = b16[0:half, :]
        cands = [a16[0:1, :] + b16, a16[1:2, :] + b_lo]
        for r in range(2, half):
            cands.append(jnp.where(row8 < P_TOPK // (r + 1), a16[r:r + 1, :] + b_lo, neg))
        cands.append(a16[half:P_TOPK, :] + b16[0:1, :])
        cmax = a16[0:1, :] + b16[0:1, :]
        work = list(cands)
        tau = cmax
        for it in range(P_TOPK):
            m = jnp.maximum(work[0][0:half, :], work[0][half:P_TOPK, :])
            for c in work[1:]:
                m = jnp.maximum(m, c)
            tau = jnp.max(m, axis=0, keepdims=True)
            if it + 1 < P_TOPK:
                work = [jnp.where(c >= tau, neg, c) for c in work]
        z = jnp.zeros((1, LANES), F32)
        sels = []
        for c in cands:
            sel = c >= tau
            z = z + jnp.sum(jnp.where(sel, jnp.exp(c - cmax), 0.0), axis=0, keepdims=True)
            sels.append(jnp.where(sel, 1.0, 0.0))
        cnt_hi = sels[0][half:P_TOPK, :]
        cnt_lo = sels[0][0:half, :]
        for c in sels[1:half]:
            cnt_lo = cnt_lo + c
        cnt_lo = cnt_lo + jnp.where(row8 == 0, jnp.sum(sels[half], axis=0, keepdims=True), 0.0)
        th1 = jnp.zeros(rank1.shape, F32)
        for q in range(P_TOPK):
            cnt = cnt_lo[q:q + 1, :] if q < half else cnt_hi[q - half:q - half + 1, :]
            th1 = jnp.where(rank1 == float(q), cnt, th1)
        th1_o[h, :, ls] = th1.astype(BF16)
        e0_o[h, :, ls] = jnp.exp(s0 - a16[0:1, :]) / z
        e1_o[h, :, ls] = jnp.exp(s1 - b16[0:1, :]).astype(BF16)

    top_scr[...] = jnp.zeros_like(top_scr)
    rank_scr[...] = jnp.zeros_like(rank_scr)

    def body(n, carry):
        combine(jnp.maximum(n - 1, 0))
        extract(jnp.minimum(n, n_items - 1))
        return carry

    lax.fori_loop(0, n_items + 1, body, 0)


def _peer_scores(h2, wq_t, keys):
    n_tok = h2.shape[0]
    tmp = TM_PEER
    big = pl.BlockSpec((P_HEADS, P_KEYS, tmp), lambda i: (0, 0, i))
    big_shape = jax.ShapeDtypeStruct((P_HEADS, P_KEYS, n_tok), F32)
    narrow_shape = jax.ShapeDtypeStruct((P_HEADS, P_KEYS, n_tok), BF16)
    return pl.pallas_call(
        _peer_score_kernel,
        grid=(n_tok // tmp,),
        in_specs=[pl.BlockSpec((tmp, D_MODEL), lambda i: (i, 0)),
                  pl.BlockSpec((2 * P_HEADS * P_HALF, D_MODEL), lambda i: (0, 0)),
                  pl.BlockSpec((2 * P_HEADS, P_KEYS, P_HALF), lambda i: (0, 0, 0))],
        out_specs=[big, big, big, big],
        out_shape=[big_shape, big_shape, narrow_shape, narrow_shape],
        scratch_shapes=[pltpu.VMEM((2 * P_HEADS, P_KEYS, tmp), F32),
                        pltpu.VMEM((2, P_TOPK, LANES), F32), pltpu.VMEM((P_KEYS, LANES), F32)],
        compiler_params=_cparams("parallel"),
        name="peer_scores",
    )(h2, wq_t, keys)


def _peer_expert_kernel(h_ref, u_ref, vt_ref, rk0_ref, e0_ref, th1_ref, e1_ref, x_ref,
                        mod_ref, *rest, final_norm):
    if final_norm:
        fg_ref, o_ref, acc_ref, gated_ref = rest
    else:
        o_ref, acc_ref, gated_ref = rest
    e = pl.program_id(1)
    tmp = h_ref.shape[0]

    @pl.when(e == 0)
    def _():
        acc_ref[...] = jnp.zeros_like(acc_ref)

    for il in range(PEER_EB // P_KEYS):
        rows = slice(il * P_KEYS, (il + 1) * P_KEYS)
        for lc in range(tmp // LANES):
            ls = slice(lc * LANES, (lc + 1) * LANES)
            w = None
            for h in range(P_HEADS):
                rk0 = jnp.broadcast_to(rk0_ref[h, pl.ds(il, 1), ls], (P_KEYS, LANES)).astype(BF16)
                e0 = jnp.broadcast_to(e0_ref[h, pl.ds(il, 1), ls], (P_KEYS, LANES)).astype(BF16)
                term = jnp.where(rk0 < th1_ref[h, :, ls], e0 * e1_ref[h, :, ls], jnp.zeros((), BF16))
                w = term if w is None else w + term
            gated_ref[rows, ls] = w
    hid = lax.dot_general(u_ref[...], h_ref[...], (((1,), (1,)), ((), ())),
                          preferred_element_type=F32)
    act = 0.5 * hid * (1.0 + lax.erf(hid * (2.0 ** -0.5)))
    acc_ref[...] += _dot(vt_ref[...], gated_ref[...] * act.astype(BF16))

    @pl.when(e == pl.num_programs(1) - 1)
    def _():
        gt2 = mod_ref[0, :, 5 * D_MODEL:6 * D_MODEL]
        xn = x_ref[...] + gt2 * acc_ref[...].T
        if final_norm:
            ms = jnp.mean(xn * xn, axis=-1, keepdims=True)
            xn = xn * lax.rsqrt(ms + EPS) * fg_ref[...]
        o_ref[...] = xn


def _peer_experts(h2, u_bf, vt_bf, rk0, e0, th1, e1, x, mod3, mod_map, final_g=None):
    n_tok = h2.shape[0]
    tmp = TM_PEER
    n_i = PEER_EB // P_KEYS
    sel = pl.BlockSpec((P_HEADS, n_i, tmp), lambda i, e: (0, e, i))
    full = pl.BlockSpec((P_HEADS, P_KEYS, tmp), lambda i, e: (0, 0, i))
    mod_map2 = lambda i, e: mod_map(i)
    tile = pl.BlockSpec((tmp, D_MODEL), lambda i, e: (i, 0))
    out = jax.ShapeDtypeStruct((n_tok, D_MODEL), F32)
    final_norm = final_g is not None
    extra_in = [pl.BlockSpec((1, D_MODEL), lambda i, e: (0, 0))] if final_norm else []
    extra_args = [final_g] if final_norm else []
    return pl.pallas_call(
        functools.partial(_peer_expert_kernel, final_norm=final_norm),
        grid=(n_tok // tmp, P_EXPERTS // PEER_EB),
        in_specs=[tile,
                  pl.BlockSpec((PEER_EB, D_MODEL), lambda i, e: (e, 0)),
                  pl.BlockSpec((D_MODEL, PEER_EB), lambda i, e: (0, e)),
                  sel, sel, full, full, tile,
                  pl.BlockSpec((1, 1, 6 * D_MODEL), mod_map2)] + extra_in,
        out_specs=tile,
        out_shape=out,
        scratch_shapes=[pltpu.VMEM((D_MODEL, tmp), F32), pltpu.VMEM((PEER_EB, tmp), BF16)],
        compiler_params=_cparams("parallel", "arbitrary"),
        name="peer_experts",
    )(h2, u_bf, vt_bf, rk0, e0, th1, e1, x, mod3, *extra_args)


def _block_diag2(w):
    z = jnp.zeros_like(w[0])
    return jnp.concatenate([jnp.concatenate([w[0], z], axis=1),
                            jnp.concatenate([z, w[1]], axis=1)], axis=0)


def kernel(x_prompt, x_sample, state_rwkv, c, c_ctx, ada_w, ada_b, norm1_g, norm2_g, w_in, shift_mu,
           decay_w0, decay_w2, iclr_a0, iclr_a2, gate_g2, k_k, k_a, r_k, gn_g, gn_b, w_a_out,
           pool_w, pool_b, pool_scale, w_b_out, conv_w, conv_b, cln_g, cln_b, w_c_out, w_out,
           peer_wq, peer_keys, peer_u, peer_v, final_g):
    d = D_MODEL
    batch, seq, _ = x_prompt.shape
    dec_batch, dec_seq, _ = x_sample.shape
    depth = ada_w.shape[0]
    n_ctx = batch * seq
    n_dec = dec_batch * dec_seq
    tm = seq
    assert n_ctx % dec_seq == 0 and dec_seq % tm == 0 and dec_seq % GRID_W == 0
    assert n_ctx % TM_PROJ == 0 and dec_seq % TM_PROJ == 0 and dec_seq % TM_PEER == 0
    assert n_ctx % TM_PEER == 0 and tm % GRID_W == 0 and dec_batch <= 7 and TM_PROJ % tm == 0

    x = jnp.concatenate([x_prompt.reshape(n_ctx, d), x_sample.reshape(n_dec, d)], axis=0)
    n_tok = n_ctx + n_dec

    cond8 = jnp.zeros((8, d), F32).at[0].set(c_ctx).at[1:1 + dec_batch].set(c)
    mod_all = _modulation(cond8, ada_w, ada_b)

    seg_ids = jnp.arange(d) // A_HEAD
    ones_bd = (seg_ids[:, None] == seg_ids[None, :]).astype(BF16)

    map_proj = _mod_row_map(n_ctx // TM_PROJ, dec_seq // TM_PROJ)
    map_tm = _mod_row_map(n_ctx // tm, dec_seq // tm)
    map_peer = _mod_row_map(n_ctx // TM_PEER, dec_seq // TM_PEER)

    n_tiles = n_tok // tm
    tile_idx = jnp.arange(n_tiles)
    dec_tile = (tile_idx - n_ctx // tm) % (dec_seq // tm)
    is_start = jnp.where(tile_idx < n_ctx // tm, True, dec_tile == 0)[:, None]
    is_end = jnp.where(tile_idx < n_ctx // tm, True, dec_tile == dec_seq // tm - 1)[:, None]

    ctx_states = []
    for l in range(depth):
        mod3 = mod_all[l][:, None, :]
        w_in_bf = w_in[l].astype(BF16)
        pa, edges = _inproj(x, mod3, norm1_g[l][None], w_in_bf[:, :A_COLS], (A_COLS,), map_proj,
                            "inproj_a", edge_tile=tm)
        pb, pc, pg = _inproj(x, mod3, norm1_g[l][None], w_in_bf[:, A_COLS:],
                             (B_WIDTH, 2 * C_WIDTH, 3 * d), map_proj, "inproj_bcg")

        zero_row = jnp.zeros((1, A_COLS), F32)
        halo_prev = jnp.concatenate([zero_row, edges[:-1, 1, :]], axis=0)
        halo_next = jnp.concatenate([edges[1:, 0, :], zero_row], axis=0)
        halo_prev = jnp.where(is_start, 0.0, halo_prev)[:, None, :]
        halo_next = jnp.where(is_end, 0.0, halo_next)[:, None, :]
        prep_params = (shift_mu[l][None], decay_w0[l].reshape(1, 2 * d),
                       _block_diag2(decay_w2[l]).astype(BF16), iclr_a0[l].reshape(1, 2 * d),
                       _block_diag2(iclr_a2[l]).astype(BF16), gate_g2[l].astype(BF16),
                       k_k[l][None], k_a[l][None], r_k[l].reshape(1, d), ones_bd)
        r, v, kk, dec, kd, bb, bv, g = _rwkv_prep(pa, halo_prev, halo_next, tm, prep_params)

        y_scan = None
        for seq_len, batch0, nb, s0 in ((seq, 0, batch, None),
                                        (dec_seq, n_ctx // dec_seq, dec_batch, state_rwkv[:, l])):
            vs = _lane_plan(nb)[1]
            if s0 is None:
                s0_c = jnp.zeros((2, A_HEAD, A_HEAD // vs, vs * nb * A_HEADS), F32)
            else:
                s0_c = _state_to_chain(s0, vs)
            lay = lambda a, value_rows=False: _to_chain(a, seq_len, batch0, nb, value_rows)
            y_fwd, y_bwd, sfin_c = _rwkv_scan(lay(kk)[0], lay(r)[0], lay(dec), lay(bb), lay(kd),
                                              lay(v, True)[0], s0_c)
            y_scan = _from_chain(y_fwd, y_bwd, y_scan, n_tok, seq_len, batch0, nb)
            if s0 is None:
                ctx_states.append(_state_from_chain(sfin_c, nb, vs))

        yb = _pool_mix(pb, n_ctx, seq, dec_seq, pool_w[l].astype(BF16), pool_b[l].reshape(1, B_WIDTH),
                       pool_scale[l][None])
        yc = _conv_mix(pc, tm, n_ctx // tm, conv_w[l], conv_b[l][None], cln_g[l][None], cln_b[l][None])

        x, h2 = _merge(y_scan, bv, g, yb, yc, pg, x, mod3, map_tm, tm, gn_g[l][None], gn_b[l][None],
                       norm2_g[l][None], w_a_out[l].astype(BF16), w_b_out[l].astype(BF16),
                       w_c_out[l].astype(BF16), w_out[l].astype(BF16), ones_bd)

        wq_t = peer_wq[l].T.astype(BF16)
        keys = peer_keys[l].reshape(2 * P_HEADS, P_KEYS, P_HALF).astype(BF16)
        rk0, e0p, th1, e1p = _peer_scores(h2, wq_t, keys)
        x = _peer_experts(h2, peer_u[l].astype(BF16), peer_v[l].T.astype(BF16), rk0, e0p, th1, e1p,
                          x, mod3, map_peer, final_g[None] if l == depth - 1 else None)

    y = x
    new_state = jnp.stack(ctx_states, axis=1)
    return (y[:n_ctx].reshape(batch, seq, d), y[n_ctx:].reshape(dec_batch, dec_seq, d), new_state)
```

```python
import functools

import jax
import jax.numpy as jnp
from jax import lax
from jax.experimental import pallas as pl
from jax.experimental.pallas import tpu as pltpu

F32 = jnp.float32
BF16 = jnp.bfloat16

D_MODEL = 1024
A_HEADS = 16
A_HEAD = 64
A_LORA = 384
A_COLS = 3 * D_MODEL + A_LORA
B_WIDTH = 512
B_GROUP = 128
POOL_WINDOWS = (2, 4, 8, 16)
POOL_PAD = 16
C_WIDTH = 512
C_CONV = 31
CONV_PAD = 16
GRID_W = 64
P_HEADS = 8
P_KEYS = 128
P_TOPK = 16
P_HALF = 128
P_EXPERTS = P_KEYS * P_KEYS
EPS = 1e-6
GN_EPS = 64e-5
LN_EPS = 1e-5

LANES = 128
VMEM_LIMIT = 56 * 1024 * 1024

TM_PROJ = 512
TM_PEER = 512
PEER_EB = 1024
SCAN_TB = 8


def _cparams(*sem):
    return pltpu.CompilerParams(dimension_semantics=sem, vmem_limit_bytes=VMEM_LIMIT)


def _sigmoid(x):
    return 1.0 / (1.0 + jnp.exp(-x))


def _dot(a, b):
    return jnp.dot(a, b, preferred_element_type=F32)


def _segsum(x, ones_bd):
    hi = x.astype(BF16)
    lo = (x - hi.astype(F32)).astype(BF16)
    return _dot(hi, ones_bd) + _dot(lo, ones_bd)


def _mod_kernel(c_ref, w_ref, b_ref, o_ref):
    c = c_ref[...]
    s = c * _sigmoid(c)
    o_ref[...] = jnp.dot(s, w_ref[...], preferred_element_type=F32,
                         precision=lax.Precision.HIGHEST) + b_ref[...]


def _modulation(cond8, ada_w, ada_b):
    L = ada_w.shape[0]
    nb = 6
    return pl.pallas_call(
        _mod_kernel,
        grid=(L, nb),
        in_specs=[pl.BlockSpec((8, D_MODEL), lambda l, j: (0, 0)),
                  pl.BlockSpec((None, D_MODEL, D_MODEL), lambda l, j: (l, 0, j)),
                  pl.BlockSpec((None, 1, D_MODEL), lambda l, j: (l, 0, j))],
        out_specs=pl.BlockSpec((None, 8, D_MODEL), lambda l, j: (l, 0, j)),
        out_shape=jax.ShapeDtypeStruct((L, 8, 6 * D_MODEL), F32),
        compiler_params=_cparams("parallel", "parallel"),
        name="adaln_mod",
    )(cond8, ada_w, ada_b.reshape(L, 1, 6 * D_MODEL))


def _mod_row_map(n_ctx_tiles, tiles_per_dec_seq):
    def index_map(i):
        row = jnp.where(i < n_ctx_tiles, 0, 1 + (i - n_ctx_tiles) // tiles_per_dec_seq)
        return (row, 0, 0)
    return index_map


def _inproj_kernel(x_ref, mod_ref, g_ref, w_ref, *o_refs, edge_tile):
    x = x_ref[...]
    ms = jnp.mean(x * x, axis=-1, keepdims=True)
    y = x * lax.rsqrt(ms + EPS) * g_ref[...]
    sh = mod_ref[0, :, 0:D_MODEL]
    sc = mod_ref[0, :, D_MODEL:2 * D_MODEL]
    h = (y * (1.0 + sc) + sh).astype(BF16)
    if edge_tile:
        *o_refs, edge_ref = o_refs
    off = 0
    for o_ref in o_refs:
        n = o_ref.shape[1]
        o_ref[...] = _dot(h, w_ref[:, off:off + n])
        off += n
    if edge_tile:
        first = o_refs[0]
        for t in range(x.shape[0] // edge_tile):
            edge_ref[t, 0:1, :] = first[t * edge_tile:t * edge_tile + 1, :]
            edge_ref[t, 1:2, :] = first[(t + 1) * edge_tile - 1:(t + 1) * edge_tile, :]


def _inproj(x, mod3, g, w, splits, mod_map, name, edge_tile=0):
    n_tok = x.shape[0]
    n_out = w.shape[1]
    out_specs = [pl.BlockSpec((TM_PROJ, n), lambda i: (i, 0)) for n in splits]
    out_shape = [jax.ShapeDtypeStruct((n_tok, n), F32) for n in splits]
    if edge_tile:
        per = TM_PROJ // edge_tile
        out_specs.append(pl.BlockSpec((per, 2, splits[0]), lambda i: (i, 0, 0)))
        out_shape.append(jax.ShapeDtypeStruct((n_tok // edge_tile, 2, splits[0]), F32))
    return pl.pallas_call(
        functools.partial(_inproj_kernel, edge_tile=edge_tile),
        grid=(n_tok // TM_PROJ,),
        in_specs=[pl.BlockSpec((TM_PROJ, D_MODEL), lambda i: (i, 0)),
                  pl.BlockSpec((1, 1, 6 * D_MODEL), mod_map),
                  pl.BlockSpec((1, D_MODEL), lambda i: (0, 0)),
                  pl.BlockSpec((D_MODEL, n_out), lambda i: (0, 0))],
        out_specs=out_specs,
        out_shape=out_shape,
        compiler_params=_cparams("parallel"),
        name=name,
    )(x, mod3, g, w)


def _prep_kernel(pa_ref, hp_ref, hn_ref, mu_ref, w0_ref, w2_ref, a0_ref, a2_ref, g2_ref,
                 kkw_ref, ka_ref, rk_ref, ones_ref,
                 r_o, v_o, kk_o, dec_o, kd_o, bb_o, bv_o, g_o):
    tm = pa_ref.shape[0]

    def shifted(c0, c1):
        x = pa_ref[:, c0:c1]
        row = lax.broadcasted_iota(jnp.int32, x.shape, 0)
        prev = jnp.where(row == 0, hp_ref[0, :, c0:c1], pltpu.roll(x, 1, 0))
        nxt = jnp.where(row == tm - 1, hn_ref[0, :, c0:c1], pltpu.roll(x, tm - 1, 0))
        return x + (0.5 * (prev + nxt) - x) * mu_ref[:, c0:c1]

    d = D_MODEL
    r = shifted(0, d)
    k = shifted(d, 2 * d)
    v = shifted(2 * d, 3 * d)
    lo = shifted(3 * d, 3 * d + A_LORA)
    w_lo = lo[:, 0:128]
    a_lo = lo[:, 128:256]
    g_lo = lo[:, 256:384]
    ones_bd = ones_ref[...]

    r_o[...] = r
    v_o[...] = v

    kkraw = k * kkw_ref[...]
    nrm = jnp.sqrt(_segsum(kkraw * kkraw, ones_bd))
    kk = kkraw / jnp.maximum(nrm, 1e-12)
    kk_o[...] = kk

    z = -(w0_ref[...] + _dot(jnp.tanh(w_lo).astype(BF16), w2_ref[...]))
    softplus = jnp.maximum(z, 0.0) + jnp.log(1.0 + jnp.exp(-jnp.abs(z)))
    dec_o[...] = jnp.exp(-jnp.exp(-softplus - 0.5))

    a = _sigmoid(a0_ref[...] + _dot(a_lo.astype(BF16), a2_ref[...]))
    ka = ka_ref[...]
    kd0 = k * (1.0 + (a[:, 0:d] - 1.0) * ka)
    kd1 = k * (1.0 + (a[:, d:2 * d] - 1.0) * ka)
    kd_o[:, 0:d] = kd0
    kd_o[:, d:2 * d] = kd1
    bb_o[:, 0:d] = kk * a[:, 0:d]
    bb_o[:, d:2 * d] = kk * a[:, d:2 * d]

    bonus = _segsum(r * (kd0 + kd1) * rk_ref[...], ones_bd)
    bv_o[...] = bonus * v
    g_o[...] = _dot(_sigmoid(g_lo).astype(BF16), g2_ref[...])


def _rwkv_prep(pa, halo_prev, halo_next, tm, params):
    n_tok = pa.shape[0]
    d = D_MODEL
    row = lambda n: pl.BlockSpec((1, n), lambda i: (0, 0))
    full = lambda a, b: pl.BlockSpec((a, b), lambda i: (0, 0))
    tile = lambda n: pl.BlockSpec((tm, n), lambda i: (i, 0))
    halo = pl.BlockSpec((1, 1, A_COLS), lambda i: (i, 0, 0))
    outs = [d, d, d, 2 * d, 2 * d, 2 * d, d, d]
    return pl.pallas_call(
        _prep_kernel,
        grid=(n_tok // tm,),
        in_specs=[tile(A_COLS), halo, halo, row(A_COLS), row(2 * d), full(128, 2 * d), row(2 * d),
                  full(128, 2 * d), full(128, d), row(d), row(d), row(d), full(d, d)],
        out_specs=[tile(n) for n in outs],
        out_shape=[jax.ShapeDtypeStruct((n_tok, n), F32) for n in outs],
        compiler_params=_cparams("parallel"),
        name="rwkv_prep",
    )(pa, halo_prev, halo_next, *params)


SCAN_ROWS = 16


def _scan_kernel(kkf, kkb, rf, rb, wf, wb, bf, bb, kdf, kdb, vf, vb, s0_ref,
                 yf_ref, yb_ref, sfin_ref, s_scr):
    nk, tb = kkf.shape[0], kkf.shape[1]
    j = pl.program_id(2)

    @pl.when(j == 0)
    def _():
        s_scr[...] = s0_ref[...]

    def advance(d, t, rs, kk_ref, r_ref, w_ref, b_ref, kd_ref, v_ref, y_ref):
        parts = [None] * 4
        for k in range(nk):
            term = s_scr[d, k, rs, :] * kk_ref[k, pl.ds(t, 1), :]
            parts[k % 4] = term if parts[k % 4] is None else parts[k % 4] + term
        sa = -((parts[0] + parts[1]) + (parts[2] + parts[3]))
        vt = v_ref[t, rs, :]
        ys = [None] * 2
        for k in range(nk):
            s_new = (s_scr[d, k, rs, :] * w_ref[k, pl.ds(t, 1), :] + sa * b_ref[k, pl.ds(t, 1), :]
                     + vt * kd_ref[k, pl.ds(t, 1), :])
            s_scr[d, k, rs, :] = s_new
            term = s_new * r_ref[k, pl.ds(t, 1), :]
            ys[k % 2] = term if ys[k % 2] is None else ys[k % 2] + term
        y_ref[t, rs, :] = ys[0] + ys[1]

    rs = slice(None)

    def step(i, carry):
        advance(0, i, rs, kkf, rf, wf, bf, kdf, vf, yf_ref)
        advance(1, tb - 1 - i, rs, kkb, rb, wb, bb, kdb, vb, yb_ref)
        return carry

    lax.fori_loop(0, tb, step, 0)

    @pl.when(j == pl.num_programs(2) - 1)
    def _():
        sfin_ref[...] = s_scr[...]


def _rwkv_scan(kk_c, r_c, w_c, b_c, kd_c, v_c, s0_c):
    nk, T, L = kk_c.shape
    vq = v_c.shape[1]
    tb = min(SCAN_TB, T)
    nt = T // tb
    fwd = lambda j: j
    bwd = lambda j: nt - 1 - j
    rows = min(vq, SCAN_ROWS)
    shared = lambda tj: pl.BlockSpec((nk, tb, LANES), lambda g, q, j: (0, tj(j), g))
    perdir = lambda d, tj: pl.BlockSpec((None, nk, tb, LANES), lambda g, q, j: (d, 0, tj(j), g))
    vspec = lambda tj: pl.BlockSpec((tb, rows, LANES), lambda g, q, j: (tj(j), q, g))
    sspec = pl.BlockSpec((2, nk, rows, LANES), lambda g, q, j: (0, 0, q, g))
    y_shape = jax.ShapeDtypeStruct((T, vq, L), F32)
    return pl.pallas_call(
        _scan_kernel,
        grid=(L // LANES, vq // rows, nt),
        in_specs=[shared(fwd), shared(bwd), shared(fwd), shared(bwd),
                  perdir(0, fwd), perdir(1, bwd), perdir(0, fwd), perdir(1, bwd),
                  perdir(0, fwd), perdir(1, bwd), vspec(fwd), vspec(bwd), sspec],
        out_specs=[vspec(fwd), vspec(bwd), sspec],
        out_shape=[y_shape, y_shape, jax.ShapeDtypeStruct((2, nk, vq, L), F32)],
        scratch_shapes=[pltpu.VMEM((2, nk, rows, LANES), F32)],
        compiler_params=_cparams("parallel", "parallel", "arbitrary"),
        name="rwkv_scan",
    )(kk_c, kk_c, r_c, r_c, w_c, w_c, b_c, b_c, kd_c, kd_c, v_c, v_c, s0_c)


def _lane_plan(n_batch):
    full = LANES // A_HEADS
    if n_batch >= full:
        assert n_batch % full == 0
        return full, 1
    assert full % n_batch == 0
    return n_batch, full // n_batch


SLAB_PITCH = A_HEAD + 4


def _to_chain_kernel(x_ref, o_ref, scr, *, vs, value_rows):
    nb, tc = x_ref.shape[0], x_ref.shape[1]
    n_slabs = nb * A_HEADS
    for b in range(nb):
        xt = x_ref[b].T
        for h in range(A_HEADS):
            scr[pl.ds((b * A_HEADS + h) * SLAB_PITCH, A_HEAD), :] = xt[h * A_HEAD:(h + 1) * A_HEAD]
    row_of_slabs = lambda c: scr[pl.ds(c, n_slabs, stride=SLAB_PITCH), :]
    vq = A_HEAD // vs
    if value_rows:
        for v in range(vq):
            rows = [row_of_slabs(vh * vq + v) for vh in range(vs)]
            rows = rows[0] if vs == 1 else jnp.concatenate(rows, axis=0)
            o_ref[:, v, :] = rows.T
    else:
        for k in range(A_HEAD):
            rows = row_of_slabs(k)
            if vs > 1:
                rows = jnp.concatenate([rows] * vs, axis=0)
            o_ref[k] = rows.T


def _to_chain(x, seq_len, batch0, n_batch, value_rows):
    n_col = x.shape[1] // D_MODEL
    nb, vs = _lane_plan(n_batch)
    assert batch0 % nb == 0 and seq_len % LANES == 0
    groups = n_batch // nb
    tc = LANES
    x3 = x.reshape(x.shape[0] // seq_len, seq_len, n_col * D_MODEL)
    vq = A_HEAD // vs
    if value_rows:
        out_spec = pl.BlockSpec((None, tc, vq, LANES), lambda c, g, j: (c, j, 0, g))
        out_shape = jax.ShapeDtypeStruct((n_col, seq_len, vq, groups * LANES), F32)
    else:
        out_spec = pl.BlockSpec((None, A_HEAD, tc, LANES), lambda c, g, j: (c, 0, j, g))
        out_shape = jax.ShapeDtypeStruct((n_col, A_HEAD, seq_len, groups * LANES), F32)
    return pl.pallas_call(
        functools.partial(_to_chain_kernel, vs=vs, value_rows=value_rows),
        grid=(n_col, groups, seq_len // tc),
        in_specs=[pl.BlockSpec((nb, tc, D_MODEL), lambda c, g, j: (batch0 // nb + g, j, c))],
        out_specs=out_spec,
        out_shape=out_shape,
        scratch_shapes=[pltpu.VMEM((nb * A_HEADS * SLAB_PITCH, tc), F32)],
        compiler_params=_cparams("parallel", "parallel", "parallel"),
        name="to_chain_v" if value_rows else "to_chain_k",
    )(x3)


def _from_chain_kernel(*refs, vs):
    yf_ref, yb_ref, o_ref, scr = refs[-4:]
    nb, tc = o_ref.shape[0], o_ref.shape[1]
    vq = A_HEAD // vs
    rows = nb * A_HEADS
    for v in range(vq):
        yv = (yf_ref[:, v, :] + yb_ref[:, v, :]).T
        for vh in range(vs):
            scr[pl.ds(vh * vq + v, rows, stride=SLAB_PITCH), :] = yv[vh * rows:(vh + 1) * rows]
    for b in range(nb):
        slabs = [scr[pl.ds((b * A_HEADS + h) * SLAB_PITCH, A_HEAD), :] for h in range(A_HEADS)]
        o_ref[b] = jnp.concatenate(slabs, axis=0).T


def _from_chain(y_fwd, y_bwd, prev_rows, n_tok, seq_len, batch0, n_batch):
    nb, vs = _lane_plan(n_batch)
    groups = n_batch // nb
    tc = LANES
    vq = A_HEAD // vs
    in_specs = [pl.BlockSpec((tc, vq, LANES), lambda g, j: (j, 0, g))] * 2
    args = [y_fwd, y_bwd]
    aliases = {}
    if prev_rows is not None:
        in_specs = [pl.BlockSpec(memory_space=pl.ANY)] + in_specs
        args = [prev_rows.reshape(n_tok // seq_len, seq_len, D_MODEL)] + args
        aliases = {0: 0}
    out = pl.pallas_call(
        functools.partial(_from_chain_kernel, vs=vs),
        grid=(groups, seq_len // tc),
        in_specs=in_specs,
        out_specs=pl.BlockSpec((nb, tc, D_MODEL), lambda g, j: (batch0 // nb + g, j, 0)),
        out_shape=jax.ShapeDtypeStruct((n_tok // seq_len, seq_len, D_MODEL), F32),
        scratch_shapes=[pltpu.VMEM((nb * A_HEADS * SLAB_PITCH, tc), F32)],
        input_output_aliases=aliases,
        compiler_params=_cparams("parallel", "parallel"),
        name="from_chain",
    )(*args)
    return out.reshape(n_tok, D_MODEL)


def _state_to_chain(s, vs):
    B = s.shape[0]
    vq = A_HEAD // vs
    s = jnp.transpose(s, (1, 4, 3, 0, 2)).reshape(2, A_HEAD, vs, vq, B * A_HEADS)
    return jnp.transpose(s, (0, 1, 3, 2, 4)).reshape(2, A_HEAD, vq, vs * B * A_HEADS)


def _state_from_chain(s_c, n_batch, vs):
    vq = A_HEAD // vs
    s = s_c.reshape(2, A_HEAD, vq, vs, n_batch, A_HEADS)
    return jnp.transpose(s, (4, 0, 5, 3, 2, 1)).reshape(n_batch, 2, A_HEADS, A_HEAD, A_HEAD)


def _pool_tail(d2, g, pw_ref, pb_ref, ps_ref):
    sl = slice(g * B_GROUP, (g + 1) * B_GROUP)
    y = _dot(d2.astype(BF16), pw_ref[g]) + pb_ref[:, sl]
    return y * ps_ref[:, sl]


def _window_count(t, w, n):
    lo = jnp.clip(t - w // 2, 0, n)
    hi = jnp.clip(t - w // 2 + w, 0, n)
    return (hi - lo).astype(F32)


def _pool_seq_kernel(z_ref, pw_ref, pb_ref, ps_ref, o_ref):
    n = z_ref.shape[0]
    pad = jnp.zeros((POOL_PAD, B_GROUP), F32)
    t = lax.broadcasted_iota(jnp.int32, (n, B_GROUP), 0)
    for g, w in enumerate(POOL_WINDOWS):
        sl = slice(g * B_GROUP, (g + 1) * B_GROUP)
        z = z_ref[:, sl]
        zp = jnp.concatenate([pad, z, pad], axis=0)
        acc = None
        for j in range(w):
            off = j - w // 2
            term = zp if off == 0 else pltpu.roll(zp, (-off) % (n + 2 * POOL_PAD), 0)
            acc = term if acc is None else acc + term
        d = acc[POOL_PAD:POOL_PAD + n] / _window_count(t, w, n) - z
        o_ref[:, sl] = _pool_tail(d, g, pw_ref, pb_ref, ps_ref)


def _pool_grid_kernel(z_ref, pw_ref, pb_ref, ps_ref, _, o_ref):
    rows, cb = z_ref.shape[0], z_ref.shape[1]
    pad = jnp.zeros((POOL_PAD, cb, B_GROUP), F32)
    t = lax.broadcasted_iota(jnp.int32, (rows, cb, B_GROUP), 0)
    for g, w in enumerate(POOL_WINDOWS):
        sl = slice(g * B_GROUP, (g + 1) * B_GROUP)
        z = z_ref[:, :, sl]
        zp = jnp.concatenate([pad, z, pad], axis=0)
        acc = None
        for j in range(w):
            s = POOL_PAD + j - w // 2
            term = zp[s:s + rows]
            acc = term if acc is None else acc + term
        d = acc / _window_count(t, w, rows) - z
        y = _pool_tail(d.reshape(rows * cb, B_GROUP), g, pw_ref, pb_ref, ps_ref)
        o_ref[:, :, sl] = y.reshape(rows, cb, B_GROUP)


def _pool_mix(pb, n_ctx, seq, dec_seq, pw, pbias, pscale):
    n_tok = pb.shape[0]
    wspecs = lambda nd: [pl.BlockSpec((4, B_GROUP, B_GROUP), lambda *a: (0, 0, 0)),
                         pl.BlockSpec((1, B_WIDTH), lambda *a: (0, 0)),
                         pl.BlockSpec((1, B_WIDTH), lambda *a: (0, 0))]
    y_ctx = pl.pallas_call(
        _pool_seq_kernel,
        grid=(n_ctx // seq,),
        in_specs=[pl.BlockSpec((seq, B_WIDTH), lambda i: (i, 0))] + wspecs(1),
        out_specs=pl.BlockSpec((seq, B_WIDTH), lambda i: (i, 0)),
        out_shape=jax.ShapeDtypeStruct((n_tok, B_WIDTH), F32),
        compiler_params=_cparams("parallel"),
        name="pool_seq",
    )(pb, pw, pbias, pscale)
    rows = dec_seq // GRID_W
    n_dec = (n_tok - n_ctx) // dec_seq
    ctx_blocks = n_ctx // dec_seq
    cb = 8
    shape4 = (n_tok // dec_seq, rows, GRID_W, B_WIDTH)
    blk = pl.BlockSpec((None, rows, cb, B_WIDTH), lambda b, c: (b + ctx_blocks, 0, c, 0))
    y_all = pl.pallas_call(
        _pool_grid_kernel,
        grid=(n_dec, GRID_W // cb),
        in_specs=[blk] + wspecs(2) + [pl.BlockSpec(memory_space=pl.ANY)],
        out_specs=blk,
        out_shape=jax.ShapeDtypeStruct(shape4, F32),
        input_output_aliases={4: 0},
        compiler_params=_cparams("parallel", "parallel"),
        name="pool_grid",
    )(pb.reshape(shape4), pw, pbias, pscale, y_ctx.reshape(shape4))
    return y_all.reshape(n_tok, B_WIDTH)


def _conv_kernel(pc_ref, cw_ref, cb_ref, lg_ref, lb_ref, o_ref, *, n_ctx_tiles):
    tm = pc_ref.shape[0]
    u = pc_ref[:, 0:C_WIDTH] * _sigmoid(pc_ref[:, C_WIDTH:2 * C_WIDTH])
    pad = jnp.zeros((CONV_PAD, C_WIDTH), F32)
    up = jnp.concatenate([pad, u, pad], axis=0)
    is_ctx = pl.program_id(0) < n_ctx_tiles
    t = lax.broadcasted_iota(jnp.int32, (tm, C_WIDTH), 0)
    pos = jnp.where(is_ctx, t, t & (GRID_W - 1))
    seg = jnp.where(is_ctx, tm, GRID_W)
    acc = jnp.zeros((tm, C_WIDTH), F32)
    for j in range(C_CONV):
        off = j - C_CONV // 2
        rolled = up if off == 0 else pltpu.roll(up, (-off) % (tm + 2 * CONV_PAD), 0)
        tap = rolled[CONV_PAD:CONV_PAD + tm]
        if off > 0:
            tap = jnp.where(pos < seg - off, tap, 0.0)
        elif off < 0:
            tap = jnp.where(pos >= -off, tap, 0.0)
        acc = acc + tap * cw_ref[pl.ds(j, 1), :]
    c = acc + cb_ref[...]
    mu = jnp.mean(c, axis=-1, keepdims=True)
    cc = c - mu
    var = jnp.mean(cc * cc, axis=-1, keepdims=True)
    y = cc * lax.rsqrt(var + LN_EPS) * lg_ref[...] + lb_ref[...]
    o_ref[...] = y * _sigmoid(y)


def _conv_mix(pc, tm, n_ctx_tiles, cw, cb, lg, lb):
    n_tok = pc.shape[0]
    row = pl.BlockSpec((1, C_WIDTH), lambda i: (0, 0))
    return pl.pallas_call(
        functools.partial(_conv_kernel, n_ctx_tiles=n_ctx_tiles),
        grid=(n_tok // tm,),
        in_specs=[pl.BlockSpec((tm, 2 * C_WIDTH), lambda i: (i, 0)),
                  pl.BlockSpec((C_CONV, C_WIDTH), lambda i: (0, 0)), row, row, row],
        out_specs=pl.BlockSpec((tm, C_WIDTH), lambda i: (i, 0)),
        out_shape=jax.ShapeDtypeStruct((n_tok, C_WIDTH), F32),
        compiler_params=_cparams("parallel"),
        name="conv_mix",
    )(pc, cw, cb, lg, lb)


def _merge_kernel(y_ref, bv_ref, g_ref, yb_ref, yc_ref, pg_ref, x_ref, mod_ref, gng_ref, gnb_ref,
                  n2g_ref, wa_ref, wb_ref, wc_ref, wo_ref, ones_ref, xo_ref, h2_ref):
    d = D_MODEL
    ones_bd = ones_ref[...]
    y = y_ref[...]
    mean = _segsum(y, ones_bd) * (1.0 / A_HEAD)
    yc = y - mean
    var = _segsum(yc * yc, ones_bd) * (1.0 / A_HEAD)
    yn = yc * lax.rsqrt(var + GN_EPS) * gng_ref[...] + gnb_ref[...]
    ya = (yn + bv_ref[...]) * g_ref[...]
    merged = (_sigmoid(pg_ref[:, 0:d]) * _dot(ya.astype(BF16), wa_ref[...])
              + _sigmoid(pg_ref[:, d:2 * d]) * _dot(yb_ref[...].astype(BF16), wb_ref[...])
              + _sigmoid(pg_ref[:, 2 * d:3 * d]) * _dot(yc_ref[...].astype(BF16), wc_ref[...]))
    gt1 = mod_ref[0, :, 2 * d:3 * d]
    xn = x_ref[...] + gt1 * _dot(merged.astype(BF16), wo_ref[...])
    xo_ref[...] = xn
    ms = jnp.mean(xn * xn, axis=-1, keepdims=True)
    h2 = xn * lax.rsqrt(ms + EPS) * n2g_ref[...]
    sh2 = mod_ref[0, :, 3 * d:4 * d]
    sc2 = mod_ref[0, :, 4 * d:5 * d]
    h2_ref[...] = (h2 * (1.0 + sc2) + sh2).astype(BF16)


def _merge(y, bv, g, yb, yc, pg, x, mod3, mod_map, tm, gng, gnb, n2g, wa, wb, wc, wo, ones_bd):
    n_tok = x.shape[0]
    d = D_MODEL
    tile = lambda n: pl.BlockSpec((tm, n), lambda i: (i, 0))
    row = pl.BlockSpec((1, d), lambda i: (0, 0))
    full = lambda a: pl.BlockSpec((a, d), lambda i: (0, 0))
    return pl.pallas_call(
        _merge_kernel,
        grid=(n_tok // tm,),
        in_specs=[tile(d), tile(d), tile(d), tile(B_WIDTH), tile(C_WIDTH), tile(3 * d), tile(d),
                  pl.BlockSpec((1, 1, 6 * d), mod_map), row, row, row,
                  full(d), full(B_WIDTH), full(C_WIDTH), full(d), full(d)],
        out_specs=[tile(d), tile(d)],
        out_shape=[jax.ShapeDtypeStruct((n_tok, d), F32), jax.ShapeDtypeStruct((n_tok, d), BF16)],
        compiler_params=_cparams("parallel"),
        name="merge_outproj",
    )(y, bv, g, yb, yc, pg, x, mod3, gng, gnb, n2g, wa, wb, wc, wo, ones_bd)


NO_RANK = 1e9
RANK_BASE = -2.0 ** 100


def _peer_score_kernel(h_ref, wq_ref, keys_ref, rk0_o, e0_o, th1_o, e1_o, s_scr, top_scr, rank_scr):
    tmp = h_ref.shape[0]
    q_t = lax.dot_general(wq_ref[...], h_ref[...], (((1,), (1,)), ((), ())),
                          preferred_element_type=F32).astype(BF16)
    for hp in range(2 * P_HEADS):
        s_scr[hp] = _dot(keys_ref[hp], q_t[hp * P_HALF:(hp + 1) * P_HALF, :])

    neg = jnp.float32(-jnp.inf)
    half = P_TOPK // 2
    row16 = lax.broadcasted_iota(jnp.int32, (P_TOPK, LANES), 0)
    row8 = lax.broadcasted_iota(jnp.int32, (half, LANES), 0)
    n_groups = tmp // LANES

    def top16_pair(sets):
        vals = [jnp.full((P_TOPK, LANES), neg, F32) for _ in sets]
        sets = list(sets)
        for it in range(P_TOPK):
            for n in range(len(sets)):
                m = jnp.max(sets[n], axis=0, keepdims=True)
                vals[n] = jnp.where(row16 == it, m, vals[n])
                sets[n] = jnp.where(sets[n] >= m, RANK_BASE * (1.0 + it / P_TOPK), sets[n])
        ranks = [jnp.where(s <= RANK_BASE, s * (P_TOPK / RANK_BASE) - P_TOPK, NO_RANK) for s in sets]
        return vals, ranks

    n_items = P_HEADS * n_groups

    def item(idx):
        return idx // n_groups, pl.ds(pl.multiple_of((idx % n_groups) * LANES, LANES), LANES)

    def extract(idx):
        h, ls = item(idx)
        (a16, b16), (rank0, rank1) = top16_pair((s_scr[2 * h, :, ls], s_scr[2 * h + 1, :, ls]))
        rk0_o[h, :, ls] = rank0
        top_scr[0] = a16
        top_scr[1] = b16
        rank_scr[...] = rank1

    def combine(idx):
        h, ls = item(idx)
        a16 = top_scr[0]
        b16 = top_scr[1]
        rank1 = rank_scr[...]
        s0 = s_scr[2 * h, :, ls]
        s1 = s_scr[2 * h + 1, :, ls]
        b_lo = b16[0:half, :]
        cands = [a16[0:1, :] + b16, a16[1:2, :] + b_lo]
        for r in range(2, half):
            cands.append(jnp.where(row8 < P_TOPK // (r + 1), a16[r:r + 1, :] + b_lo, neg))
        cands.append(a16[half:P_TOPK, :] + b16[0:1, :])
        cmax = a16[0:1, :] + b16[0:1, :]
        work = list(cands)
        tau = cmax
        for it in range(P_TOPK):
            m = jnp.maximum(work[0][0:half, :], work[0][half:P_TOPK, :])
            for c in work[1:]:
                m = jnp.maximum(m, c)
            tau = jnp.max(m, axis=0, keepdims=True)
            if it + 1 < P_TOPK:
                work = [jnp.where(c >= tau, neg, c) for c in work]
        z = jnp.zeros((1, LANES), F32)
        sels = []
        for c in cands:
            sel = c >= tau
            z = z + jnp.sum(jnp.where(sel, jnp.exp(c - cmax), 0.0), axis=0, keepdims=True)
            sels.append(jnp.where(sel, 1.0, 0.0))
        cnt_hi = sels[0][half:P_TOPK, :]
        cnt_lo = sels[0][0:half, :]
        for c in sels[1:half]:
            cnt_lo = cnt_lo + c
        cnt_lo = cnt_lo + jnp.where(row8 == 0, jnp.sum(sels[half], axis=0, keepdims=True), 0.0)
        th1 = jnp.zeros(rank1.shape, F32)
        for q in range(P_TOPK):
            cnt = cnt_lo[q:q + 1, :] if q < half else cnt_hi[q - half:q - half + 1, :]
            th1 = jnp.where(rank1 == float(q), cnt, th1)
        th1_o[h, :, ls] = th1.astype(BF16)
        e0_o[h, :, ls] = jnp.exp(s0 - a16[0:1, :]) / z
        e1_o[h, :, ls] = jnp.exp(s1 - b16[0:1, :]).astype(BF16)

    top_scr[...] = jnp.zeros_like(top_scr)
    rank_scr[...] = jnp.zeros_like(rank_scr)

    def body(n, carry):
        combine(jnp.maximum(n - 1, 0))
        extract(jnp.minimum(n, n_items - 1))
        return carry

    lax.fori_loop(0, n_items + 1, body, 0)


def _peer_scores(h2, wq_t, keys):
    n_tok = h2.shape[0]
    tmp = TM_PEER
    big = pl.BlockSpec((P_HEADS, P_KEYS, tmp), lambda i: (0, 0, i))
    big_shape = jax.ShapeDtypeStruct((P_HEADS, P_KEYS, n_tok), F32)
    narrow_shape = jax.ShapeDtypeStruct((P_HEADS, P_KEYS, n_tok), BF16)
    return pl.pallas_call(
        _peer_score_kernel,
        grid=(n_tok // tmp,),
        in_specs=[pl.BlockSpec((tmp, D_MODEL), lambda i: (i, 0)),
                  pl.BlockSpec((2 * P_HEADS * P_HALF, D_MODEL), lambda i: (0, 0)),
                  pl.BlockSpec((2 * P_HEADS, P_KEYS, P_HALF), lambda i: (0, 0, 0))],
        out_specs=[big, big, big, big],
        out_shape=[big_shape, big_shape, narrow_shape, narrow_shape],
        scratch_shapes=[pltpu.VMEM((2 * P_HEADS, P_KEYS, tmp), F32),
                        pltpu.VMEM((2, P_TOPK, LANES), F32), pltpu.VMEM((P_KEYS, LANES), F32)],
        compiler_params=_cparams("parallel"),
        name="peer_scores",
    )(h2, wq_t, keys)


def _peer_expert_kernel(h_ref, u_ref, vt_ref, rk0_ref, e0_ref, th1_ref, e1_ref, x_ref,
                        mod_ref, *rest, final_norm):
    if final_norm:
        fg_ref, o_ref, acc_ref, gated_ref = rest
    else:
        o_ref, acc_ref, gated_ref = rest
    e = pl.program_id(1)
    tmp = h_ref.shape[0]

    @pl.when(e == 0)
    def _():
        acc_ref[...] = jnp.zeros_like(acc_ref)

    for il in range(PEER_EB // P_KEYS):
        rows = slice(il * P_KEYS, (il + 1) * P_KEYS)
        for lc in range(tmp // LANES):
            ls = slice(lc * LANES, (lc + 1) * LANES)
            w = None
            for h in range(P_HEADS):
                rk0 = jnp.broadcast_to(rk0_ref[h, pl.ds(il, 1), ls], (P_KEYS, LANES)).astype(BF16)
                e0 = jnp.broadcast_to(e0_ref[h, pl.ds(il, 1), ls], (P_KEYS, LANES)).astype(BF16)
                term = jnp.where(rk0 < th1_ref[h, :, ls], e0 * e1_ref[h, :, ls], jnp.zeros((), BF16))
                w = term if w is None else w + term
            gated_ref[rows, ls] = w
    hid = lax.dot_general(u_ref[...], h_ref[...], (((1,), (1,)), ((), ())),
                          preferred_element_type=F32)
    act = 0.5 * hid * (1.0 + lax.erf(hid * (2.0 ** -0.5)))
    acc_ref[...] += _dot(vt_ref[...], gated_ref[...] * act.astype(BF16))

    @pl.when(e == pl.num_programs(1) - 1)
    def _():
        gt2 = mod_ref[0, :, 5 * D_MODEL:6 * D_MODEL]
        xn = x_ref[...] + gt2 * acc_ref[...].T
        if final_norm:
            ms = jnp.mean(xn * xn, axis=-1, keepdims=True)
            xn = xn * lax.rsqrt(ms + EPS) * fg_ref[...]
        o_ref[...] = xn


def _peer_experts(h2, u_bf, vt_bf, rk0, e0, th1, e1, x, mod3, mod_map, final_g=None):
    n_tok = h2.shape[0]
    tmp = TM_PEER
    n_i = PEER_EB // P_KEYS
    sel = pl.BlockSpec((P_HEADS, n_i, tmp), lambda i, e: (0, e, i))
    full = pl.BlockSpec((P_HEADS, P_KEYS, tmp), lambda i, e: (0, 0, i))
    mod_map2 = lambda i, e: mod_map(i)
    tile = pl.BlockSpec((tmp, D_MODEL), lambda i, e: (i, 0))
    out = jax.ShapeDtypeStruct((n_tok, D_MODEL), F32)
    final_norm = final_g is not None
    extra_in = [pl.BlockSpec((1, D_MODEL), lambda i, e: (0, 0))] if final_norm else []
    extra_args = [final_g] if final_norm else []
    return pl.pallas_call(
        functools.partial(_peer_expert_kernel, final_norm=final_norm),
        grid=(n_tok // tmp, P_EXPERTS // PEER_EB),
        in_specs=[tile,
                  pl.BlockSpec((PEER_EB, D_MODEL), lambda i, e: (e, 0)),
                  pl.BlockSpec((D_MODEL, PEER_EB), lambda i, e: (0, e)),
                  sel, sel, full, full, tile,
                  pl.BlockSpec((1, 1, 6 * D_MODEL), mod_map2)] + extra_in,
        out_specs=tile,
        out_shape=out,
        scratch_shapes=[pltpu.VMEM((D_MODEL, tmp), F32), pltpu.VMEM((PEER_EB, tmp), BF16)],
        compiler_params=_cparams("parallel", "arbitrary"),
        name="peer_experts",
    )(h2, u_bf, vt_bf, rk0, e0, th1, e1, x, mod3, *extra_args)


def _block_diag2(w):
    z = jnp.zeros_like(w[0])
    return jnp.concatenate([jnp.concatenate([w[0], z], axis=1),
                            jnp.concatenate([z, w[1]], axis=1)], axis=0)


def kernel(x_prompt, x_sample, state_rwkv, c, c_ctx, ada_w, ada_b, norm1_g, norm2_g, w_in, shift_mu,
           decay_w0, decay_w2, iclr_a0, iclr_a2, gate_g2, k_k, k_a, r_k, gn_g, gn_b, w_a_out,
           pool_w, pool_b, pool_scale, w_b_out, conv_w, conv_b, cln_g, cln_b, w_c_out, w_out,
           peer_wq, peer_keys, peer_u, peer_v, final_g):
    d = D_MODEL
    batch, seq, _ = x_prompt.shape
    dec_batch, dec_seq, _ = x_sample.shape
    depth = ada_w.shape[0]
    n_ctx = batch * seq
    n_dec = dec_batch * dec_seq
    tm = seq
    assert n_ctx % dec_seq == 0 and dec_seq % tm == 0 and dec_seq % GRID_W == 0
    assert n_ctx % TM_PROJ == 0 and dec_seq % TM_PROJ == 0 and dec_seq % TM_PEER == 0
    assert n_ctx % TM_PEER == 0 and tm % GRID_W == 0 and dec_batch <= 7 and TM_PROJ % tm == 0

    x = jnp.concatenate([x_prompt.reshape(n_ctx, d), x_sample.reshape(n_dec, d)], axis=0)
    n_tok = n_ctx + n_dec

    cond8 = jnp.zeros((8, d), F32).at[0].set(c_ctx).at[1:1 + dec_batch].set(c)
    mod_all = _modulation(cond8, ada_w, ada_b)

    seg_ids = jnp.arange(d) // A_HEAD
    ones_bd = (seg_ids[:, None] == seg_ids[None, :]).astype(BF16)

    map_proj = _mod_row_map(n_ctx // TM_PROJ, dec_seq // TM_PROJ)
    map_tm = _mod_row_map(n_ctx // tm, dec_seq // tm)
    map_peer = _mod_row_map(n_ctx // TM_PEER, dec_seq // TM_PEER)

    n_tiles = n_tok // tm
    tile_idx = jnp.arange(n_tiles)
    dec_tile = (tile_idx - n_ctx // tm) % (dec_seq // tm)
    is_start = jnp.where(tile_idx < n_ctx // tm, True, dec_tile == 0)[:, None]
    is_end = jnp.where(tile_idx < n_ctx // tm, True, dec_tile == dec_seq // tm - 1)[:, None]

    ctx_states = []
    for l in range(depth):
        mod3 = mod_all[l][:, None, :]
        w_in_bf = w_in[l].astype(BF16)
        pa, edges = _inproj(x, mod3, norm1_g[l][None], w_in_bf[:, :A_COLS], (A_COLS,), map_proj,
                            "inproj_a", edge_tile=tm)
        pb, pc, pg = _inproj(x, mod3, norm1_g[l][None], w_in_bf[:, A_COLS:],
                             (B_WIDTH, 2 * C_WIDTH, 3 * d), map_proj, "inproj_bcg")

        zero_row = jnp.zeros((1, A_COLS), F32)
        halo_prev = jnp.concatenate([zero_row, edges[:-1, 1, :]], axis=0)
        halo_next = jnp.concatenate([edges[1:, 0, :], zero_row], axis=0)
        halo_prev = jnp.where(is_start, 0.0, halo_prev)[:, None, :]
        halo_next = jnp.where(is_end, 0.0, halo_next)[:, None, :]
        prep_params = (shift_mu[l][None], decay_w0[l].reshape(1, 2 * d),
                       _block_diag2(decay_w2[l]).astype(BF16), iclr_a0[l].reshape(1, 2 * d),
                       _block_diag2(iclr_a2[l]).astype(BF16), gate_g2[l].astype(BF16),
                       k_k[l][None], k_a[l][None], r_k[l].reshape(1, d), ones_bd)
        r, v, kk, dec, kd, bb, bv, g = _rwkv_prep(pa, halo_prev, halo_next, tm, prep_params)

        y_scan = None
        for seq_len, batch0, nb, s0 in ((seq, 0, batch, None),
                                        (dec_seq, n_ctx // dec_seq, dec_batch, state_rwkv[:, l])):
            vs = _lane_plan(nb)[1]
            if s0 is None:
                s0_c = jnp.zeros((2, A_HEAD, A_HEAD // vs, vs * nb * A_HEADS), F32)
            else:
                s0_c = _state_to_chain(s0, vs)
            lay = lambda a, value_rows=False: _to_chain(a, seq_len, batch0, nb, value_rows)
            y_fwd, y_bwd, sfin_c = _rwkv_scan(lay(kk)[0], lay(r)[0], lay(dec), lay(bb), lay(kd),
                                              lay(v, True)[0], s0_c)
            y_scan = _from_chain(y_fwd, y_bwd, y_scan, n_tok, seq_len, batch0, nb)
            if s0 is None:
                ctx_states.append(_state_from_chain(sfin_c, nb, vs))

        yb = _pool_mix(pb, n_ctx, seq, dec_seq, pool_w[l].astype(BF16), pool_b[l].reshape(1, B_WIDTH),
                       pool_scale[l][None])
        yc = _conv_mix(pc, tm, n_ctx // tm, conv_w[l], conv_b[l][None], cln_g[l][None], cln_b[l][None])

        x, h2 = _merge(y_scan, bv, g, yb, yc, pg, x, mod3, map_tm, tm, gn_g[l][None], gn_b[l][None],
                       norm2_g[l][None], w_a_out[l].astype(BF16), w_b_out[l].astype(BF16),
                       w_c_out[l].astype(BF16), w_out[l].astype(BF16), ones_bd)

        wq_t = peer_wq[l].T.astype(BF16)
        keys = peer_keys[l].reshape(2 * P_HEADS, P_KEYS, P_HALF).astype(BF16)
        rk0, e0p, th1, e1p = _peer_scores(h2, wq_t, keys)
        x = _peer_experts(h2, peer_u[l].astype(BF16), peer_v[l].T.astype(BF16), rk0, e0p, th1, e1p,
                          x, mod3, map_peer, final_g[None] if l == depth - 1 else None)

    y = x
    new_state = jnp.stack(ctx_states, axis=1)
    return (y[:n_ctx].reshape(batch, seq, d), y[n_ctx:].reshape(dec_batch, dec_seq, d), new_state)
```

```python
import functools

import jax
import jax.numpy as jnp
from jax import lax
from jax.experimental import pallas as pl
from jax.experimental.pallas import tpu as pltpu

F32 = jnp.float32
BF16 = jnp.bfloat16

D_MODEL = 1024
A_HEADS = 16
A_HEAD = 64
A_LORA = 384
A_COLS = 3 * D_MODEL + A_LORA
B_WIDTH = 512
B_GROUP = 128
POOL_WINDOWS = (2, 4, 8, 16)
POOL_PAD = 16
C_WIDTH = 512
C_CONV = 31
CONV_PAD = 16
GRID_W = 64
P_HEADS = 8
P_KEYS = 128
P_TOPK = 16
P_HALF = 128
P_EXPERTS = P_KEYS * P_KEYS
EPS = 1e-6
GN_EPS = 64e-5
LN_EPS = 1e-5

LANES = 128
VMEM_LIMIT = 56 * 1024 * 1024

TM_PROJ = 512
TM_PEER = 512
PEER_EB = 1024
SCAN_TB = 8


def _cparams(*sem):
    return pltpu.CompilerParams(dimension_semantics=sem, vmem_limit_bytes=VMEM_LIMIT)


def _sigmoid(x):
    return 1.0 / (1.0 + jnp.exp(-x))


def _dot(a, b):
    return jnp.dot(a, b, preferred_element_type=F32)


def _segsum(x, ones_bd):
    hi = x.astype(BF16)
    lo = (x - hi.astype(F32)).astype(BF16)
    return _dot(hi, ones_bd) + _dot(lo, ones_bd)


def _mod_kernel(c_ref, w_ref, b_ref, o_ref):
    c = c_ref[...]
    s = c * _sigmoid(c)
    o_ref[...] = jnp.dot(s, w_ref[...], preferred_element_type=F32,
                         precision=lax.Precision.HIGHEST) + b_ref[...]


def _modulation(cond8, ada_w, ada_b):
    L = ada_w.shape[0]
    nb = 6
    return pl.pallas_call(
        _mod_kernel,
        grid=(L, nb),
        in_specs=[pl.BlockSpec((8, D_MODEL), lambda l, j: (0, 0)),
                  pl.BlockSpec((None, D_MODEL, D_MODEL), lambda l, j: (l, 0, j)),
                  pl.BlockSpec((None, 1, D_MODEL), lambda l, j: (l, 0, j))],
        out_specs=pl.BlockSpec((None, 8, D_MODEL), lambda l, j: (l, 0, j)),
        out_shape=jax.ShapeDtypeStruct((L, 8, 6 * D_MODEL), F32),
        compiler_params=_cparams("parallel", "parallel"),
        name="adaln_mod",
    )(cond8, ada_w, ada_b.reshape(L, 1, 6 * D_MODEL))


def _mod_row_map(n_ctx_tiles, tiles_per_dec_seq):
    def index_map(i):
        row = jnp.where(i < n_ctx_tiles, 0, 1 + (i - n_ctx_tiles) // tiles_per_dec_seq)
        return (row, 0, 0)
    return index_map


def _inproj_kernel(x_ref, mod_ref, g_ref, w_ref, *o_refs, edge_tile):
    x = x_ref[...]
    ms = jnp.mean(x * x, axis=-1, keepdims=True)
    y = x * lax.rsqrt(ms + EPS) * g_ref[...]
    sh = mod_ref[0, :, 0:D_MODEL]
    sc = mod_ref[0, :, D_MODEL:2 * D_MODEL]
    h = (y * (1.0 + sc) + sh).astype(BF16)
    if edge_tile:
        *o_refs, edge_ref = o_refs
    off = 0
    for o_ref in o_refs:
        n = o_ref.shape[1]
        o_ref[...] = _dot(h, w_ref[:, off:off + n])
        off += n
    if edge_tile:
        first = o_refs[0]
        for t in range(x.shape[0] // edge_tile):
            edge_ref[t, 0:1, :] = first[t * edge_tile:t * edge_tile + 1, :]
            edge_ref[t, 1:2, :] = first[(t + 1) * edge_tile - 1:(t + 1) * edge_tile, :]


def _inproj(x, mod3, g, w, splits, mod_map, name, edge_tile=0):
    n_tok = x.shape[0]
    n_out = w.shape[1]
    out_specs = [pl.BlockSpec((TM_PROJ, n), lambda i: (i, 0)) for n in splits]
    out_shape = [jax.ShapeDtypeStruct((n_tok, n), F32) for n in splits]
    if edge_tile:
        per = TM_PROJ // edge_tile
        out_specs.append(pl.BlockSpec((per, 2, splits[0]), lambda i: (i, 0, 0)))
        out_shape.append(jax.ShapeDtypeStruct((n_tok // edge_tile, 2, splits[0]), F32))
    return pl.pallas_call(
        functools.partial(_inproj_kernel, edge_tile=edge_tile),
        grid=(n_tok // TM_PROJ,),
        in_specs=[pl.BlockSpec((TM_PROJ, D_MODEL), lambda i: (i, 0)),
                  pl.BlockSpec((1, 1, 6 * D_MODEL), mod_map),
                  pl.BlockSpec((1, D_MODEL), lambda i: (0, 0)),
                  pl.BlockSpec((D_MODEL, n_out), lambda i: (0, 0))],
        out_specs=out_specs,
        out_shape=out_shape,
        compiler_params=_cparams("parallel"),
        name=name,
    )(x, mod3, g, w)


def _prep_kernel(pa_ref, hp_ref, hn_ref, mu_ref, w0_ref, w2_ref, a0_ref, a2_ref, g2_ref,
                 kkw_ref, ka_ref, rk_ref, ones_ref,
                 r_o, v_o, kk_o, dec_o, kd_o, bb_o, bv_o, g_o):
    tm = pa_ref.shape[0]

    def shifted(c0, c1):
        x = pa_ref[:, c0:c1]
        row = lax.broadcasted_iota(jnp.int32, x.shape, 0)
        prev = jnp.where(row == 0, hp_ref[0, :, c0:c1], pltpu.roll(x, 1, 0))
        nxt = jnp.where(row == tm - 1, hn_ref[0, :, c0:c1], pltpu.roll(x, tm - 1, 0))
        return x + (0.5 * (prev + nxt) - x) * mu_ref[:, c0:c1]

    d = D_MODEL
    r = shifted(0, d)
    k = shifted(d, 2 * d)
    v = shifted(2 * d, 3 * d)
    lo = shifted(3 * d, 3 * d + A_LORA)
    w_lo = lo[:, 0:128]
    a_lo = lo[:, 128:256]
    g_lo = lo[:, 256:384]
    ones_bd = ones_ref[...]

    r_o[...] = r
    v_o[...] = v

    kkraw = k * kkw_ref[...]
    nrm = jnp.sqrt(_segsum(kkraw * kkraw, ones_bd))
    kk = kkraw / jnp.maximum(nrm, 1e-12)
    kk_o[...] = kk

    z = -(w0_ref[...] + _dot(jnp.tanh(w_lo).astype(BF16), w2_ref[...]))
    softplus = jnp.maximum(z, 0.0) + jnp.log(1.0 + jnp.exp(-jnp.abs(z)))
    dec_o[...] = jnp.exp(-jnp.exp(-softplus - 0.5))

    a = _sigmoid(a0_ref[...] + _dot(a_lo.astype(BF16), a2_ref[...]))
    ka = ka_ref[...]
    kd0 = k * (1.0 + (a[:, 0:d] - 1.0) * ka)
    kd1 = k * (1.0 + (a[:, d:2 * d] - 1.0) * ka)
    kd_o[:, 0:d] = kd0
    kd_o[:, d:2 * d] = kd1
    bb_o[:, 0:d] = kk * a[:, 0:d]
    bb_o[:, d:2 * d] = kk * a[:, d:2 * d]

    bonus = _segsum(r * (kd0 + kd1) * rk_ref[...], ones_bd)
    bv_o[...] = bonus * v
    g_o[...] = _dot(_sigmoid(g_lo).astype(BF16), g2_ref[...])


def _rwkv_prep(pa, halo_prev, halo_next, tm, params):
    n_tok = pa.shape[0]
    d = D_MODEL
    row = lambda n: pl.BlockSpec((1, n), lambda i: (0, 0))
    full = lambda a, b: pl.BlockSpec((a, b), lambda i: (0, 0))
    tile = lambda n: pl.BlockSpec((tm, n), lambda i: (i, 0))
    halo = pl.BlockSpec((1, 1, A_COLS), lambda i: (i, 0, 0))
    outs = [d, d, d, 2 * d, 2 * d, 2 * d, d, d]
    return pl.pallas_call(
        _prep_kernel,
        grid=(n_tok // tm,),
        in_specs=[tile(A_COLS), halo, halo, row(A_COLS), row(2 * d), full(128, 2 * d), row(2 * d),
                  full(128, 2 * d), full(128, d), row(d), row(d), row(d), full(d, d)],
        out_specs=[tile(n) for n in outs],
        out_shape=[jax.ShapeDtypeStruct((n_tok, n), F32) for n in outs],
        compiler_params=_cparams("parallel"),
        name="rwkv_prep",
    )(pa, halo_prev, halo_next, *params)


SCAN_ROWS = 16
SCAN_STATE_PAD = 8


def _scan_kernel(kkf, kkb, rf, rb, wf, wb, bf, bb, kdf, kdb, vf, vb, s0_ref,
                 yf_ref, yb_ref, sfin_ref, s_scr):
    nk, tb = kkf.shape[0], kkf.shape[1]
    j = pl.program_id(2)
    live = slice(0, vf.shape[1])

    @pl.when(j == 0)
    def _():
        s_scr[:, :, live, :] = s0_ref[...]

    def advance(d, t, rs, kk_ref, r_ref, w_ref, b_ref, kd_ref, v_ref, y_ref):
        parts = [None] * 4
        for k in range(nk):
            term = s_scr[d, k, live, :] * kk_ref[k, pl.ds(t, 1), :]
            parts[k % 4] = term if parts[k % 4] is None else parts[k % 4] + term
        sa = -((parts[0] + parts[1]) + (parts[2] + parts[3]))
        vt = v_ref[t, rs, :]
        ys = [None] * 2
        for k in range(nk):
            s_new = (s_scr[d, k, live, :] * w_ref[k, pl.ds(t, 1), :] + sa * b_ref[k, pl.ds(t, 1), :]
                     + vt * kd_ref[k, pl.ds(t, 1), :])
            s_scr[d, k, live, :] = s_new
            term = s_new * r_ref[k, pl.ds(t, 1), :]
            ys[k % 2] = term if ys[k % 2] is None else ys[k % 2] + term
        y_ref[t, rs, :] = ys[0] + ys[1]

    rs = slice(None)

    def step(i, carry):
        advance(0, i, rs, kkf, rf, wf, bf, kdf, vf, yf_ref)
        advance(1, tb - 1 - i, rs, kkb, rb, wb, bb, kdb, vb, yb_ref)
        return carry

    lax.fori_loop(0, tb, step, 0)

    @pl.when(j == pl.num_programs(2) - 1)
    def _():
        sfin_ref[...] = s_scr[:, :, live, :]


def _rwkv_scan(kk_c, r_c, w_c, b_c, kd_c, v_c, s0_c):
    nk, T, L = kk_c.shape
    vq = v_c.shape[1]
    tb = min(SCAN_TB, T)
    nt = T // tb
    fwd = lambda j: j
    bwd = lambda j: nt - 1 - j
    rows = min(vq, SCAN_ROWS)
    shared = lambda tj: pl.BlockSpec((nk, tb, LANES), lambda g, q, j: (0, tj(j), g))
    perdir = lambda d, tj: pl.BlockSpec((None, nk, tb, LANES), lambda g, q, j: (d, 0, tj(j), g))
    vspec = lambda tj: pl.BlockSpec((tb, rows, LANES), lambda g, q, j: (tj(j), q, g))
    sspec = pl.BlockSpec((2, nk, rows, LANES), lambda g, q, j: (0, 0, q, g))
    y_shape = jax.ShapeDtypeStruct((T, vq, L), F32)
    return pl.pallas_call(
        _scan_kernel,
        grid=(L // LANES, vq // rows, nt),
        in_specs=[shared(fwd), shared(bwd), shared(fwd), shared(bwd),
                  perdir(0, fwd), perdir(1, bwd), perdir(0, fwd), perdir(1, bwd),
                  perdir(0, fwd), perdir(1, bwd), vspec(fwd), vspec(bwd), sspec],
        out_specs=[vspec(fwd), vspec(bwd), sspec],
        out_shape=[y_shape, y_shape, jax.ShapeDtypeStruct((2, nk, vq, L), F32)],
        scratch_shapes=[pltpu.VMEM((2, nk, rows + SCAN_STATE_PAD, LANES), F32)],
        compiler_params=_cparams("parallel", "parallel", "arbitrary"),
        name="rwkv_scan",
    )(kk_c, kk_c, r_c, r_c, w_c, w_c, b_c, b_c, kd_c, kd_c, v_c, v_c, s0_c)


def _lane_plan(n_batch):
    full = LANES // A_HEADS
    if n_batch >= full:
        assert n_batch % full == 0
        return full, 1
    assert full % n_batch == 0
    return n_batch, full // n_batch


SLAB_PITCH = A_HEAD + 4


def _to_chain_kernel(x_ref, o_ref, scr, *, vs, value_rows):
    nb, tc = x_ref.shape[0], x_ref.shape[1]
    n_slabs = nb * A_HEADS
    for b in range(nb):
        xt = x_ref[b].T
        for h in range(A_HEADS):
            scr[pl.ds((b * A_HEADS + h) * SLAB_PITCH, A_HEAD), :] = xt[h * A_HEAD:(h + 1) * A_HEAD]
    row_of_slabs = lambda c: scr[pl.ds(c, n_slabs, stride=SLAB_PITCH), :]
    vq = A_HEAD // vs
    if value_rows:
        for v in range(vq):
            rows = [row_of_slabs(vh * vq + v) for vh in range(vs)]
            rows = rows[0] if vs == 1 else jnp.concatenate(rows, axis=0)
            o_ref[:, v, :] = rows.T
    else:
        for k in range(A_HEAD):
            rows = row_of_slabs(k)
            if vs > 1:
                rows = jnp.concatenate([rows] * vs, axis=0)
            o_ref[k] = rows.T


def _to_chain(x, seq_len, batch0, n_batch, value_rows):
    n_col = x.shape[1] // D_MODEL
    nb, vs = _lane_plan(n_batch)
    assert batch0 % nb == 0 and seq_len % LANES == 0
    groups = n_batch // nb
    tc = LANES
    x3 = x.reshape(x.shape[0] // seq_len, seq_len, n_col * D_MODEL)
    vq = A_HEAD // vs
    if value_rows:
        out_spec = pl.BlockSpec((None, tc, vq, LANES), lambda c, g, j: (c, j, 0, g))
        out_shape = jax.ShapeDtypeStruct((n_col, seq_len, vq, groups * LANES), F32)
    else:
        out_spec = pl.BlockSpec((None, A_HEAD, tc, LANES), lambda c, g, j: (c, 0, j, g))
        out_shape = jax.ShapeDtypeStruct((n_col, A_HEAD, seq_len, groups * LANES), F32)
    return pl.pallas_call(
        functools.partial(_to_chain_kernel, vs=vs, value_rows=value_rows),
        grid=(n_col, groups, seq_len // tc),
        in_specs=[pl.BlockSpec((nb, tc, D_MODEL), lambda c, g, j: (batch0 // nb + g, j, c))],
        out_specs=out_spec,
        out_shape=out_shape,
        scratch_shapes=[pltpu.VMEM((nb * A_HEADS * SLAB_PITCH, tc), F32)],
        compiler_params=_cparams("parallel", "parallel", "parallel"),
        name="to_chain_v" if value_rows else "to_chain_k",
    )(x3)


def _from_chain_kernel(*refs, vs):
    yf_ref, yb_ref, o_ref, scr = refs[-4:]
    nb, tc = o_ref.shape[0], o_ref.shape[1]
    vq = A_HEAD // vs
    rows = nb * A_HEADS
    for v in range(vq):
        yv = (yf_ref[:, v, :] + yb_ref[:, v, :]).T
        for vh in range(vs):
            scr[pl.ds(vh * vq + v, rows, stride=SLAB_PITCH), :] = yv[vh * rows:(vh + 1) * rows]
    for b in range(nb):
        slabs = [scr[pl.ds((b * A_HEADS + h) * SLAB_PITCH, A_HEAD), :] for h in range(A_HEADS)]
        o_ref[b] = jnp.concatenate(slabs, axis=0).T


def _from_chain(y_fwd, y_bwd, prev_rows, n_tok, seq_len, batch0, n_batch):
    nb, vs = _lane_plan(n_batch)
    groups = n_batch // nb
    tc = LANES
    vq = A_HEAD // vs
    in_specs = [pl.BlockSpec((tc, vq, LANES), lambda g, j: (j, 0, g))] * 2
    args = [y_fwd, y_bwd]
    aliases = {}
    if prev_rows is not None:
        in_specs = [pl.BlockSpec(memory_space=pl.ANY)] + in_specs
        args = [prev_rows.reshape(n_tok // seq_len, seq_len, D_MODEL)] + args
        aliases = {0: 0}
    out = pl.pallas_call(
        functools.partial(_from_chain_kernel, vs=vs),
        grid=(groups, seq_len // tc),
        in_specs=in_specs,
        out_specs=pl.BlockSpec((nb, tc, D_MODEL), lambda g, j: (batch0 // nb + g, j, 0)),
        out_shape=jax.ShapeDtypeStruct((n_tok // seq_len, seq_len, D_MODEL), F32),
        scratch_shapes=[pltpu.VMEM((nb * A_HEADS * SLAB_PITCH, tc), F32)],
        input_output_aliases=aliases,
        compiler_params=_cparams("parallel", "parallel"),
        name="from_chain",
    )(*args)
    return out.reshape(n_tok, D_MODEL)


def _state_to_chain(s, vs):
    B = s.shape[0]
    vq = A_HEAD // vs
    s = jnp.transpose(s, (1, 4, 3, 0, 2)).reshape(2, A_HEAD, vs, vq, B * A_HEADS)
    return jnp.transpose(s, (0, 1, 3, 2, 4)).reshape(2, A_HEAD, vq, vs * B * A_HEADS)


def _state_from_chain(s_c, n_batch, vs):
    vq = A_HEAD // vs
    s = s_c.reshape(2, A_HEAD, vq, vs, n_batch, A_HEADS)
    return jnp.transpose(s, (4, 0, 5, 3, 2, 1)).reshape(n_batch, 2, A_HEADS, A_HEAD, A_HEAD)


def _pool_tail(d2, g, pw_ref, pb_ref, ps_ref):
    sl = slice(g * B_GROUP, (g + 1) * B_GROUP)
    y = _dot(d2.astype(BF16), pw_ref[g]) + pb_ref[:, sl]
    return y * ps_ref[:, sl]


def _window_count(t, w, n):
    lo = jnp.clip(t - w // 2, 0, n)
    hi = jnp.clip(t - w // 2 + w, 0, n)
    return (hi - lo).astype(F32)


def _pool_seq_kernel(z_ref, pw_ref, pb_ref, ps_ref, o_ref):
    n = z_ref.shape[0]
    pad = jnp.zeros((POOL_PAD, B_GROUP), F32)
    t = lax.broadcasted_iota(jnp.int32, (n, B_GROUP), 0)
    for g, w in enumerate(POOL_WINDOWS):
        sl = slice(g * B_GROUP, (g + 1) * B_GROUP)
        z = z_ref[:, sl]
        zp = jnp.concatenate([pad, z, pad], axis=0)
        acc = None
        for j in range(w):
            off = j - w // 2
            term = zp if off == 0 else pltpu.roll(zp, (-off) % (n + 2 * POOL_PAD), 0)
            acc = term if acc is None else acc + term
        d = acc[POOL_PAD:POOL_PAD + n] / _window_count(t, w, n) - z
        o_ref[:, sl] = _pool_tail(d, g, pw_ref, pb_ref, ps_ref)


def _pool_grid_kernel(z_ref, pw_ref, pb_ref, ps_ref, _, o_ref):
    rows, cb = z_ref.shape[0], z_ref.shape[1]
    pad = jnp.zeros((POOL_PAD, cb, B_GROUP), F32)
    t = lax.broadcasted_iota(jnp.int32, (rows, cb, B_GROUP), 0)
    for g, w in enumerate(POOL_WINDOWS):
        sl = slice(g * B_GROUP, (g + 1) * B_GROUP)
        z = z_ref[:, :, sl]
        zp = jnp.concatenate([pad, z, pad], axis=0)
        acc = None
        for j in range(w):
            s = POOL_PAD + j - w // 2
            term = zp[s:s + rows]
            acc = term if acc is None else acc + term
        d = acc / _window_count(t, w, rows) - z
        y = _pool_tail(d.reshape(rows * cb, B_GROUP), g, pw_ref, pb_ref, ps_ref)
        o_ref[:, :, sl] = y.reshape(rows, cb, B_GROUP)


def _pool_mix(pb, n_ctx, seq, dec_seq, pw, pbias, pscale):
    n_tok = pb.shape[0]
    wspecs = lambda nd: [pl.BlockSpec((4, B_GROUP, B_GROUP), lambda *a: (0, 0, 0)),
                         pl.BlockSpec((1, B_WIDTH), lambda *a: (0, 0)),
                         pl.BlockSpec((1, B_WIDTH), lambda *a: (0, 0))]
    y_ctx = pl.pallas_call(
        _pool_seq_kernel,
        grid=(n_ctx // seq,),
        in_specs=[pl.BlockSpec((seq, B_WIDTH), lambda i: (i, 0))] + wspecs(1),
        out_specs=pl.BlockSpec((seq, B_WIDTH), lambda i: (i, 0)),
        out_shape=jax.ShapeDtypeStruct((n_tok, B_WIDTH), F32),
        compiler_params=_cparams("parallel"),
        name="pool_seq",
    )(pb, pw, pbias, pscale)
    rows = dec_seq // GRID_W
    n_dec = (n_tok - n_ctx) // dec_seq
    ctx_blocks = n_ctx // dec_seq
    cb = 8
    shape4 = (n_tok // dec_seq, rows, GRID_W, B_WIDTH)
    blk = pl.BlockSpec((None, rows, cb, B_WIDTH), lambda b, c: (b + ctx_blocks, 0, c, 0))
    y_all = pl.pallas_call(
        _pool_grid_kernel,
        grid=(n_dec, GRID_W // cb),
        in_specs=[blk] + wspecs(2) + [pl.BlockSpec(memory_space=pl.ANY)],
        out_specs=blk,
        out_shape=jax.ShapeDtypeStruct(shape4, F32),
        input_output_aliases={4: 0},
        compiler_params=_cparams("parallel", "parallel"),
        name="pool_grid",
    )(pb.reshape(shape4), pw, pbias, pscale, y_ctx.reshape(shape4))
    return y_all.reshape(n_tok, B_WIDTH)


def _conv_kernel(pc_ref, cw_ref, cb_ref, lg_ref, lb_ref, o_ref, *, n_ctx_tiles):
    tm = pc_ref.shape[0]
    u = pc_ref[:, 0:C_WIDTH] * _sigmoid(pc_ref[:, C_WIDTH:2 * C_WIDTH])
    pad = jnp.zeros((CONV_PAD, C_WIDTH), F32)
    up = jnp.concatenate([pad, u, pad], axis=0)
    is_ctx = pl.program_id(0) < n_ctx_tiles
    t = lax.broadcasted_iota(jnp.int32, (tm, C_WIDTH), 0)
    pos = jnp.where(is_ctx, t, t & (GRID_W - 1))
    seg = jnp.where(is_ctx, tm, GRID_W)
    acc = jnp.zeros((tm, C_WIDTH), F32)
    for j in range(C_CONV):
        off = j - C_CONV // 2
        rolled = up if off == 0 else pltpu.roll(up, (-off) % (tm + 2 * CONV_PAD), 0)
        tap = rolled[CONV_PAD:CONV_PAD + tm]
        if off > 0:
            tap = jnp.where(pos < seg - off, tap, 0.0)
        elif off < 0:
            tap = jnp.where(pos >= -off, tap, 0.0)
        acc = acc + tap * cw_ref[pl.ds(j, 1), :]
    c = acc + cb_ref[...]
    mu = jnp.mean(c, axis=-1, keepdims=True)
    cc = c - mu
    var = jnp.mean(cc * cc, axis=-1, keepdims=True)
    y = cc * lax.rsqrt(var + LN_EPS) * lg_ref[...] + lb_ref[...]
    o_ref[...] = y * _sigmoid(y)


def _conv_mix(pc, tm, n_ctx_tiles, cw, cb, lg, lb):
    n_tok = pc.shape[0]
    row = pl.BlockSpec((1, C_WIDTH), lambda i: (0, 0))
    return pl.pallas_call(
        functools.partial(_conv_kernel, n_ctx_tiles=n_ctx_tiles),
        grid=(n_tok // tm,),
        in_specs=[pl.BlockSpec((tm, 2 * C_WIDTH), lambda i: (i, 0)),
                  pl.BlockSpec((C_CONV, C_WIDTH), lambda i: (0, 0)), row, row, row],
        out_specs=pl.BlockSpec((tm, C_WIDTH), lambda i: (i, 0)),
        out_shape=jax.ShapeDtypeStruct((n_tok, C_WIDTH), F32),
        compiler_params=_cparams("parallel"),
        name="conv_mix",
    )(pc, cw, cb, lg, lb)


def _merge_kernel(y_ref, bv_ref, g_ref, yb_ref, yc_ref, pg_ref, x_ref, mod_ref, gng_ref, gnb_ref,
                  n2g_ref, wa_ref, wb_ref, wc_ref, wo_ref, ones_ref, xo_ref, h2_ref):
    d = D_MODEL
    ones_bd = ones_ref[...]
    y = y_ref[...]
    mean = _segsum(y, ones_bd) * (1.0 / A_HEAD)
    yc = y - mean
    var = _segsum(yc * yc, ones_bd) * (1.0 / A_HEAD)
    yn = yc * lax.rsqrt(var + GN_EPS) * gng_ref[...] + gnb_ref[...]
    ya = (yn + bv_ref[...]) * g_ref[...]
    merged = (_sigmoid(pg_ref[:, 0:d]) * _dot(ya.astype(BF16), wa_ref[...])
              + _sigmoid(pg_ref[:, d:2 * d]) * _dot(yb_ref[...].astype(BF16), wb_ref[...])
              + _sigmoid(pg_ref[:, 2 * d:3 * d]) * _dot(yc_ref[...].astype(BF16), wc_ref[...]))
    gt1 = mod_ref[0, :, 2 * d:3 * d]
    xn = x_ref[...] + gt1 * _dot(merged.astype(BF16), wo_ref[...])
    xo_ref[...] = xn
    ms = jnp.mean(xn * xn, axis=-1, keepdims=True)
    h2 = xn * lax.rsqrt(ms + EPS) * n2g_ref[...]
    sh2 = mod_ref[0, :, 3 * d:4 * d]
    sc2 = mod_ref[0, :, 4 * d:5 * d]
    h2_ref[...] = (h2 * (1.0 + sc2) + sh2).astype(BF16)


def _merge(y, bv, g, yb, yc, pg, x, mod3, mod_map, tm, gng, gnb, n2g, wa, wb, wc, wo, ones_bd):
    n_tok = x.shape[0]
    d = D_MODEL
    tile = lambda n: pl.BlockSpec((tm, n), lambda i: (i, 0))
    row = pl.BlockSpec((1, d), lambda i: (0, 0))
    full = lambda a: pl.BlockSpec((a, d), lambda i: (0, 0))
    return pl.pallas_call(
        _merge_kernel,
        grid=(n_tok // tm,),
        in_specs=[tile(d), tile(d), tile(d), tile(B_WIDTH), tile(C_WIDTH), tile(3 * d), tile(d),
                  pl.BlockSpec((1, 1, 6 * d), mod_map), row, row, row,
                  full(d), full(B_WIDTH), full(C_WIDTH), full(d), full(d)],
        out_specs=[tile(d), tile(d)],
        out_shape=[jax.ShapeDtypeStruct((n_tok, d), F32), jax.ShapeDtypeStruct((n_tok, d), BF16)],
        compiler_params=_cparams("parallel"),
        name="merge_outproj",
    )(y, bv, g, yb, yc, pg, x, mod3, gng, gnb, n2g, wa, wb, wc, wo, ones_bd)


NO_RANK = 1e9
RANK_BASE = -2.0 ** 100


def _peer_score_kernel(h_ref, wq_ref, keys_ref, rk0_o, e0_o, th1_o, e1_o, s_scr, top_scr, rank_scr):
    tmp = h_ref.shape[0]
    q_t = lax.dot_general(wq_ref[...], h_ref[...], (((1,), (1,)), ((), ())),
                          preferred_element_type=F32).astype(BF16)
    for hp in range(2 * P_HEADS):
        s_scr[hp] = _dot(keys_ref[hp], q_t[hp * P_HALF:(hp + 1) * P_HALF, :])

    neg = jnp.float32(-jnp.inf)
    half = P_TOPK // 2
    row16 = lax.broadcasted_iota(jnp.int32, (P_TOPK, LANES), 0)
    row8 = lax.broadcasted_iota(jnp.int32, (half, LANES), 0)
    n_groups = tmp // LANES

    def top16_pair(sets):
        vals = [jnp.full((P_TOPK, LANES), neg, F32) for _ in sets]
        sets = list(sets)
        for it in range(P_TOPK):
            for n in range(len(sets)):
                m = jnp.max(sets[n], axis=0, keepdims=True)
                vals[n] = jnp.where(row16 == it, m, vals[n])
                sets[n] = jnp.where(sets[n] >= m, RANK_BASE * (1.0 + it / P_TOPK), sets[n])
        ranks = [jnp.where(s <= RANK_BASE, s * (P_TOPK / RANK_BASE) - P_TOPK, NO_RANK) for s in sets]
        return vals, ranks

    n_items = P_HEADS * n_groups

    def item(idx):
        return idx // n_groups, pl.ds(pl.multiple_of((idx % n_groups) * LANES, LANES), LANES)

    def extract(idx):
        h, ls = item(idx)
        (a16, b16), (rank0, rank1) = top16_pair((s_scr[2 * h, :, ls], s_scr[2 * h + 1, :, ls]))
        rk0_o[h, :, ls] = rank0
        top_scr[0] = a16
        top_scr[1] = b16
        rank_scr[...] = rank1

    def combine(idx):
        h, ls = item(idx)
        a16 = top_scr[0]
        b16 = top_scr[1]
        rank1 = rank_scr[...]
        s0 = s_scr[2 * h, :, ls]
        s1 = s_scr[2 * h + 1, :, ls]
        b_lo = b16[0:half, :]
        cands = [a16[0:1, :] + b16, a16[1:2, :] + b_lo]
        for r in range(2, half):
            cands.append(jnp.where(row8 < P_TOPK // (r + 1), a16[r:r + 1, :] + b_lo, neg))
        cands.append(a16[half:P_TOPK, :] + b16[0:1, :])
        cmax = a16[0:1, :] + b16[0:1, :]
        work = list(cands)
        tau = cmax
        for it in range(P_TOPK):
            m = jnp.maximum(work[0][0:half, :], work[0][half:P_TOPK, :])
            for c in work[1:]:
                m = jnp.maximum(m, c)
            tau = jnp.max(m, axis=0, keepdims=True)
            if it + 1 < P_TOPK:
                work = [jnp.where(c >= tau, neg, c) for c in work]
        z = jnp.zeros((1, LANES), F32)
        sels = []
        for c in cands:
            sel = c >= tau
            z = z + jnp.sum(jnp.where(sel, jnp.exp(c - cmax), 0.0), axis=0, keepdims=True)
            sels.append(jnp.where(sel, 1.0, 0.0))
        cnt_hi = sels[0][half:P_TOPK, :]
        cnt_lo = sels[0][0:half, :]
        for c in sels[1:half]:
            cnt_lo = cnt_lo + c
        cnt_lo = cnt_lo + jnp.where(row8 == 0, jnp.sum(sels[half], axis=0, keepdims=True), 0.0)
        th1 = jnp.zeros(rank1.shape, F32)
        for q in range(P_TOPK):
            cnt = cnt_lo[q:q + 1, :] if q < half else cnt_hi[q - half:q - half + 1, :]
            th1 = jnp.where(rank1 == float(q), cnt, th1)
        th1_o[h, :, ls] = th1.astype(BF16)
        e0_o[h, :, ls] = jnp.exp(s0 - a16[0:1, :]) / z
        e1_o[h, :, ls] = jnp.exp(s1 - b16[0:1, :]).astype(BF16)

    top_scr[...] = jnp.zeros_like(top_scr)
    rank_scr[...] = jnp.zeros_like(rank_scr)

    def body(n, carry):
        combine(jnp.maximum(n - 1, 0))
        extract(jnp.minimum(n, n_items - 1))
        return carry

    lax.fori_loop(0, n_items + 1, body, 0)


def _peer_scores(h2, wq_t, keys):
    n_tok = h2.shape[0]
    tmp = TM_PEER
    big = pl.BlockSpec((P_HEADS, P_KEYS, tmp), lambda i: (0, 0, i))
    big_shape = jax.ShapeDtypeStruct((P_HEADS, P_KEYS, n_tok), F32)
    narrow_shape = jax.ShapeDtypeStruct((P_HEADS, P_KEYS, n_tok), BF16)
    return pl.pallas_call(
        _peer_score_kernel,
        grid=(n_tok // tmp,),
        in_specs=[pl.BlockSpec((tmp, D_MODEL), lambda i: (i, 0)),
                  pl.BlockSpec((2 * P_HEADS * P_HALF, D_MODEL), lambda i: (0, 0)),
                  pl.BlockSpec((2 * P_HEADS, P_KEYS, P_HALF), lambda i: (0, 0, 0))],
        out_specs=[big, big, big, big],
        out_shape=[big_shape, big_shape, narrow_shape, narrow_shape],
        scratch_shapes=[pltpu.VMEM((2 * P_HEADS, P_KEYS, tmp), F32),
                        pltpu.VMEM((2, P_TOPK, LANES), F32), pltpu.VMEM((P_KEYS, LANES), F32)],
        compiler_params=_cparams("parallel"),
        name="peer_scores",
    )(h2, wq_t, keys)


def _peer_expert_kernel(h_ref, u_ref, vt_ref, rk0_ref, e0_ref, th1_ref, e1_ref, x_ref,
                        mod_ref, *rest, final_norm):
    if final_norm:
        fg_ref, o_ref, acc_ref, gated_ref = rest
    else:
        o_ref, acc_ref, gated_ref = rest
    e = pl.program_id(1)
    tmp = h_ref.shape[0]

    @pl.when(e == 0)
    def _():
        acc_ref[...] = jnp.zeros_like(acc_ref)

    for il in range(PEER_EB // P_KEYS):
        rows = slice(il * P_KEYS, (il + 1) * P_KEYS)
        for lc in range(tmp // LANES):
            ls = slice(lc * LANES, (lc + 1) * LANES)
            w = None
            for h in range(P_HEADS):
                rk0 = jnp.broadcast_to(rk0_ref[h, pl.ds(il, 1), ls], (P_KEYS, LANES)).astype(BF16)
                e0 = jnp.broadcast_to(e0_ref[h, pl.ds(il, 1), ls], (P_KEYS, LANES)).astype(BF16)
                term = jnp.where(rk0 < th1_ref[h, :, ls], e0 * e1_ref[h, :, ls], jnp.zeros((), BF16))
                w = term if w is None else w + term
            gated_ref[rows, ls] = w
    hid = lax.dot_general(u_ref[...], h_ref[...], (((1,), (1,)), ((), ())),
                          preferred_element_type=F32)
    act = 0.5 * hid * (1.0 + lax.erf(hid * (2.0 ** -0.5)))
    acc_ref[...] += _dot(vt_ref[...], gated_ref[...] * act.astype(BF16))

    @pl.when(e == pl.num_programs(1) - 1)
    def _():
        gt2 = mod_ref[0, :, 5 * D_MODEL:6 * D_MODEL]
        xn = x_ref[...] + gt2 * acc_ref[...].T
        if final_norm:
            ms = jnp.mean(xn * xn, axis=-1, keepdims=True)
            xn = xn * lax.rsqrt(ms + EPS) * fg_ref[...]
        o_ref[...] = xn


def _peer_experts(h2, u_bf, vt_bf, rk0, e0, th1, e1, x, mod3, mod_map, final_g=None):
    n_tok = h2.shape[0]
    tmp = TM_PEER
    n_i = PEER_EB // P_KEYS
    sel = pl.BlockSpec((P_HEADS, n_i, tmp), lambda i, e: (0, e, i))
    full = pl.BlockSpec((P_HEADS, P_KEYS, tmp), lambda i, e: (0, 0, i))
    mod_map2 = lambda i, e: mod_map(i)
    tile = pl.BlockSpec((tmp, D_MODEL), lambda i, e: (i, 0))
    out = jax.ShapeDtypeStruct((n_tok, D_MODEL), F32)
    final_norm = final_g is not None
    extra_in = [pl.BlockSpec((1, D_MODEL), lambda i, e: (0, 0))] if final_norm else []
    extra_args = [final_g] if final_norm else []
    return pl.pallas_call(
        functools.partial(_peer_expert_kernel, final_norm=final_norm),
        grid=(n_tok // tmp, P_EXPERTS // PEER_EB),
        in_specs=[tile,
                  pl.BlockSpec((PEER_EB, D_MODEL), lambda i, e: (e, 0)),
                  pl.BlockSpec((D_MODEL, PEER_EB), lambda i, e: (0, e)),
                  sel, sel, full, full, tile,
                  pl.BlockSpec((1, 1, 6 * D_MODEL), mod_map2)] + extra_in,
        out_specs=tile,
        out_shape=out,
        scratch_shapes=[pltpu.VMEM((D_MODEL, tmp), F32), pltpu.VMEM((PEER_EB, tmp), BF16)],
        compiler_params=_cparams("parallel", "arbitrary"),
        name="peer_experts",
    )(h2, u_bf, vt_bf, rk0, e0, th1, e1, x, mod3, *extra_args)


def _block_diag2(w):
    z = jnp.zeros_like(w[0])
    return jnp.concatenate([jnp.concatenate([w[0], z], axis=1),
                            jnp.concatenate([z, w[1]], axis=1)], axis=0)


def kernel(x_prompt, x_sample, state_rwkv, c, c_ctx, ada_w, ada_b, norm1_g, norm2_g, w_in, shift_mu,
           decay_w0, decay_w2, iclr_a0, iclr_a2, gate_g2, k_k, k_a, r_k, gn_g, gn_b, w_a_out,
           pool_w, pool_b, pool_scale, w_b_out, conv_w, conv_b, cln_g, cln_b, w_c_out, w_out,
           peer_wq, peer_keys, peer_u, peer_v, final_g):
    d = D_MODEL
    batch, seq, _ = x_prompt.shape
    dec_batch, dec_seq, _ = x_sample.shape
    depth = ada_w.shape[0]
    n_ctx = batch * seq
    n_dec = dec_batch * dec_seq
    tm = seq
    assert n_ctx % dec_seq == 0 and dec_seq % tm == 0 and dec_seq % GRID_W == 0
    assert n_ctx % TM_PROJ == 0 and dec_seq % TM_PROJ == 0 and dec_seq % TM_PEER == 0
    assert n_ctx % TM_PEER == 0 and tm % GRID_W == 0 and dec_batch <= 7 and TM_PROJ % tm == 0

    x = jnp.concatenate([x_prompt.reshape(n_ctx, d), x_sample.reshape(n_dec, d)], axis=0)
    n_tok = n_ctx + n_dec

    cond8 = jnp.zeros((8, d), F32).at[0].set(c_ctx).at[1:1 + dec_batch].set(c)
    mod_all = _modulation(cond8, ada_w, ada_b)

    seg_ids = jnp.arange(d) // A_HEAD
    ones_bd = (seg_ids[:, None] == seg_ids[None, :]).astype(BF16)

    map_proj = _mod_row_map(n_ctx // TM_PROJ, dec_seq // TM_PROJ)
    map_tm = _mod_row_map(n_ctx // tm, dec_seq // tm)
    map_peer = _mod_row_map(n_ctx // TM_PEER, dec_seq // TM_PEER)

    n_tiles = n_tok // tm
    tile_idx = jnp.arange(n_tiles)
    dec_tile = (tile_idx - n_ctx // tm) % (dec_seq // tm)
    is_start = jnp.where(tile_idx < n_ctx // tm, True, dec_tile == 0)[:, None]
    is_end = jnp.where(tile_idx < n_ctx // tm, True, dec_tile == dec_seq // tm - 1)[:, None]

    ctx_states = []
    for l in range(depth):
        mod3 = mod_all[l][:, None, :]
        w_in_bf = w_in[l].astype(BF16)
        pa, edges = _inproj(x, mod3, norm1_g[l][None], w_in_bf[:, :A_COLS], (A_COLS,), map_proj,
                            "inproj_a", edge_tile=tm)
        pb, pc, pg = _inproj(x, mod3, norm1_g[l][None], w_in_bf[:, A_COLS:],
                             (B_WIDTH, 2 * C_WIDTH, 3 * d), map_proj, "inproj_bcg")

        zero_row = jnp.zeros((1, A_COLS), F32)
        halo_prev = jnp.concatenate([zero_row, edges[:-1, 1, :]], axis=0)
        halo_next = jnp.concatenate([edges[1:, 0, :], zero_row], axis=0)
        halo_prev = jnp.where(is_start, 0.0, halo_prev)[:, None, :]
        halo_next = jnp.where(is_end, 0.0, halo_next)[:, None, :]
        prep_params = (shift_mu[l][None], decay_w0[l].reshape(1, 2 * d),
                       _block_diag2(decay_w2[l]).astype(BF16), iclr_a0[l].reshape(1, 2 * d),
                       _block_diag2(iclr_a2[l]).astype(BF16), gate_g2[l].astype(BF16),
                       k_k[l][None], k_a[l][None], r_k[l].reshape(1, d), ones_bd)
        r, v, kk, dec, kd, bb, bv, g = _rwkv_prep(pa, halo_prev, halo_next, tm, prep_params)

        y_scan = None
        for seq_len, batch0, nb, s0 in ((seq, 0, batch, None),
                                        (dec_seq, n_ctx // dec_seq, dec_batch, state_rwkv[:, l])):
            vs = _lane_plan(nb)[1]
            if s0 is None:
                s0_c = jnp.zeros((2, A_HEAD, A_HEAD // vs, vs * nb * A_HEADS), F32)
            else:
                s0_c = _state_to_chain(s0, vs)
            lay = lambda a, value_rows=False: _to_chain(a, seq_len, batch0, nb, value_rows)
            y_fwd, y_bwd, sfin_c = _rwkv_scan(lay(kk)[0], lay(r)[0], lay(dec), lay(bb), lay(kd),
                                              lay(v, True)[0], s0_c)
            y_scan = _from_chain(y_fwd, y_bwd, y_scan, n_tok, seq_len, batch0, nb)
            if s0 is None:
                ctx_states.append(_state_from_chain(sfin_c, nb, vs))

        yb = _pool_mix(pb, n_ctx, seq, dec_seq, pool_w[l].astype(BF16), pool_b[l].reshape(1, B_WIDTH),
                       pool_scale[l][None])
        yc = _conv_mix(pc, tm, n_ctx // tm, conv_w[l], conv_b[l][None], cln_g[l][None], cln_b[l][None])

        x, h2 = _merge(y_scan, bv, g, yb, yc, pg, x, mod3, map_tm, tm, gn_g[l][None], gn_b[l][None],
                       norm2_g[l][None], w_a_out[l].astype(BF16), w_b_out[l].astype(BF16),
                       w_c_out[l].astype(BF16), w_out[l].astype(BF16), ones_bd)

        wq_t = peer_wq[l].T.astype(BF16)
        keys = peer_keys[l].reshape(2 * P_HEADS, P_KEYS, P_HALF).astype(BF16)
        rk0, e0p, th1, e1p = _peer_scores(h2, wq_t, keys)
        x = _peer_experts(h2, peer_u[l].astype(BF16), peer_v[l].T.astype(BF16), rk0, e0p, th1, e1p,
                          x, mod3, map_peer, final_g[None] if l == depth - 1 else None)

    y = x
    new_state = jnp.stack(ctx_states, axis=1)
    return (y[:n_ctx].reshape(batch, seq, d), y[n_ctx:].reshape(dec_batch, dec_seq, d), new_state)
```

```python
import functools

import jax
import jax.numpy as jnp
from jax import lax
from jax.experimental import pallas as pl
from jax.experimental.pallas import tpu as pltpu

F32 = jnp.float32
BF16 = jnp.bfloat16

D_MODEL = 1024
A_HEADS = 16
A_HEAD = 64
A_LORA = 384
A_COLS = 3 * D_MODEL + A_LORA
B_WIDTH = 512
B_GROUP = 128
POOL_WINDOWS = (2, 4, 8, 16)
POOL_PAD = 16
C_WIDTH = 512
C_CONV = 31
CONV_PAD = 16
GRID_W = 64
P_HEADS = 8
P_KEYS = 128
P_TOPK = 16
P_HALF = 128
P_EXPERTS = P_KEYS * P_KEYS
EPS = 1e-6
GN_EPS = 64e-5
LN_EPS = 1e-5

LANES = 128
VMEM_LIMIT = 56 * 1024 * 1024

TM_PROJ = 512
TM_PEER = 512
PEER_EB = 1024
SCAN_TB = 8


def _cparams(*sem):
    return pltpu.CompilerParams(dimension_semantics=sem, vmem_limit_bytes=VMEM_LIMIT)


def _sigmoid(x):
    return 1.0 / (1.0 + jnp.exp(-x))


def _dot(a, b):
    return jnp.dot(a, b, preferred_element_type=F32)


def _segsum(x, ones_bd):
    hi = x.astype(BF16)
    lo = (x - hi.astype(F32)).astype(BF16)
    return _dot(hi, ones_bd) + _dot(lo, ones_bd)


def _mod_kernel(c_ref, w_ref, b_ref, o_ref):
    c = c_ref[...]
    s = c * _sigmoid(c)
    o_ref[...] = jnp.dot(s, w_ref[...], preferred_element_type=F32,
                         precision=lax.Precision.HIGHEST) + b_ref[...]


def _modulation(cond8, ada_w, ada_b):
    L = ada_w.shape[0]
    nb = 6
    return pl.pallas_call(
        _mod_kernel,
        grid=(L, nb),
        in_specs=[pl.BlockSpec((8, D_MODEL), lambda l, j: (0, 0)),
                  pl.BlockSpec((None, D_MODEL, D_MODEL), lambda l, j: (l, 0, j)),
                  pl.BlockSpec((None, 1, D_MODEL), lambda l, j: (l, 0, j))],
        out_specs=pl.BlockSpec((None, 8, D_MODEL), lambda l, j: (l, 0, j)),
        out_shape=jax.ShapeDtypeStruct((L, 8, 6 * D_MODEL), F32),
        compiler_params=_cparams("parallel", "parallel"),
        name="adaln_mod",
    )(cond8, ada_w, ada_b.reshape(L, 1, 6 * D_MODEL))


def _mod_row_map(n_ctx_tiles, tiles_per_dec_seq):
    def index_map(i):
        row = jnp.where(i < n_ctx_tiles, 0, 1 + (i - n_ctx_tiles) // tiles_per_dec_seq)
        return (row, 0, 0)
    return index_map


def _inproj_kernel(x_ref, mod_ref, g_ref, w_ref, *o_refs, edge_tile):
    x = x_ref[...]
    ms = jnp.mean(x * x, axis=-1, keepdims=True)
    y = x * lax.rsqrt(ms + EPS) * g_ref[...]
    sh = mod_ref[0, :, 0:D_MODEL]
    sc = mod_ref[0, :, D_MODEL:2 * D_MODEL]
    h = (y * (1.0 + sc) + sh).astype(BF16)
    if edge_tile:
        *o_refs, edge_ref = o_refs
    off = 0
    for o_ref in o_refs:
        n = o_ref.shape[1]
        o_ref[...] = _dot(h, w_ref[:, off:off + n])
        off += n
    if edge_tile:
        first = o_refs[0]
        for t in range(x.shape[0] // edge_tile):
            edge_ref[t, 0:1, :] = first[t * edge_tile:t * edge_tile + 1, :]
            edge_ref[t, 1:2, :] = first[(t + 1) * edge_tile - 1:(t + 1) * edge_tile, :]


def _inproj(x, mod3, g, w, splits, mod_map, name, edge_tile=0):
    n_tok = x.shape[0]
    n_out = w.shape[1]
    out_specs = [pl.BlockSpec((TM_PROJ, n), lambda i: (i, 0)) for n in splits]
    out_shape = [jax.ShapeDtypeStruct((n_tok, n), F32) for n in splits]
    if edge_tile:
        per = TM_PROJ // edge_tile
        out_specs.append(pl.BlockSpec((per, 2, splits[0]), lambda i: (i, 0, 0)))
        out_shape.append(jax.ShapeDtypeStruct((n_tok // edge_tile, 2, splits[0]), F32))
    return pl.pallas_call(
        functools.partial(_inproj_kernel, edge_tile=edge_tile),
        grid=(n_tok // TM_PROJ,),
        in_specs=[pl.BlockSpec((TM_PROJ, D_MODEL), lambda i: (i, 0)),
                  pl.BlockSpec((1, 1, 6 * D_MODEL), mod_map),
                  pl.BlockSpec((1, D_MODEL), lambda i: (0, 0)),
                  pl.BlockSpec((D_MODEL, n_out), lambda i: (0, 0))],
        out_specs=out_specs,
        out_shape=out_shape,
        compiler_params=_cparams("parallel"),
        name=name,
    )(x, mod3, g, w)


def _prep_kernel(pa_ref, hp_ref, hn_ref, mu_ref, w0_ref, w2_ref, a0_ref, a2_ref, g2_ref,
                 kkw_ref, ka_ref, rk_ref, ones_ref,
                 r_o, v_o, kk_o, dec_o, kd_o, bb_o, bv_o, g_o):
    tm = pa_ref.shape[0]

    def shifted(c0, c1):
        x = pa_ref[:, c0:c1]
        row = lax.broadcasted_iota(jnp.int32, x.shape, 0)
        prev = jnp.where(row == 0, hp_ref[0, :, c0:c1], pltpu.roll(x, 1, 0))
        nxt = jnp.where(row == tm - 1, hn_ref[0, :, c0:c1], pltpu.roll(x, tm - 1, 0))
        return x + (0.5 * (prev + nxt) - x) * mu_ref[:, c0:c1]

    d = D_MODEL
    r = shifted(0, d)
    k = shifted(d, 2 * d)
    v = shifted(2 * d, 3 * d)
    lo = shifted(3 * d, 3 * d + A_LORA)
    w_lo = lo[:, 0:128]
    a_lo = lo[:, 128:256]
    g_lo = lo[:, 256:384]
    ones_bd = ones_ref[...]

    r_o[...] = r
    v_o[...] = v

    kkraw = k * kkw_ref[...]
    nrm = jnp.sqrt(_segsum(kkraw * kkraw, ones_bd))
    kk = kkraw / jnp.maximum(nrm, 1e-12)
    kk_o[...] = kk

    z = -(w0_ref[...] + _dot(jnp.tanh(w_lo).astype(BF16), w2_ref[...]))
    softplus = jnp.maximum(z, 0.0) + jnp.log(1.0 + jnp.exp(-jnp.abs(z)))
    dec_o[...] = jnp.exp(-jnp.exp(-softplus - 0.5))

    a = _sigmoid(a0_ref[...] + _dot(a_lo.astype(BF16), a2_ref[...]))
    ka = ka_ref[...]
    kd0 = k * (1.0 + (a[:, 0:d] - 1.0) * ka)
    kd1 = k * (1.0 + (a[:, d:2 * d] - 1.0) * ka)
    kd_o[:, 0:d] = kd0
    kd_o[:, d:2 * d] = kd1
    bb_o[:, 0:d] = kk * a[:, 0:d]
    bb_o[:, d:2 * d] = kk * a[:, d:2 * d]

    bonus = _segsum(r * (kd0 + kd1) * rk_ref[...], ones_bd)
    bv_o[...] = bonus * v
    g_o[...] = _dot(_sigmoid(g_lo).astype(BF16), g2_ref[...])


def _rwkv_prep(pa, halo_prev, halo_next, tm, params):
    n_tok = pa.shape[0]
    d = D_MODEL
    row = lambda n: pl.BlockSpec((1, n), lambda i: (0, 0))
    full = lambda a, b: pl.BlockSpec((a, b), lambda i: (0, 0))
    tile = lambda n: pl.BlockSpec((tm, n), lambda i: (i, 0))
    halo = pl.BlockSpec((1, 1, A_COLS), lambda i: (i, 0, 0))
    outs = [d, d, d, 2 * d, 2 * d, 2 * d, d, d]
    return pl.pallas_call(
        _prep_kernel,
        grid=(n_tok // tm,),
        in_specs=[tile(A_COLS), halo, halo, row(A_COLS), row(2 * d), full(128, 2 * d), row(2 * d),
                  full(128, 2 * d), full(128, d), row(d), row(d), row(d), full(d, d)],
        out_specs=[tile(n) for n in outs],
        out_shape=[jax.ShapeDtypeStruct((n_tok, n), F32) for n in outs],
        compiler_params=_cparams("parallel"),
        name="rwkv_prep",
    )(pa, halo_prev, halo_next, *params)


SCAN_ROWS = 16
SCAN_STATE_PAD = 8


def _scan_kernel(kkf, kkb, rf, rb, wf, wb, bf, bb, kdf, kdb, vf, vb, s0_ref,
                 yf_ref, yb_ref, sfin_ref, s_scr):
    nk, tb = kkf.shape[0], kkf.shape[1]
    j = pl.program_id(2)
    live = slice(0, vf.shape[1])

    @pl.when(j == 0)
    def _():
        s_scr[:, :, live, :] = s0_ref[...]

    def advance(d, t, rs, kk_ref, r_ref, w_ref, b_ref, kd_ref, v_ref, y_ref):
        parts = [None] * 4
        for k in range(nk):
            term = s_scr[d, k, live, :] * kk_ref[k, pl.ds(t, 1), :]
            parts[k % 4] = term if parts[k % 4] is None else parts[k % 4] + term
        sa = -((parts[0] + parts[1]) + (parts[2] + parts[3]))
        vt = v_ref[t, rs, :]
        ys = [None] * 2
        for k in range(nk):
            s_new = (s_scr[d, k, live, :] * w_ref[k, pl.ds(t, 1), :] + sa * b_ref[k, pl.ds(t, 1), :]
                     + vt * kd_ref[k, pl.ds(t, 1), :])
            s_scr[d, k, live, :] = s_new
            term = s_new * r_ref[k, pl.ds(t, 1), :]
            ys[k % 2] = term if ys[k % 2] is None else ys[k % 2] + term
        y_ref[t, rs, :] = ys[0] + ys[1]

    rs = slice(None)

    def step(i, carry):
        advance(0, i, rs, kkf, rf, wf, bf, kdf, vf, yf_ref)
        advance(1, tb - 1 - i, rs, kkb, rb, wb, bb, kdb, vb, yb_ref)
        return carry

    lax.fori_loop(0, tb, step, 0)

    @pl.when(j == pl.num_programs(2) - 1)
    def _():
        sfin_ref[...] = s_scr[:, :, live, :]


def _rwkv_scan(kk_c, r_c, w_c, b_c, kd_c, v_c, s0_c):
    nk, T, L = kk_c.shape
    vq = v_c.shape[1]
    tb = min(SCAN_TB, T)
    nt = T // tb
    fwd = lambda j: j
    bwd = lambda j: nt - 1 - j
    rows = min(vq, SCAN_ROWS)
    shared = lambda tj: pl.BlockSpec((nk, tb, LANES), lambda g, q, j: (0, tj(j), g))
    perdir = lambda d, tj: pl.BlockSpec((None, nk, tb, LANES), lambda g, q, j: (d, 0, tj(j), g))
    vspec = lambda tj: pl.BlockSpec((tb, rows, LANES), lambda g, q, j: (tj(j), q, g))
    sspec = pl.BlockSpec((2, nk, rows, LANES), lambda g, q, j: (0, 0, q, g))
    y_shape = jax.ShapeDtypeStruct((T, vq, L), F32)
    return pl.pallas_call(
        _scan_kernel,
        grid=(L // LANES, vq // rows, nt),
        in_specs=[shared(fwd), shared(bwd), shared(fwd), shared(bwd),
                  perdir(0, fwd), perdir(1, bwd), perdir(0, fwd), perdir(1, bwd),
                  perdir(0, fwd), perdir(1, bwd), vspec(fwd), vspec(bwd), sspec],
        out_specs=[vspec(fwd), vspec(bwd), sspec],
        out_shape=[y_shape, y_shape, jax.ShapeDtypeStruct((2, nk, vq, L), F32)],
        scratch_shapes=[pltpu.VMEM((2, nk, rows + SCAN_STATE_PAD, LANES), F32)],
        compiler_params=_cparams("parallel", "parallel", "arbitrary"),
        name="rwkv_scan",
    )(kk_c, kk_c, r_c, r_c, w_c, w_c, b_c, b_c, kd_c, kd_c, v_c, v_c, s0_c)


def _lane_plan(n_batch):
    full = LANES // A_HEADS
    if n_batch >= full:
        assert n_batch % full == 0
        return full, 1
    assert full % n_batch == 0
    return n_batch, full // n_batch


SLAB_PITCH = A_HEAD + 4


def _to_chain_kernel(x_ref, o_ref, scr, *, vs, value_rows):
    nb, tc = x_ref.shape[0], x_ref.shape[1]
    n_slabs = nb * A_HEADS
    for b in range(nb):
        xt = x_ref[b].T
        for h in range(A_HEADS):
            scr[pl.ds((b * A_HEADS + h) * SLAB_PITCH, A_HEAD), :] = xt[h * A_HEAD:(h + 1) * A_HEAD]
    row_of_slabs = lambda c: scr[pl.ds(c, n_slabs, stride=SLAB_PITCH), :]
    vq = A_HEAD // vs
    if value_rows:
        for v in range(vq):
            rows = [row_of_slabs(vh * vq + v) for vh in range(vs)]
            rows = rows[0] if vs == 1 else jnp.concatenate(rows, axis=0)
            o_ref[:, v, :] = rows.T
    else:
        for k in range(A_HEAD):
            rows = row_of_slabs(k)
            if vs > 1:
                rows = jnp.concatenate([rows] * vs, axis=0)
            o_ref[k] = rows.T


def _to_chain(x, seq_len, batch0, n_batch, value_rows):
    n_col = x.shape[1] // D_MODEL
    nb, vs = _lane_plan(n_batch)
    assert batch0 % nb == 0 and seq_len % LANES == 0
    groups = n_batch // nb
    tc = LANES
    x3 = x.reshape(x.shape[0] // seq_len, seq_len, n_col * D_MODEL)
    vq = A_HEAD // vs
    if value_rows:
        out_spec = pl.BlockSpec((None, tc, vq, LANES), lambda c, g, j: (c, j, 0, g))
        out_shape = jax.ShapeDtypeStruct((n_col, seq_len, vq, groups * LANES), F32)
    else:
        out_spec = pl.BlockSpec((None, A_HEAD, tc, LANES), lambda c, g, j: (c, 0, j, g))
        out_shape = jax.ShapeDtypeStruct((n_col, A_HEAD, seq_len, groups * LANES), F32)
    return pl.pallas_call(
        functools.partial(_to_chain_kernel, vs=vs, value_rows=value_rows),
        grid=(n_col, groups, seq_len // tc),
        in_specs=[pl.BlockSpec((nb, tc, D_MODEL), lambda c, g, j: (batch0 // nb + g, j, c))],
        out_specs=out_spec,
        out_shape=out_shape,
        scratch_shapes=[pltpu.VMEM((nb * A_HEADS * SLAB_PITCH, tc), F32)],
        compiler_params=_cparams("parallel", "parallel", "parallel"),
        name="to_chain_v" if value_rows else "to_chain_k",
    )(x3)


def _from_chain_kernel(*refs, vs):
    yf_ref, yb_ref, o_ref, scr = refs[-4:]
    nb, tc = o_ref.shape[0], o_ref.shape[1]
    vq = A_HEAD // vs
    rows = nb * A_HEADS
    for v in range(vq):
        yv = (yf_ref[:, v, :] + yb_ref[:, v, :]).T
        for vh in range(vs):
            scr[pl.ds(vh * vq + v, rows, stride=SLAB_PITCH), :] = yv[vh * rows:(vh + 1) * rows]
    for b in range(nb):
        slabs = [scr[pl.ds((b * A_HEADS + h) * SLAB_PITCH, A_HEAD), :] for h in range(A_HEADS)]
        o_ref[b] = jnp.concatenate(slabs, axis=0).T


def _from_chain(y_fwd, y_bwd, prev_rows, n_tok, seq_len, batch0, n_batch):
    nb, vs = _lane_plan(n_batch)
    groups = n_batch // nb
    tc = LANES
    vq = A_HEAD // vs
    in_specs = [pl.BlockSpec((tc, vq, LANES), lambda g, j: (j, 0, g))] * 2
    args = [y_fwd, y_bwd]
    aliases = {}
    if prev_rows is not None:
        in_specs = [pl.BlockSpec(memory_space=pl.ANY)] + in_specs
        args = [prev_rows.reshape(n_tok // seq_len, seq_len, D_MODEL)] + args
        aliases = {0: 0}
    out = pl.pallas_call(
        functools.partial(_from_chain_kernel, vs=vs),
        grid=(groups, seq_len // tc),
        in_specs=in_specs,
        out_specs=pl.BlockSpec((nb, tc, D_MODEL), lambda g, j: (batch0 // nb + g, j, 0)),
        out_shape=jax.ShapeDtypeStruct((n_tok // seq_len, seq_len, D_MODEL), F32),
        scratch_shapes=[pltpu.VMEM((nb * A_HEADS * SLAB_PITCH, tc), F32)],
        input_output_aliases=aliases,
        compiler_params=_cparams("parallel", "parallel"),
        name="from_chain",
    )(*args)
    return out.reshape(n_tok, D_MODEL)


def _state_to_chain(s, vs):
    B = s.shape[0]
    vq = A_HEAD // vs
    s = jnp.transpose(s, (1, 4, 3, 0, 2)).reshape(2, A_HEAD, vs, vq, B * A_HEADS)
    return jnp.transpose(s, (0, 1, 3, 2, 4)).reshape(2, A_HEAD, vq, vs * B * A_HEADS)


def _state_from_chain(s_c, n_batch, vs):
    vq = A_HEAD // vs
    s = s_c.reshape(2, A_HEAD, vq, vs, n_batch, A_HEADS)
    return jnp.transpose(s, (4, 0, 5, 3, 2, 1)).reshape(n_batch, 2, A_HEADS, A_HEAD, A_HEAD)


def _pool_tail(d2, g, pw_ref, pb_ref, ps_ref):
    sl = slice(g * B_GROUP, (g + 1) * B_GROUP)
    y = _dot(d2.astype(BF16), pw_ref[g]) + pb_ref[:, sl]
    return y * ps_ref[:, sl]


def _window_count(t, w, n):
    lo = jnp.clip(t - w // 2, 0, n)
    hi = jnp.clip(t - w // 2 + w, 0, n)
    return (hi - lo).astype(F32)


def _pool_seq_kernel(z_ref, pw_ref, pb_ref, ps_ref, o_ref):
    n = z_ref.shape[0]
    pad = jnp.zeros((POOL_PAD, B_GROUP), F32)
    t = lax.broadcasted_iota(jnp.int32, (n, B_GROUP), 0)
    for g, w in enumerate(POOL_WINDOWS):
        sl = slice(g * B_GROUP, (g + 1) * B_GROUP)
        z = z_ref[:, sl]
        zp = jnp.concatenate([pad, z, pad], axis=0)
        acc = None
        for j in range(w):
            off = j - w // 2
            term = zp if off == 0 else pltpu.roll(zp, (-off) % (n + 2 * POOL_PAD), 0)
            acc = term if acc is None else acc + term
        d = acc[POOL_PAD:POOL_PAD + n] / _window_count(t, w, n) - z
        o_ref[:, sl] = _pool_tail(d, g, pw_ref, pb_ref, ps_ref)


def _pool_grid_kernel(z_ref, pw_ref, pb_ref, ps_ref, _, o_ref):
    rows, cb = z_ref.shape[0], z_ref.shape[1]
    pad = jnp.zeros((POOL_PAD, cb, B_GROUP), F32)
    t = lax.broadcasted_iota(jnp.int32, (rows, cb, B_GROUP), 0)
    for g, w in enumerate(POOL_WINDOWS):
        sl = slice(g * B_GROUP, (g + 1) * B_GROUP)
        z = z_ref[:, :, sl]
        zp = jnp.concatenate([pad, z, pad], axis=0)
        acc = None
        for j in range(w):
            s = POOL_PAD + j - w // 2
            term = zp[s:s + rows]
            acc = term if acc is None else acc + term
        d = acc / _window_count(t, w, rows) - z
        y = _pool_tail(d.reshape(rows * cb, B_GROUP), g, pw_ref, pb_ref, ps_ref)
        o_ref[:, :, sl] = y.reshape(rows, cb, B_GROUP)


def _pool_mix(pb, n_ctx, seq, dec_seq, pw, pbias, pscale):
    n_tok = pb.shape[0]
    wspecs = lambda nd: [pl.BlockSpec((4, B_GROUP, B_GROUP), lambda *a: (0, 0, 0)),
                         pl.BlockSpec((1, B_WIDTH), lambda *a: (0, 0)),
                         pl.BlockSpec((1, B_WIDTH), lambda *a: (0, 0))]
    y_ctx = pl.pallas_call(
        _pool_seq_kernel,
        grid=(n_ctx // seq,),
        in_specs=[pl.BlockSpec((seq, B_WIDTH), lambda i: (i, 0))] + wspecs(1),
        out_specs=pl.BlockSpec((seq, B_WIDTH), lambda i: (i, 0)),
        out_shape=jax.ShapeDtypeStruct((n_tok, B_WIDTH), F32),
        compiler_params=_cparams("parallel"),
        name="pool_seq",
    )(pb, pw, pbias, pscale)
    rows = dec_seq // GRID_W
    n_dec = (n_tok - n_ctx) // dec_seq
    ctx_blocks = n_ctx // dec_seq
    cb = 8
    shape4 = (n_tok // dec_seq, rows, GRID_W, B_WIDTH)
    blk = pl.BlockSpec((None, rows, cb, B_WIDTH), lambda b, c: (b + ctx_blocks, 0, c, 0))
    y_all = pl.pallas_call(
        _pool_grid_kernel,
        grid=(n_dec, GRID_W // cb),
        in_specs=[blk] + wspecs(2) + [pl.BlockSpec(memory_space=pl.ANY)],
        out_specs=blk,
        out_shape=jax.ShapeDtypeStruct(shape4, F32),
        input_output_aliases={4: 0},
        compiler_params=_cparams("parallel", "parallel"),
        name="pool_grid",
    )(pb.reshape(shape4), pw, pbias, pscale, y_ctx.reshape(shape4))
    return y_all.reshape(n_tok, B_WIDTH)


def _conv_kernel(pc_ref, cw_ref, cb_ref, lg_ref, lb_ref, o_ref, *, n_ctx_tiles):
    tm = pc_ref.shape[0]
    u = pc_ref[:, 0:C_WIDTH] * _sigmoid(pc_ref[:, C_WIDTH:2 * C_WIDTH])
    pad = jnp.zeros((CONV_PAD, C_WIDTH), F32)
    up = jnp.concatenate([pad, u, pad], axis=0)
    is_ctx = pl.program_id(0) < n_ctx_tiles
    t = lax.broadcasted_iota(jnp.int32, (tm, C_WIDTH), 0)
    pos = jnp.where(is_ctx, t, t & (GRID_W - 1))
    seg = jnp.where(is_ctx, tm, GRID_W)
    acc = jnp.zeros((tm, C_WIDTH), F32)
    for j in range(C_CONV):
        off = j - C_CONV // 2
        rolled = up if off == 0 else pltpu.roll(up, (-off) % (tm + 2 * CONV_PAD), 0)
        tap = rolled[CONV_PAD:CONV_PAD + tm]
        if off > 0:
            tap = jnp.where(pos < seg - off, tap, 0.0)
        elif off < 0:
            tap = jnp.where(pos >= -off, tap, 0.0)
        acc = acc + tap * cw_ref[pl.ds(j, 1), :]
    c = acc + cb_ref[...]
    mu = jnp.mean(c, axis=-1, keepdims=True)
    cc = c - mu
    var = jnp.mean(cc * cc, axis=-1, keepdims=True)
    y = cc * lax.rsqrt(var + LN_EPS) * lg_ref[...] + lb_ref[...]
    o_ref[...] = y * _sigmoid(y)


def _conv_mix(pc, tm, n_ctx_tiles, cw, cb, lg, lb):
    n_tok = pc.shape[0]
    row = pl.BlockSpec((1, C_WIDTH), lambda i: (0, 0))
    return pl.pallas_call(
        functools.partial(_conv_kernel, n_ctx_tiles=n_ctx_tiles),
        grid=(n_tok // tm,),
        in_specs=[pl.BlockSpec((tm, 2 * C_WIDTH), lambda i: (i, 0)),
                  pl.BlockSpec((C_CONV, C_WIDTH), lambda i: (0, 0)), row, row, row],
        out_specs=pl.BlockSpec((tm, C_WIDTH), lambda i: (i, 0)),
        out_shape=jax.ShapeDtypeStruct((n_tok, C_WIDTH), F32),
        compiler_params=_cparams("parallel"),
        name="conv_mix",
    )(pc, cw, cb, lg, lb)


def _merge_kernel(y_ref, bv_ref, g_ref, yb_ref, yc_ref, pg_ref, x_ref, mod_ref, gng_ref, gnb_ref,
                  n2g_ref, wa_ref, wb_ref, wc_ref, wo_ref, ones_ref, xo_ref, h2_ref):
    d = D_MODEL
    ones_bd = ones_ref[...]
    y = y_ref[...]
    mean = _segsum(y, ones_bd) * (1.0 / A_HEAD)
    yc = y - mean
    var = _segsum(yc * yc, ones_bd) * (1.0 / A_HEAD)
    yn = yc * lax.rsqrt(var + GN_EPS) * gng_ref[...] + gnb_ref[...]
    ya = (yn + bv_ref[...]) * g_ref[...]
    merged = (_sigmoid(pg_ref[:, 0:d]) * _dot(ya.astype(BF16), wa_ref[...])
              + _sigmoid(pg_ref[:, d:2 * d]) * _dot(yb_ref[...].astype(BF16), wb_ref[...])
              + _sigmoid(pg_ref[:, 2 * d:3 * d]) * _dot(yc_ref[...].astype(BF16), wc_ref[...]))
    gt1 = mod_ref[0, :, 2 * d:3 * d]
    xn = x_ref[...] + gt1 * _dot(merged.astype(BF16), wo_ref[...])
    xo_ref[...] = xn
    ms = jnp.mean(xn * xn, axis=-1, keepdims=True)
    h2 = xn * lax.rsqrt(ms + EPS) * n2g_ref[...]
    sh2 = mod_ref[0, :, 3 * d:4 * d]
    sc2 = mod_ref[0, :, 4 * d:5 * d]
    h2_ref[...] = (h2 * (1.0 + sc2) + sh2).astype(BF16)


def _merge(y, bv, g, yb, yc, pg, x, mod3, mod_map, tm, gng, gnb, n2g, wa, wb, wc, wo, ones_bd):
    n_tok = x.shape[0]
    d = D_MODEL
    tile = lambda n: pl.BlockSpec((tm, n), lambda i: (i, 0))
    row = pl.BlockSpec((1, d), lambda i: (0, 0))
    full = lambda a: pl.BlockSpec((a, d), lambda i: (0, 0))
    return pl.pallas_call(
        _merge_kernel,
        grid=(n_tok // tm,),
        in_specs=[tile(d), tile(d), tile(d), tile(B_WIDTH), tile(C_WIDTH), tile(3 * d), tile(d),
                  pl.BlockSpec((1, 1, 6 * d), mod_map), row, row, row,
                  full(d), full(B_WIDTH), full(C_WIDTH), full(d), full(d)],
        out_specs=[tile(d), tile(d)],
        out_shape=[jax.ShapeDtypeStruct((n_tok, d), F32), jax.ShapeDtypeStruct((n_tok, d), BF16)],
        compiler_params=_cparams("parallel"),
        name="merge_outproj",
    )(y, bv, g, yb, yc, pg, x, mod3, gng, gnb, n2g, wa, wb, wc, wo, ones_bd)


NO_RANK = 1e9
RANK_BASE = -2.0 ** 100


def _peer_score_kernel(h_ref, wq_ref, keys_ref, rk0_o, e0_o, th1_o, e1_o, s_scr, top_scr, rank_scr):
    tmp = h_ref.shape[0]
    q_t = lax.dot_general(wq_ref[...], h_ref[...], (((1,), (1,)), ((), ())),
                          preferred_element_type=F32).astype(BF16)
    for hp in range(2 * P_HEADS):
        s_scr[hp] = _dot(keys_ref[hp], q_t[hp * P_HALF:(hp + 1) * P_HALF, :])

    neg = jnp.float32(-jnp.inf)
    half = P_TOPK // 2
    row16 = lax.broadcasted_iota(jnp.int32, (P_TOPK, LANES), 0)
    row8 = lax.broadcasted_iota(jnp.int32, (half, LANES), 0)
    n_groups = tmp // LANES

    def top16_pair(sets):
        vals = [jnp.full((P_TOPK, LANES), neg, F32) for _ in sets]
        sets = list(sets)
        for it in range(P_TOPK):
            for n in range(len(sets)):
                m = jnp.max(sets[n], axis=0, keepdims=True)
                vals[n] = jnp.where(row16 == it, m, vals[n])
                sets[n] = jnp.where(sets[n] >= m, RANK_BASE * (1.0 + it / P_TOPK), sets[n])
        ranks = [jnp.where(s <= RANK_BASE, s * (P_TOPK / RANK_BASE) - P_TOPK, NO_RANK) for s in sets]
        return vals, ranks

    n_items = P_HEADS * n_groups

    def item(idx):
        return idx // n_groups, pl.ds(pl.multiple_of((idx % n_groups) * LANES, LANES), LANES)

    def extract(idx):
        h, ls = item(idx)
        (a16, b16), (rank0, rank1) = top16_pair((s_scr[2 * h, :, ls], s_scr[2 * h + 1, :, ls]))
        rk0_o[h, :, ls] = rank0
        top_scr[idx, 0] = a16
        top_scr[idx, 1] = b16
        rank_scr[idx] = rank1

    def combine(idx):
        h, ls = item(idx)
        a16 = top_scr[idx, 0]
        b16 = top_scr[idx, 1]
        rank1 = rank_scr[idx]
        s0 = s_scr[2 * h, :, ls]
        s1 = s_scr[2 * h + 1, :, ls]
        b_lo = b16[0:half, :]
        cands = [a16[0:1, :] + b16, a16[1:2, :] + b_lo]
        for r in range(2, half):
            cands.append(jnp.where(row8 < P_TOPK // (r + 1), a16[r:r + 1, :] + b_lo, neg))
        cands.append(a16[half:P_TOPK, :] + b16[0:1, :])
        cmax = a16[0:1, :] + b16[0:1, :]
        work = list(cands)
        tau = cmax
        for it in range(P_TOPK):
            m = jnp.maximum(work[0][0:half, :], work[0][half:P_TOPK, :])
            for c in work[1:]:
                m = jnp.maximum(m, c)
            tau = jnp.max(m, axis=0, keepdims=True)
            if it + 1 < P_TOPK:
                work = [jnp.where(c >= tau, neg, c) for c in work]
        z = jnp.zeros((1, LANES), F32)
        sels = []
        for c in cands:
            sel = c >= tau
            z = z + jnp.sum(jnp.where(sel, jnp.exp(c - cmax), 0.0), axis=0, keepdims=True)
            sels.append(jnp.where(sel, 1.0, 0.0))
        cnt_hi = sels[0][half:P_TOPK, :]
        cnt_lo = sels[0][0:half, :]
        for c in sels[1:half]:
            cnt_lo = cnt_lo + c
        cnt_lo = cnt_lo + jnp.where(row8 == 0, jnp.sum(sels[half], axis=0, keepdims=True), 0.0)
        th1 = jnp.zeros(rank1.shape, F32)
        for q in range(P_TOPK):
            cnt = cnt_lo[q:q + 1, :] if q < half else cnt_hi[q - half:q - half + 1, :]
            th1 = jnp.where(rank1 == float(q), cnt, th1)
        th1_o[h, :, ls] = th1.astype(BF16)
        e0_o[h, :, ls] = jnp.exp(s0 - a16[0:1, :]) / z
        e1_o[h, :, ls] = jnp.exp(s1 - b16[0:1, :]).astype(BF16)

    def extract_step(n, carry):
        extract(2 * n)
        extract(2 * n + 1)
        return carry

    def combine_step(n, carry):
        combine(2 * n)
        combine(2 * n + 1)
        return carry

    lax.fori_loop(0, n_items // 2, extract_step, 0)
    lax.fori_loop(0, n_items // 2, combine_step, 0)


def _peer_scores(h2, wq_t, keys):
    n_tok = h2.shape[0]
    tmp = TM_PEER
    big = pl.BlockSpec((P_HEADS, P_KEYS, tmp), lambda i: (0, 0, i))
    big_shape = jax.ShapeDtypeStruct((P_HEADS, P_KEYS, n_tok), F32)
    narrow_shape = jax.ShapeDtypeStruct((P_HEADS, P_KEYS, n_tok), BF16)
    return pl.pallas_call(
        _peer_score_kernel,
        grid=(n_tok // tmp,),
        in_specs=[pl.BlockSpec((tmp, D_MODEL), lambda i: (i, 0)),
                  pl.BlockSpec((2 * P_HEADS * P_HALF, D_MODEL), lambda i: (0, 0)),
                  pl.BlockSpec((2 * P_HEADS, P_KEYS, P_HALF), lambda i: (0, 0, 0))],
        out_specs=[big, big, big, big],
        out_shape=[big_shape, big_shape, narrow_shape, narrow_shape],
        scratch_shapes=[pltpu.VMEM((2 * P_HEADS, P_KEYS, tmp), F32),
                        pltpu.VMEM((P_HEADS * (tmp // LANES), 2, P_TOPK, LANES), F32),
                        pltpu.VMEM((P_HEADS * (tmp // LANES), P_KEYS, LANES), F32)],
        compiler_params=_cparams("parallel"),
        name="peer_scores",
    )(h2, wq_t, keys)


def _peer_expert_kernel(h_ref, u_ref, vt_ref, rk0_ref, e0_ref, th1_ref, e1_ref, x_ref,
                        mod_ref, *rest, final_norm):
    if final_norm:
        fg_ref, o_ref, acc_ref, gated_ref = rest
    else:
        o_ref, acc_ref, gated_ref = rest
    e = pl.program_id(1)
    tmp = h_ref.shape[0]

    @pl.when(e == 0)
    def _():
        acc_ref[...] = jnp.zeros_like(acc_ref)

    for il in range(PEER_EB // P_KEYS):
        rows = slice(il * P_KEYS, (il + 1) * P_KEYS)
        for lc in range(tmp // LANES):
            ls = slice(lc * LANES, (lc + 1) * LANES)
            w = None
            for h in range(P_HEADS):
                rk0 = jnp.broadcast_to(rk0_ref[h, pl.ds(il, 1), ls], (P_KEYS, LANES)).astype(BF16)
                e0 = jnp.broadcast_to(e0_ref[h, pl.ds(il, 1), ls], (P_KEYS, LANES)).astype(BF16)
                term = jnp.where(rk0 < th1_ref[h, :, ls], e0 * e1_ref[h, :, ls], jnp.zeros((), BF16))
                w = term if w is None else w + term
            gated_ref[rows, ls] = w
    hid = lax.dot_general(u_ref[...], h_ref[...], (((1,), (1,)), ((), ())),
                          preferred_element_type=F32)
    act = 0.5 * hid * (1.0 + lax.erf(hid * (2.0 ** -0.5)))
    acc_ref[...] += _dot(vt_ref[...], gated_ref[...] * act.astype(BF16))

    @pl.when(e == pl.num_programs(1) - 1)
    def _():
        gt2 = mod_ref[0, :, 5 * D_MODEL:6 * D_MODEL]
        xn = x_ref[...] + gt2 * acc_ref[...].T
        if final_norm:
            ms = jnp.mean(xn * xn, axis=-1, keepdims=True)
            xn = xn * lax.rsqrt(ms + EPS) * fg_ref[...]
        o_ref[...] = xn


def _peer_experts(h2, u_bf, vt_bf, rk0, e0, th1, e1, x, mod3, mod_map, final_g=None):
    n_tok = h2.shape[0]
    tmp = TM_PEER
    n_i = PEER_EB // P_KEYS
    sel = pl.BlockSpec((P_HEADS, n_i, tmp), lambda i, e: (0, e, i))
    full = pl.BlockSpec((P_HEADS, P_KEYS, tmp), lambda i, e: (0, 0, i))
    mod_map2 = lambda i, e: mod_map(i)
    tile = pl.BlockSpec((tmp, D_MODEL), lambda i, e: (i, 0))
    out = jax.ShapeDtypeStruct((n_tok, D_MODEL), F32)
    final_norm = final_g is not None
    extra_in = [pl.BlockSpec((1, D_MODEL), lambda i, e: (0, 0))] if final_norm else []
    extra_args = [final_g] if final_norm else []
    return pl.pallas_call(
        functools.partial(_peer_expert_kernel, final_norm=final_norm),
        grid=(n_tok // tmp, P_EXPERTS // PEER_EB),
        in_specs=[tile,
                  pl.BlockSpec((PEER_EB, D_MODEL), lambda i, e: (e, 0)),
                  pl.BlockSpec((D_MODEL, PEER_EB), lambda i, e: (0, e)),
                  sel, sel, full, full, tile,
                  pl.BlockSpec((1, 1, 6 * D_MODEL), mod_map2)] + extra_in,
        out_specs=tile,
        out_shape=out,
        scratch_shapes=[pltpu.VMEM((D_MODEL, tmp), F32), pltpu.VMEM((PEER_EB, tmp), BF16)],
        compiler_params=_cparams("parallel", "arbitrary"),
        name="peer_experts",
    )(h2, u_bf, vt_bf, rk0, e0, th1, e1, x, mod3, *extra_args)


def _block_diag2(w):
    z = jnp.zeros_like(w[0])
    return jnp.concatenate([jnp.concatenate([w[0], z], axis=1),
                            jnp.concatenate([z, w[1]], axis=1)], axis=0)


def kernel(x_prompt, x_sample, state_rwkv, c, c_ctx, ada_w, ada_b, norm1_g, norm2_g, w_in, shift_mu,
           decay_w0, decay_w2, iclr_a0, iclr_a2, gate_g2, k_k, k_a, r_k, gn_g, gn_b, w_a_out,
           pool_w, pool_b, pool_scale, w_b_out, conv_w, conv_b, cln_g, cln_b, w_c_out, w_out,
           peer_wq, peer_keys, peer_u, peer_v, final_g):
    d = D_MODEL
    batch, seq, _ = x_prompt.shape
    dec_batch, dec_seq, _ = x_sample.shape
    depth = ada_w.shape[0]
    n_ctx = batch * seq
    n_dec = dec_batch * dec_seq
    tm = seq
    assert n_ctx % dec_seq == 0 and dec_seq % tm == 0 and dec_seq % GRID_W == 0
    assert n_ctx % TM_PROJ == 0 and dec_seq % TM_PROJ == 0 and dec_seq % TM_PEER == 0
    assert n_ctx % TM_PEER == 0 and tm % GRID_W == 0 and dec_batch <= 7 and TM_PROJ % tm == 0

    x = jnp.concatenate([x_prompt.reshape(n_ctx, d), x_sample.reshape(n_dec, d)], axis=0)
    n_tok = n_ctx + n_dec

    cond8 = jnp.zeros((8, d), F32).at[0].set(c_ctx).at[1:1 + dec_batch].set(c)
    mod_all = _modulation(cond8, ada_w, ada_b)

    seg_ids = jnp.arange(d) // A_HEAD
    ones_bd = (seg_ids[:, None] == seg_ids[None, :]).astype(BF16)

    map_proj = _mod_row_map(n_ctx // TM_PROJ, dec_seq // TM_PROJ)
    map_tm = _mod_row_map(n_ctx // tm, dec_seq // tm)
    map_peer = _mod_row_map(n_ctx // TM_PEER, dec_seq // TM_PEER)

    n_tiles = n_tok // tm
    tile_idx = jnp.arange(n_tiles)
    dec_tile = (tile_idx - n_ctx // tm) % (dec_seq // tm)
    is_start = jnp.where(tile_idx < n_ctx // tm, True, dec_tile == 0)[:, None]
    is_end = jnp.where(tile_idx < n_ctx // tm, True, dec_tile == dec_seq // tm - 1)[:, None]

    ctx_states = []
    for l in range(depth):
        mod3 = mod_all[l][:, None, :]
        w_in_bf = w_in[l].astype(BF16)
        pa, edges = _inproj(x, mod3, norm1_g[l][None], w_in_bf[:, :A_COLS], (A_COLS,), map_proj,
                            "inproj_a", edge_tile=tm)
        pb, pc, pg = _inproj(x, mod3, norm1_g[l][None], w_in_bf[:, A_COLS:],
                             (B_WIDTH, 2 * C_WIDTH, 3 * d), map_proj, "inproj_bcg")

        zero_row = jnp.zeros((1, A_COLS), F32)
        halo_prev = jnp.concatenate([zero_row, edges[:-1, 1, :]], axis=0)
        halo_next = jnp.concatenate([edges[1:, 0, :], zero_row], axis=0)
        halo_prev = jnp.where(is_start, 0.0, halo_prev)[:, None, :]
        halo_next = jnp.where(is_end, 0.0, halo_next)[:, None, :]
        prep_params = (shift_mu[l][None], decay_w0[l].reshape(1, 2 * d),
                       _block_diag2(decay_w2[l]).astype(BF16), iclr_a0[l].reshape(1, 2 * d),
                       _block_diag2(iclr_a2[l]).astype(BF16), gate_g2[l].astype(BF16),
                       k_k[l][None], k_a[l][None], r_k[l].reshape(1, d), ones_bd)
        r, v, kk, dec, kd, bb, bv, g = _rwkv_prep(pa, halo_prev, halo_next, tm, prep_params)

        y_scan = None
        for seq_len, batch0, nb, s0 in ((seq, 0, batch, None),
                                        (dec_seq, n_ctx // dec_seq, dec_batch, state_rwkv[:, l])):
            vs = _lane_plan(nb)[1]
            if s0 is None:
                s0_c = jnp.zeros((2, A_HEAD, A_HEAD // vs, vs * nb * A_HEADS), F32)
            else:
                s0_c = _state_to_chain(s0, vs)
            lay = lambda a, value_rows=False: _to_chain(a, seq_len, batch0, nb, value_rows)
            y_fwd, y_bwd, sfin_c = _rwkv_scan(lay(kk)[0], lay(r)[0], lay(dec), lay(bb), lay(kd),
                                              lay(v, True)[0], s0_c)
            y_scan = _from_chain(y_fwd, y_bwd, y_scan, n_tok, seq_len, batch0, nb)
            if s0 is None:
                ctx_states.append(_state_from_chain(sfin_c, nb, vs))

        yb = _pool_mix(pb, n_ctx, seq, dec_seq, pool_w[l].astype(BF16), pool_b[l].reshape(1, B_WIDTH),
                       pool_scale[l][None])
        yc = _conv_mix(pc, tm, n_ctx // tm, conv_w[l], conv_b[l][None], cln_g[l][None], cln_b[l][None])

        x, h2 = _merge(y_scan, bv, g, yb, yc, pg, x, mod3, map_tm, tm, gn_g[l][None], gn_b[l][None],
                       norm2_g[l][None], w_a_out[l].astype(BF16), w_b_out[l].astype(BF16),
                       w_c_out[l].astype(BF16), w_out[l].astype(BF16), ones_bd)

        wq_t = peer_wq[l].T.astype(BF16)
        keys = peer_keys[l].reshape(2 * P_HEADS, P_KEYS, P_HALF).astype(BF16)
        rk0, e0p, th1, e1p = _peer_scores(h2, wq_t, keys)
        x = _peer_experts(h2, peer_u[l].astype(BF16), peer_v[l].T.astype(BF16), rk0, e0p, th1, e1p,
                          x, mod3, map_peer, final_g[None] if l == depth - 1 else None)

    y = x
    new_state = jnp.stack(ctx_states, axis=1)
    return (y[:n_ctx].reshape(batch, seq, d), y[n_ctx:].reshape(dec_batch, dec_seq, d), new_state)
```

```python
import functools

import jax
import jax.numpy as jnp
from jax import lax
from jax.experimental import pallas as pl
from jax.experimental.pallas import tpu as pltpu

F32 = jnp.float32
BF16 = jnp.bfloat16

D_MODEL = 1024
A_HEADS = 16
A_HEAD = 64
A_LORA = 384
A_COLS = 3 * D_MODEL + A_LORA
B_WIDTH = 512
B_GROUP = 128
POOL_WINDOWS = (2, 4, 8, 16)
POOL_PAD = 16
C_WIDTH = 512
C_CONV = 31
CONV_PAD = 16
GRID_W = 64
P_HEADS = 8
P_KEYS = 128
P_TOPK = 16
P_HALF = 128
P_EXPERTS = P_KEYS * P_KEYS
EPS = 1e-6
GN_EPS = 64e-5
LN_EPS = 1e-5

LANES = 128
VMEM_LIMIT = 56 * 1024 * 1024

TM_PROJ = 512
TM_PEER = 512
PEER_EB = 1024
SCAN_TB = 8


def _cparams(*sem):
    return pltpu.CompilerParams(dimension_semantics=sem, vmem_limit_bytes=VMEM_LIMIT)


def _sigmoid(x):
    return 1.0 / (1.0 + jnp.exp(-x))


def _dot(a, b):
    return jnp.dot(a, b, preferred_element_type=F32)


def _segsum(x, ones_bd):
    hi = x.astype(BF16)
    lo = (x - hi.astype(F32)).astype(BF16)
    return _dot(hi, ones_bd) + _dot(lo, ones_bd)


def _mod_kernel(c_ref, w_ref, b_ref, o_ref):
    c = c_ref[...]
    s = c * _sigmoid(c)
    o_ref[...] = jnp.dot(s, w_ref[...], preferred_element_type=F32,
                         precision=lax.Precision.HIGHEST) + b_ref[...]


def _modulation(cond8, ada_w, ada_b):
    L = ada_w.shape[0]
    nb = 6
    return pl.pallas_call(
        _mod_kernel,
        grid=(L, nb),
        in_specs=[pl.BlockSpec((8, D_MODEL), lambda l, j: (0, 0)),
                  pl.BlockSpec((None, D_MODEL, D_MODEL), lambda l, j: (l, 0, j)),
                  pl.BlockSpec((None, 1, D_MODEL), lambda l, j: (l, 0, j))],
        out_specs=pl.BlockSpec((None, 8, D_MODEL), lambda l, j: (l, 0, j)),
        out_shape=jax.ShapeDtypeStruct((L, 8, 6 * D_MODEL), F32),
        compiler_params=_cparams("parallel", "parallel"),
        name="adaln_mod",
    )(cond8, ada_w, ada_b.reshape(L, 1, 6 * D_MODEL))


def _mod_row_map(n_ctx_tiles, tiles_per_dec_seq):
    def index_map(i):
        row = jnp.where(i < n_ctx_tiles, 0, 1 + (i - n_ctx_tiles) // tiles_per_dec_seq)
        return (row, 0, 0)
    return index_map


def _inproj_kernel(x_ref, mod_ref, g_ref, w_ref, *o_refs, edge_tile):
    x = x_ref[...]
    ms = jnp.mean(x * x, axis=-1, keepdims=True)
    y = x * lax.rsqrt(ms + EPS) * g_ref[...]
    sh = mod_ref[0, :, 0:D_MODEL]
    sc = mod_ref[0, :, D_MODEL:2 * D_MODEL]
    h = (y * (1.0 + sc) + sh).astype(BF16)
    if edge_tile:
        *o_refs, edge_ref = o_refs
    off = 0
    for o_ref in o_refs:
        n = o_ref.shape[1]
        o_ref[...] = _dot(h, w_ref[:, off:off + n])
        off += n
    if edge_tile:
        first = o_refs[0]
        for t in range(x.shape[0] // edge_tile):
            edge_ref[t, 0:1, :] = first[t * edge_tile:t * edge_tile + 1, :]
            edge_ref[t, 1:2, :] = first[(t + 1) * edge_tile - 1:(t + 1) * edge_tile, :]


def _inproj(x, mod3, g, w, splits, mod_map, name, edge_tile=0):
    n_tok = x.shape[0]
    n_out = w.shape[1]
    out_specs = [pl.BlockSpec((TM_PROJ, n), lambda i: (i, 0)) for n in splits]
    out_shape = [jax.ShapeDtypeStruct((n_tok, n), F32) for n in splits]
    if edge_tile:
        per = TM_PROJ // edge_tile
        out_specs.append(pl.BlockSpec((per, 2, splits[0]), lambda i: (i, 0, 0)))
        out_shape.append(jax.ShapeDtypeStruct((n_tok // edge_tile, 2, splits[0]), F32))
    return pl.pallas_call(
        functools.partial(_inproj_kernel, edge_tile=edge_tile),
        grid=(n_tok // TM_PROJ,),
        in_specs=[pl.BlockSpec((TM_PROJ, D_MODEL), lambda i: (i, 0)),
                  pl.BlockSpec((1, 1, 6 * D_MODEL), mod_map),
                  pl.BlockSpec((1, D_MODEL), lambda i: (0, 0)),
                  pl.BlockSpec((D_MODEL, n_out), lambda i: (0, 0))],
        out_specs=out_specs,
        out_shape=out_shape,
        compiler_params=_cparams("parallel"),
        name=name,
    )(x, mod3, g, w)


def _prep_kernel(pa_ref, hp_ref, hn_ref, mu_ref, w0_ref, w2_ref, a0_ref, a2_ref, g2_ref,
                 kkw_ref, ka_ref, rk_ref, ones_ref,
                 r_o, v_o, kk_o, dec_o, kd_o, bb_o, bv_o, g_o):
    tm = pa_ref.shape[0]

    def shifted(c0, c1):
        x = pa_ref[:, c0:c1]
        row = lax.broadcasted_iota(jnp.int32, x.shape, 0)
        prev = jnp.where(row == 0, hp_ref[0, :, c0:c1], pltpu.roll(x, 1, 0))
        nxt = jnp.where(row == tm - 1, hn_ref[0, :, c0:c1], pltpu.roll(x, tm - 1, 0))
        return x + (0.5 * (prev + nxt) - x) * mu_ref[:, c0:c1]

    d = D_MODEL
    r = shifted(0, d)
    k = shifted(d, 2 * d)
    v = shifted(2 * d, 3 * d)
    lo = shifted(3 * d, 3 * d + A_LORA)
    w_lo = lo[:, 0:128]
    a_lo = lo[:, 128:256]
    g_lo = lo[:, 256:384]
    ones_bd = ones_ref[...]

    r_o[...] = r
    v_o[...] = v

    kkraw = k * kkw_ref[...]
    nrm = jnp.sqrt(_segsum(kkraw * kkraw, ones_bd))
    kk = kkraw / jnp.maximum(nrm, 1e-12)
    kk_o[...] = kk

    z = -(w0_ref[...] + _dot(jnp.tanh(w_lo).astype(BF16), w2_ref[...]))
    softplus = jnp.maximum(z, 0.0) + jnp.log(1.0 + jnp.exp(-jnp.abs(z)))
    dec_o[...] = jnp.exp(-jnp.exp(-softplus - 0.5))

    a = _sigmoid(a0_ref[...] + _dot(a_lo.astype(BF16), a2_ref[...]))
    ka = ka_ref[...]
    kd0 = k * (1.0 + (a[:, 0:d] - 1.0) * ka)
    kd1 = k * (1.0 + (a[:, d:2 * d] - 1.0) * ka)
    kd_o[:, 0:d] = kd0
    kd_o[:, d:2 * d] = kd1
    bb_o[:, 0:d] = kk * a[:, 0:d]
    bb_o[:, d:2 * d] = kk * a[:, d:2 * d]

    bonus = _segsum(r * (kd0 + kd1) * rk_ref[...], ones_bd)
    bv_o[...] = bonus * v
    g_o[...] = _dot(_sigmoid(g_lo).astype(BF16), g2_ref[...])


def _rwkv_prep(pa, halo_prev, halo_next, tm, params):
    n_tok = pa.shape[0]
    d = D_MODEL
    row = lambda n: pl.BlockSpec((1, n), lambda i: (0, 0))
    full = lambda a, b: pl.BlockSpec((a, b), lambda i: (0, 0))
    tile = lambda n: pl.BlockSpec((tm, n), lambda i: (i, 0))
    halo = pl.BlockSpec((1, 1, A_COLS), lambda i: (i, 0, 0))
    outs = [d, d, d, 2 * d, 2 * d, 2 * d, d, d]
    return pl.pallas_call(
        _prep_kernel,
        grid=(n_tok // tm,),
        in_specs=[tile(A_COLS), halo, halo, row(A_COLS), row(2 * d), full(128, 2 * d), row(2 * d),
                  full(128, 2 * d), full(128, d), row(d), row(d), row(d), full(d, d)],
        out_specs=[tile(n) for n in outs],
        out_shape=[jax.ShapeDtypeStruct((n_tok, n), F32) for n in outs],
        compiler_params=_cparams("parallel"),
        name="rwkv_prep",
    )(pa, halo_prev, halo_next, *params)


SCAN_ROWS = 16
SCAN_STATE_PAD = 8


def _scan_kernel(kkf, kkb, rf, rb, wf, wb, bf, bb, kdf, kdb, vf, vb, s0_ref,
                 yf_ref, yb_ref, sfin_ref, s_scr):
    nk, tb = kkf.shape[0], kkf.shape[1]
    j = pl.program_id(2)
    live = slice(0, vf.shape[1])

    @pl.when(j == 0)
    def _():
        s_scr[:, :, live, :] = s0_ref[...]

    def advance(d, t, rs, kk_ref, r_ref, w_ref, b_ref, kd_ref, v_ref, y_ref):
        parts = [None] * 4
        for k in range(nk):
            term = s_scr[d, k, live, :] * kk_ref[k, pl.ds(t, 1), :]
            parts[k % 4] = term if parts[k % 4] is None else parts[k % 4] + term
        sa = -((parts[0] + parts[1]) + (parts[2] + parts[3]))
        vt = v_ref[t, rs, :]
        ys = [None] * 2
        for k in range(nk):
            s_new = (s_scr[d, k, live, :] * w_ref[k, pl.ds(t, 1), :] + sa * b_ref[k, pl.ds(t, 1), :]
                     + vt * kd_ref[k, pl.ds(t, 1), :])
            s_scr[d, k, live, :] = s_new
            term = s_new * r_ref[k, pl.ds(t, 1), :]
            ys[k % 2] = term if ys[k % 2] is None else ys[k % 2] + term
        y_ref[t, rs, :] = ys[0] + ys[1]

    rs = slice(None)

    def step(i, carry):
        advance(0, i, rs, kkf, rf, wf, bf, kdf, vf, yf_ref)
        advance(1, tb - 1 - i, rs, kkb, rb, wb, bb, kdb, vb, yb_ref)
        return carry

    lax.fori_loop(0, tb, step, 0)

    @pl.when(j == pl.num_programs(2) - 1)
    def _():
        sfin_ref[...] = s_scr[:, :, live, :]


def _rwkv_scan(kk_c, r_c, w_c, b_c, kd_c, v_c, s0_c):
    nk, T, L = kk_c.shape
    vq = v_c.shape[1]
    tb = min(SCAN_TB, T)
    nt = T // tb
    fwd = lambda j: j
    bwd = lambda j: nt - 1 - j
    rows = min(vq, SCAN_ROWS)
    shared = lambda tj: pl.BlockSpec((nk, tb, LANES), lambda g, q, j: (0, tj(j), g))
    perdir = lambda d, tj: pl.BlockSpec((None, nk, tb, LANES), lambda g, q, j: (d, 0, tj(j), g))
    vspec = lambda tj: pl.BlockSpec((tb, rows, LANES), lambda g, q, j: (tj(j), q, g))
    sspec = pl.BlockSpec((2, nk, rows, LANES), lambda g, q, j: (0, 0, q, g))
    y_shape = jax.ShapeDtypeStruct((T, vq, L), F32)
    return pl.pallas_call(
        _scan_kernel,
        grid=(L // LANES, vq // rows, nt),
        in_specs=[shared(fwd), shared(bwd), shared(fwd), shared(bwd),
                  perdir(0, fwd), perdir(1, bwd), perdir(0, fwd), perdir(1, bwd),
                  perdir(0, fwd), perdir(1, bwd), vspec(fwd), vspec(bwd), sspec],
        out_specs=[vspec(fwd), vspec(bwd), sspec],
        out_shape=[y_shape, y_shape, jax.ShapeDtypeStruct((2, nk, vq, L), F32)],
        scratch_shapes=[pltpu.VMEM((2, nk, rows + SCAN_STATE_PAD, LANES), F32)],
        compiler_params=_cparams("parallel", "parallel", "arbitrary"),
        name="rwkv_scan",
    )(kk_c, kk_c, r_c, r_c, w_c, w_c, b_c, b_c, kd_c, kd_c, v_c, v_c, s0_c)


def _lane_plan(n_batch):
    full = LANES // A_HEADS
    if n_batch >= full:
        assert n_batch % full == 0
        return full, 1
    assert full % n_batch == 0
    return n_batch, full // n_batch


SLAB_PITCH = A_HEAD + 4


def _to_chain_kernel(x_ref, o_ref, scr, *, vs, value_rows):
    nb, tc = x_ref.shape[0], x_ref.shape[1]
    n_slabs = nb * A_HEADS
    for b in range(nb):
        xt = x_ref[b].T
        for h in range(A_HEADS):
            scr[pl.ds((b * A_HEADS + h) * SLAB_PITCH, A_HEAD), :] = xt[h * A_HEAD:(h + 1) * A_HEAD]
    row_of_slabs = lambda c: scr[pl.ds(c, n_slabs, stride=SLAB_PITCH), :]
    vq = A_HEAD // vs
    if value_rows:
        for v in range(vq):
            rows = [row_of_slabs(vh * vq + v) for vh in range(vs)]
            rows = rows[0] if vs == 1 else jnp.concatenate(rows, axis=0)
            o_ref[:, v, :] = rows.T
    else:
        for k in range(A_HEAD):
            rows = row_of_slabs(k)
            if vs > 1:
                rows = jnp.concatenate([rows] * vs, axis=0)
            o_ref[k] = rows.T


def _to_chain(x, seq_len, batch0, n_batch, value_rows):
    n_col = x.shape[1] // D_MODEL
    nb, vs = _lane_plan(n_batch)
    assert batch0 % nb == 0 and seq_len % LANES == 0
    groups = n_batch // nb
    tc = LANES
    x3 = x.reshape(x.shape[0] // seq_len, seq_len, n_col * D_MODEL)
    vq = A_HEAD // vs
    if value_rows:
        out_spec = pl.BlockSpec((None, tc, vq, LANES), lambda c, g, j: (c, j, 0, g))
        out_shape = jax.ShapeDtypeStruct((n_col, seq_len, vq, groups * LANES), F32)
    else:
        out_spec = pl.BlockSpec((None, A_HEAD, tc, LANES), lambda c, g, j: (c, 0, j, g))
        out_shape = jax.ShapeDtypeStruct((n_col, A_HEAD, seq_len, groups * LANES), F32)
    return pl.pallas_call(
        functools.partial(_to_chain_kernel, vs=vs, value_rows=value_rows),
        grid=(n_col, groups, seq_len // tc),
        in_specs=[pl.BlockSpec((nb, tc, D_MODEL), lambda c, g, j: (batch0 // nb + g, j, c))],
        out_specs=out_spec,
        out_shape=out_shape,
        scratch_shapes=[pltpu.VMEM((nb * A_HEADS * SLAB_PITCH, tc), F32)],
        compiler_params=_cparams("parallel", "parallel", "parallel"),
        name="to_chain_v" if value_rows else "to_chain_k",
    )(x3)


def _from_chain_kernel(*refs, vs):
    yf_ref, yb_ref, o_ref, scr = refs[-4:]
    nb, tc = o_ref.shape[0], o_ref.shape[1]
    vq = A_HEAD // vs
    rows = nb * A_HEADS
    for v in range(vq):
        yv = (yf_ref[:, v, :] + yb_ref[:, v, :]).T
        for vh in range(vs):
            scr[pl.ds(vh * vq + v, rows, stride=SLAB_PITCH), :] = yv[vh * rows:(vh + 1) * rows]
    for b in range(nb):
        slabs = [scr[pl.ds((b * A_HEADS + h) * SLAB_PITCH, A_HEAD), :] for h in range(A_HEADS)]
        o_ref[b] = jnp.concatenate(slabs, axis=0).T


def _from_chain(y_fwd, y_bwd, prev_rows, n_tok, seq_len, batch0, n_batch):
    nb, vs = _lane_plan(n_batch)
    groups = n_batch // nb
    tc = LANES
    vq = A_HEAD // vs
    in_specs = [pl.BlockSpec((tc, vq, LANES), lambda g, j: (j, 0, g))] * 2
    args = [y_fwd, y_bwd]
    aliases = {}
    if prev_rows is not None:
        in_specs = [pl.BlockSpec(memory_space=pl.ANY)] + in_specs
        args = [prev_rows.reshape(n_tok // seq_len, seq_len, D_MODEL)] + args
        aliases = {0: 0}
    out = pl.pallas_call(
        functools.partial(_from_chain_kernel, vs=vs),
        grid=(groups, seq_len // tc),
        in_specs=in_specs,
        out_specs=pl.BlockSpec((nb, tc, D_MODEL), lambda g, j: (batch0 // nb + g, j, 0)),
        out_shape=jax.ShapeDtypeStruct((n_tok // seq_len, seq_len, D_MODEL), F32),
        scratch_shapes=[pltpu.VMEM((nb * A_HEADS * SLAB_PITCH, tc), F32)],
        input_output_aliases=aliases,
        compiler_params=_cparams("parallel", "parallel"),
        name="from_chain",
    )(*args)
    return out.reshape(n_tok, D_MODEL)


def _state_to_chain(s, vs):
    B = s.shape[0]
    vq = A_HEAD // vs
    s = jnp.transpose(s, (1, 4, 3, 0, 2)).reshape(2, A_HEAD, vs, vq, B * A_HEADS)
    return jnp.transpose(s, (0, 1, 3, 2, 4)).reshape(2, A_HEAD, vq, vs * B * A_HEADS)


def _state_from_chain(s_c, n_batch, vs):
    vq = A_HEAD // vs
    s = s_c.reshape(2, A_HEAD, vq, vs, n_batch, A_HEADS)
    return jnp.transpose(s, (4, 0, 5, 3, 2, 1)).reshape(n_batch, 2, A_HEADS, A_HEAD, A_HEAD)


def _pool_tail(d2, g, pw_ref, pb_ref, ps_ref):
    sl = slice(g * B_GROUP, (g + 1) * B_GROUP)
    y = _dot(d2.astype(BF16), pw_ref[g]) + pb_ref[:, sl]
    return y * ps_ref[:, sl]


def _window_count(t, w, n):
    lo = jnp.clip(t - w // 2, 0, n)
    hi = jnp.clip(t - w // 2 + w, 0, n)
    return (hi - lo).astype(F32)


def _pool_seq_kernel(z_ref, pw_ref, pb_ref, ps_ref, o_ref):
    n = z_ref.shape[0]
    pad = jnp.zeros((POOL_PAD, B_GROUP), F32)
    t = lax.broadcasted_iota(jnp.int32, (n, B_GROUP), 0)
    for g, w in enumerate(POOL_WINDOWS):
        sl = slice(g * B_GROUP, (g + 1) * B_GROUP)
        z = z_ref[:, sl]
        zp = jnp.concatenate([pad, z, pad], axis=0)
        acc = None
        for j in range(w):
            off = j - w // 2
            term = zp if off == 0 else pltpu.roll(zp, (-off) % (n + 2 * POOL_PAD), 0)
            acc = term if acc is None else acc + term
        d = acc[POOL_PAD:POOL_PAD + n] / _window_count(t, w, n) - z
        o_ref[:, sl] = _pool_tail(d, g, pw_ref, pb_ref, ps_ref)


def _pool_grid_kernel(z_ref, pw_ref, pb_ref, ps_ref, _, o_ref):
    rows, cb = z_ref.shape[0], z_ref.shape[1]
    pad = jnp.zeros((POOL_PAD, cb, B_GROUP), F32)
    t = lax.broadcasted_iota(jnp.int32, (rows, cb, B_GROUP), 0)
    for g, w in enumerate(POOL_WINDOWS):
        sl = slice(g * B_GROUP, (g + 1) * B_GROUP)
        z = z_ref[:, :, sl]
        zp = jnp.concatenate([pad, z, pad], axis=0)
        acc = None
        for j in range(w):
            s = POOL_PAD + j - w // 2
            term = zp[s:s + rows]
            acc = term if acc is None else acc + term
        d = acc / _window_count(t, w, rows) - z
        y = _pool_tail(d.reshape(rows * cb, B_GROUP), g, pw_ref, pb_ref, ps_ref)
        o_ref[:, :, sl] = y.reshape(rows, cb, B_GROUP)


def _pool_mix(pb, n_ctx, seq, dec_seq, pw, pbias, pscale):
    n_tok = pb.shape[0]
    wspecs = lambda nd: [pl.BlockSpec((4, B_GROUP, B_GROUP), lambda *a: (0, 0, 0)),
                         pl.BlockSpec((1, B_WIDTH), lambda *a: (0, 0)),
                         pl.BlockSpec((1, B_WIDTH), lambda *a: (0, 0))]
    y_ctx = pl.pallas_call(
        _pool_seq_kernel,
        grid=(n_ctx // seq,),
        in_specs=[pl.BlockSpec((seq, B_WIDTH), lambda i: (i, 0))] + wspecs(1),
        out_specs=pl.BlockSpec((seq, B_WIDTH), lambda i: (i, 0)),
        out_shape=jax.ShapeDtypeStruct((n_tok, B_WIDTH), F32),
        compiler_params=_cparams("parallel"),
        name="pool_seq",
    )(pb, pw, pbias, pscale)
    rows = dec_seq // GRID_W
    n_dec = (n_tok - n_ctx) // dec_seq
    ctx_blocks = n_ctx // dec_seq
    cb = 8
    shape4 = (n_tok // dec_seq, rows, GRID_W, B_WIDTH)
    blk = pl.BlockSpec((None, rows, cb, B_WIDTH), lambda b, c: (b + ctx_blocks, 0, c, 0))
    y_all = pl.pallas_call(
        _pool_grid_kernel,
        grid=(n_dec, GRID_W // cb),
        in_specs=[blk] + wspecs(2) + [pl.BlockSpec(memory_space=pl.ANY)],
        out_specs=blk,
        out_shape=jax.ShapeDtypeStruct(shape4, F32),
        input_output_aliases={4: 0},
        compiler_params=_cparams("parallel", "parallel"),
        name="pool_grid",
    )(pb.reshape(shape4), pw, pbias, pscale, y_ctx.reshape(shape4))
    return y_all.reshape(n_tok, B_WIDTH)


def _conv_kernel(pc_ref, cw_ref, cb_ref, lg_ref, lb_ref, o_ref, *, n_ctx_tiles):
    tm = pc_ref.shape[0]
    u = pc_ref[:, 0:C_WIDTH] * _sigmoid(pc_ref[:, C_WIDTH:2 * C_WIDTH])
    pad = jnp.zeros((CONV_PAD, C_WIDTH), F32)
    up = jnp.concatenate([pad, u, pad], axis=0)
    is_ctx = pl.program_id(0) < n_ctx_tiles
    t = lax.broadcasted_iota(jnp.int32, (tm, C_WIDTH), 0)
    pos = jnp.where(is_ctx, t, t & (GRID_W - 1))
    seg = jnp.where(is_ctx, tm, GRID_W)
    acc = jnp.zeros((tm, C_WIDTH), F32)
    for j in range(C_CONV):
        off = j - C_CONV // 2
        rolled = up if off == 0 else pltpu.roll(up, (-off) % (tm + 2 * CONV_PAD), 0)
        tap = rolled[CONV_PAD:CONV_PAD + tm]
        if off > 0:
            tap = jnp.where(pos < seg - off, tap, 0.0)
        elif off < 0:
            tap = jnp.where(pos >= -off, tap, 0.0)
        acc = acc + tap * cw_ref[pl.ds(j, 1), :]
    c = acc + cb_ref[...]
    mu = jnp.mean(c, axis=-1, keepdims=True)
    cc = c - mu
    var = jnp.mean(cc * cc, axis=-1, keepdims=True)
    y = cc * lax.rsqrt(var + LN_EPS) * lg_ref[...] + lb_ref[...]
    o_ref[...] = y * _sigmoid(y)


def _conv_mix(pc, tm, n_ctx_tiles, cw, cb, lg, lb):
    n_tok = pc.shape[0]
    row = pl.BlockSpec((1, C_WIDTH), lambda i: (0, 0))
    return pl.pallas_call(
        functools.partial(_conv_kernel, n_ctx_tiles=n_ctx_tiles),
        grid=(n_tok // tm,),
        in_specs=[pl.BlockSpec((tm, 2 * C_WIDTH), lambda i: (i, 0)),
                  pl.BlockSpec((C_CONV, C_WIDTH), lambda i: (0, 0)), row, row, row],
        out_specs=pl.BlockSpec((tm, C_WIDTH), lambda i: (i, 0)),
        out_shape=jax.ShapeDtypeStruct((n_tok, C_WIDTH), F32),
        compiler_params=_cparams("parallel"),
        name="conv_mix",
    )(pc, cw, cb, lg, lb)


def _merge_kernel(y_ref, bv_ref, g_ref, yb_ref, yc_ref, pg_ref, x_ref, mod_ref, gng_ref, gnb_ref,
                  n2g_ref, wa_ref, wb_ref, wc_ref, wo_ref, ones_ref, xo_ref, h2_ref):
    d = D_MODEL
    ones_bd = ones_ref[...]
    y = y_ref[...]
    mean = _segsum(y, ones_bd) * (1.0 / A_HEAD)
    yc = y - mean
    var = _segsum(yc * yc, ones_bd) * (1.0 / A_HEAD)
    yn = yc * lax.rsqrt(var + GN_EPS) * gng_ref[...] + gnb_ref[...]
    ya = (yn + bv_ref[...]) * g_ref[...]
    merged = (_sigmoid(pg_ref[:, 0:d]) * _dot(ya.astype(BF16), wa_ref[...])
              + _sigmoid(pg_ref[:, d:2 * d]) * _dot(yb_ref[...].astype(BF16), wb_ref[...])
              + _sigmoid(pg_ref[:, 2 * d:3 * d]) * _dot(yc_ref[...].astype(BF16), wc_ref[...]))
    gt1 = mod_ref[0, :, 2 * d:3 * d]
    xn = x_ref[...] + gt1 * _dot(merged.astype(BF16), wo_ref[...])
    xo_ref[...] = xn
    ms = jnp.mean(xn * xn, axis=-1, keepdims=True)
    h2 = xn * lax.rsqrt(ms + EPS) * n2g_ref[...]
    sh2 = mod_ref[0, :, 3 * d:4 * d]
    sc2 = mod_ref[0, :, 4 * d:5 * d]
    h2_ref[...] = (h2 * (1.0 + sc2) + sh2).astype(BF16)


def _merge(y, bv, g, yb, yc, pg, x, mod3, mod_map, tm, gng, gnb, n2g, wa, wb, wc, wo, ones_bd):
    n_tok = x.shape[0]
    d = D_MODEL
    tile = lambda n: pl.BlockSpec((tm, n), lambda i: (i, 0))
    row = pl.BlockSpec((1, d), lambda i: (0, 0))
    full = lambda a: pl.BlockSpec((a, d), lambda i: (0, 0))
    return pl.pallas_call(
        _merge_kernel,
        grid=(n_tok // tm,),
        in_specs=[tile(d), tile(d), tile(d), tile(B_WIDTH), tile(C_WIDTH), tile(3 * d), tile(d),
                  pl.BlockSpec((1, 1, 6 * d), mod_map), row, row, row,
                  full(d), full(B_WIDTH), full(C_WIDTH), full(d), full(d)],
        out_specs=[tile(d), tile(d)],
        out_shape=[jax.ShapeDtypeStruct((n_tok, d), F32), jax.ShapeDtypeStruct((n_tok, d), BF16)],
        compiler_params=_cparams("parallel"),
        name="merge_outproj",
    )(y, bv, g, yb, yc, pg, x, mod3, gng, gnb, n2g, wa, wb, wc, wo, ones_bd)


NO_RANK = 1e9
RANK_BASE = -2.0 ** 100


def _peer_score_kernel(h_ref, wq_ref, keys_ref, rk0_o, e0_o, th1_o, e1_o, s_scr, top_scr, rank_scr):
    tmp = h_ref.shape[0]
    q_t = lax.dot_general(wq_ref[...], h_ref[...], (((1,), (1,)), ((), ())),
                          preferred_element_type=F32).astype(BF16)
    for hp in range(2 * P_HEADS):
        s_scr[hp] = _dot(keys_ref[hp], q_t[hp * P_HALF:(hp + 1) * P_HALF, :])

    neg = jnp.float32(-jnp.inf)
    half = P_TOPK // 2
    row16 = lax.broadcasted_iota(jnp.int32, (P_TOPK, LANES), 0)
    row8 = lax.broadcasted_iota(jnp.int32, (half, LANES), 0)
    n_groups = tmp // LANES

    def top16_pair(sets):
        vals = [jnp.full((P_TOPK, LANES), neg, F32) for _ in sets]
        sets = list(sets)
        for it in range(P_TOPK):
            for n in range(len(sets)):
                m = jnp.max(sets[n], axis=0, keepdims=True)
                vals[n] = jnp.where(row16 == it, m, vals[n])
                sets[n] = jnp.where(sets[n] >= m, RANK_BASE * (1.0 + it / P_TOPK), sets[n])
        ranks = [jnp.where(s <= RANK_BASE, s * (P_TOPK / RANK_BASE) - P_TOPK, NO_RANK) for s in sets]
        return vals, ranks

    n_items = P_HEADS * n_groups

    def item(idx):
        return idx // n_groups, pl.ds(pl.multiple_of((idx % n_groups) * LANES, LANES), LANES)

    def extract(idx):
        h, ls = item(idx)
        (a16, b16), (rank0, rank1) = top16_pair((s_scr[2 * h, :, ls], s_scr[2 * h + 1, :, ls]))
        rk0_o[h, :, ls] = rank0
        top_scr[idx, 0] = a16
        top_scr[idx, 1] = b16
        rank_scr[idx] = rank1

    def combine(idx):
        h, ls = item(idx)
        a16 = top_scr[idx, 0]
        b16 = top_scr[idx, 1]
        rank1 = rank_scr[idx]
        s0 = s_scr[2 * h, :, ls]
        s1 = s_scr[2 * h + 1, :, ls]
        b_lo = b16[0:half, :]
        cands = [a16[0:1, :] + b16, a16[1:2, :] + b_lo]
        for r in range(2, half):
            cands.append(jnp.where(row8 < P_TOPK // (r + 1), a16[r:r + 1, :] + b_lo, neg))
        cands.append(a16[half:P_TOPK, :] + b16[0:1, :])
        cmax = a16[0:1, :] + b16[0:1, :]
        work = list(cands)
        tau = cmax
        for it in range(P_TOPK):
            m = jnp.maximum(work[0][0:half, :], work[0][half:P_TOPK, :])
            for c in work[1:]:
                m = jnp.maximum(m, c)
            tau = jnp.max(m, axis=0, keepdims=True)
            if it + 1 < P_TOPK:
                work = [jnp.where(c >= tau, neg, c) for c in work]
        z = jnp.zeros((1, LANES), F32)
        sels = []
        for c in cands:
            sel = c >= tau
            z = z + jnp.sum(jnp.where(sel, jnp.exp(c - cmax), 0.0), axis=0, keepdims=True)
            sels.append(jnp.where(sel, 1.0, 0.0))
        cnt_hi = sels[0][half:P_TOPK, :]
        cnt_lo = sels[0][0:half, :]
        for c in sels[1:half]:
            cnt_lo = cnt_lo + c
        cnt_lo = cnt_lo + jnp.where(row8 == 0, jnp.sum(sels[half], axis=0, keepdims=True), 0.0)
        th1 = jnp.zeros(rank1.shape, F32)
        for q in range(P_TOPK):
            cnt = cnt_lo[q:q + 1, :] if q < half else cnt_hi[q - half:q - half + 1, :]
            th1 = jnp.where(rank1 == float(q), cnt, th1)
        th1_o[h, :, ls] = th1.astype(BF16)
        e0_o[h, :, ls] = jnp.exp(s0 - a16[0:1, :]) / z
        e1_o[h, :, ls] = jnp.exp(s1 - b16[0:1, :]).astype(BF16)

    def extract_step(n, carry):
        extract(2 * n)
        extract(2 * n + 1)
        return carry

    def combine_step(n, carry):
        combine(2 * n)
        combine(2 * n + 1)
        return carry

    lax.fori_loop(0, n_items // 2, extract_step, 0)
    lax.fori_loop(0, n_items // 2, combine_step, 0)


def _peer_scores(h2, wq_t, keys):
    n_tok = h2.shape[0]
    tmp = TM_PEER
    big = pl.BlockSpec((P_HEADS, P_KEYS, tmp), lambda i: (0, 0, i))
    big_shape = jax.ShapeDtypeStruct((P_HEADS, P_KEYS, n_tok), F32)
    narrow_shape = jax.ShapeDtypeStruct((P_HEADS, P_KEYS, n_tok), BF16)
    return pl.pallas_call(
        _peer_score_kernel,
        grid=(n_tok // tmp,),
        in_specs=[pl.BlockSpec((tmp, D_MODEL), lambda i: (i, 0)),
                  pl.BlockSpec((2 * P_HEADS * P_HALF, D_MODEL), lambda i: (0, 0)),
                  pl.BlockSpec((2 * P_HEADS, P_KEYS, P_HALF), lambda i: (0, 0, 0))],
        out_specs=[big, big, big, big],
        out_shape=[big_shape, big_shape, narrow_shape, narrow_shape],
        scratch_shapes=[pltpu.VMEM((2 * P_HEADS, P_KEYS, tmp), F32),
                        pltpu.VMEM((P_HEADS * (tmp // LANES), 2, P_TOPK, LANES), F32),
                        pltpu.VMEM((P_HEADS * (tmp // LANES), P_KEYS, LANES), F32)],
        compiler_params=_cparams("parallel"),
        name="peer_scores",
    )(h2, wq_t, keys)


def _peer_expert_kernel(h_ref, u_ref, vt_ref, rk0_ref, e0_ref, th1_ref, e1_ref, x_ref,
                        mod_ref, *rest, final_norm):
    if final_norm:
        fg_ref, o_ref, acc_ref, gat_a, gat_b = rest
    else:
        o_ref, acc_ref, gat_a, gat_b = rest
    e = pl.program_id(1)
    tmp = h_ref.shape[0]

    @pl.when(e == 0)
    def _():
        acc_ref[...] = jnp.zeros_like(acc_ref)
        gat_b[...] = jnp.zeros_like(gat_b)

    def stage(gated_ref, gated_prev):
        acc_ref[...] += _dot(vt_ref[...], gated_prev[...])
        for il in range(PEER_EB // P_KEYS):
            rows = slice(il * P_KEYS, (il + 1) * P_KEYS)
            for lc in range(tmp // LANES):
                ls = slice(lc * LANES, (lc + 1) * LANES)
                w = None
                for h in range(P_HEADS):
                    rk0 = jnp.broadcast_to(rk0_ref[h, pl.ds(il, 1), ls], (P_KEYS, LANES)).astype(BF16)
                    e0 = jnp.broadcast_to(e0_ref[h, pl.ds(il, 1), ls], (P_KEYS, LANES)).astype(BF16)
                    term = jnp.where(rk0 < th1_ref[h, :, ls], e0 * e1_ref[h, :, ls],
                                     jnp.zeros((), BF16))
                    w = term if w is None else w + term
                gated_ref[rows, ls] = w
        hid = lax.dot_general(u_ref[...], h_ref[...], (((1,), (1,)), ((), ())),
                              preferred_element_type=F32)
        act = 0.5 * hid * (1.0 + lax.erf(hid * (2.0 ** -0.5)))
        gated_ref[...] = gated_ref[...] * act.astype(BF16)

    @pl.when(e % 2 == 0)
    def _():
        stage(gat_a, gat_b)

    @pl.when(e % 2 == 1)
    def _():
        stage(gat_b, gat_a)

    @pl.when(e == pl.num_programs(1) - 1)
    def _():
        gt2 = mod_ref[0, :, 5 * D_MODEL:6 * D_MODEL]
        xn = x_ref[...] + gt2 * acc_ref[...].T
        if final_norm:
            ms = jnp.mean(xn * xn, axis=-1, keepdims=True)
            xn = xn * lax.rsqrt(ms + EPS) * fg_ref[...]
        o_ref[...] = xn


def _peer_experts(h2, u_bf, vt_bf, rk0, e0, th1, e1, x, mod3, mod_map, final_g=None):
    n_tok = h2.shape[0]
    tmp = TM_PEER
    n_i = PEER_EB // P_KEYS
    last = P_EXPERTS // PEER_EB - 1
    cur = lambda e: jnp.minimum(e, last)
    sel = pl.BlockSpec((P_HEADS, n_i, tmp), lambda i, e: (0, cur(e), i))
    full = pl.BlockSpec((P_HEADS, P_KEYS, tmp), lambda i, e: (0, 0, i))
    mod_map2 = lambda i, e: mod_map(i)
    tile = pl.BlockSpec((tmp, D_MODEL), lambda i, e: (i, 0))
    out = jax.ShapeDtypeStruct((n_tok, D_MODEL), F32)
    final_norm = final_g is not None
    extra_in = [pl.BlockSpec((1, D_MODEL), lambda i, e: (0, 0))] if final_norm else []
    extra_args = [final_g] if final_norm else []
    return pl.pallas_call(
        functools.partial(_peer_expert_kernel, final_norm=final_norm),
        grid=(n_tok // tmp, last + 2),
        in_specs=[tile,
                  pl.BlockSpec((PEER_EB, D_MODEL), lambda i, e: (cur(e), 0)),
                  pl.BlockSpec((D_MODEL, PEER_EB), lambda i, e: (0, jnp.maximum(e - 1, 0))),
                  sel, sel, full, full, tile,
                  pl.BlockSpec((1, 1, 6 * D_MODEL), mod_map2)] + extra_in,
        out_specs=tile,
        out_shape=out,
        scratch_shapes=[pltpu.VMEM((D_MODEL, tmp), F32), pltpu.VMEM((PEER_EB, tmp), BF16),
                        pltpu.VMEM((PEER_EB, tmp), BF16)],
        compiler_params=_cparams("parallel", "arbitrary"),
        name="peer_experts",
    )(h2, u_bf, vt_bf, rk0, e0, th1, e1, x, mod3, *extra_args)


def _block_diag2(w):
    z = jnp.zeros_like(w[0])
    return jnp.concatenate([jnp.concatenate([w[0], z], axis=1),
                            jnp.concatenate([z, w[1]], axis=1)], axis=0)


def kernel(x_prompt, x_sample, state_rwkv, c, c_ctx, ada_w, ada_b, norm1_g, norm2_g, w_in, shift_mu,
           decay_w0, decay_w2, iclr_a0, iclr_a2, gate_g2, k_k, k_a, r_k, gn_g, gn_b, w_a_out,
           pool_w, pool_b, pool_scale, w_b_out, conv_w, conv_b, cln_g, cln_b, w_c_out, w_out,
           peer_wq, peer_keys, peer_u, peer_v, final_g):
    d = D_MODEL
    batch, seq, _ = x_prompt.shape
    dec_batch, dec_seq, _ = x_sample.shape
    depth = ada_w.shape[0]
    n_ctx = batch * seq
    n_dec = dec_batch * dec_seq
    tm = seq
    assert n_ctx % dec_seq == 0 and dec_seq % tm == 0 and dec_seq % GRID_W == 0
    assert n_ctx % TM_PROJ == 0 and dec_seq % TM_PROJ == 0 and dec_seq % TM_PEER == 0
    assert n_ctx % TM_PEER == 0 and tm % GRID_W == 0 and dec_batch <= 7 and TM_PROJ % tm == 0

    x = jnp.concatenate([x_prompt.reshape(n_ctx, d), x_sample.reshape(n_dec, d)], axis=0)
    n_tok = n_ctx + n_dec

    cond8 = jnp.zeros((8, d), F32).at[0].set(c_ctx).at[1:1 + dec_batch].set(c)
    mod_all = _modulation(cond8, ada_w, ada_b)

    seg_ids = jnp.arange(d) // A_HEAD
    ones_bd = (seg_ids[:, None] == seg_ids[None, :]).astype(BF16)

    map_proj = _mod_row_map(n_ctx // TM_PROJ, dec_seq // TM_PROJ)
    map_tm = _mod_row_map(n_ctx // tm, dec_seq // tm)
    map_peer = _mod_row_map(n_ctx // TM_PEER, dec_seq // TM_PEER)

    n_tiles = n_tok // tm
    tile_idx = jnp.arange(n_tiles)
    dec_tile = (tile_idx - n_ctx // tm) % (dec_seq // tm)
    is_start = jnp.where(tile_idx < n_ctx // tm, True, dec_tile == 0)[:, None]
    is_end = jnp.where(tile_idx < n_ctx // tm, True, dec_tile == dec_seq // tm - 1)[:, None]

    ctx_states = []
    for l in range(depth):
        mod3 = mod_all[l][:, None, :]
        w_in_bf = w_in[l].astype(BF16)
        pa, edges = _inproj(x, mod3, norm1_g[l][None], w_in_bf[:, :A_COLS], (A_COLS,), map_proj,
                            "inproj_a", edge_tile=tm)
        pb, pc, pg = _inproj(x, mod3, norm1_g[l][None], w_in_bf[:, A_COLS:],
                             (B_WIDTH, 2 * C_WIDTH, 3 * d), map_proj, "inproj_bcg")

        zero_row = jnp.zeros((1, A_COLS), F32)
        halo_prev = jnp.concatenate([zero_row, edges[:-1, 1, :]], axis=0)
        halo_next = jnp.concatenate([edges[1:, 0, :], zero_row], axis=0)
        halo_prev = jnp.where(is_start, 0.0, halo_prev)[:, None, :]
        halo_next = jnp.where(is_end, 0.0, halo_next)[:, None, :]
        prep_params = (shift_mu[l][None], decay_w0[l].reshape(1, 2 * d),
                       _block_diag2(decay_w2[l]).astype(BF16), iclr_a0[l].reshape(1, 2 * d),
                       _block_diag2(iclr_a2[l]).astype(BF16), gate_g2[l].astype(BF16),
                       k_k[l][None], k_a[l][None], r_k[l].reshape(1, d), ones_bd)
        r, v, kk, dec, kd, bb, bv, g = _rwkv_prep(pa, halo_prev, halo_next, tm, prep_params)

        y_scan = None
        for seq_len, batch0, nb, s0 in ((seq, 0, batch, None),
                                        (dec_seq, n_ctx // dec_seq, dec_batch, state_rwkv[:, l])):
            vs = _lane_plan(nb)[1]
            if s0 is None:
                s0_c = jnp.zeros((2, A_HEAD, A_HEAD // vs, vs * nb * A_HEADS), F32)
            else:
                s0_c = _state_to_chain(s0, vs)
            lay = lambda a, value_rows=False: _to_chain(a, seq_len, batch0, nb, value_rows)
            y_fwd, y_bwd, sfin_c = _rwkv_scan(lay(kk)[0], lay(r)[0], lay(dec), lay(bb), lay(kd),
                                              lay(v, True)[0], s0_c)
            y_scan = _from_chain(y_fwd, y_bwd, y_scan, n_tok, seq_len, batch0, nb)
            if s0 is None:
                ctx_states.append(_state_from_chain(sfin_c, nb, vs))

        yb = _pool_mix(pb, n_ctx, seq, dec_seq, pool_w[l].astype(BF16), pool_b[l].reshape(1, B_WIDTH),
                       pool_scale[l][None])
        yc = _conv_mix(pc, tm, n_ctx // tm, conv_w[l], conv_b[l][None], cln_g[l][None], cln_b[l][None])

        x, h2 = _merge(y_scan, bv, g, yb, yc, pg, x, mod3, map_tm, tm, gn_g[l][None], gn_b[l][None],
                       norm2_g[l][None], w_a_out[l].astype(BF16), w_b_out[l].astype(BF16),
                       w_c_out[l].astype(BF16), w_out[l].astype(BF16), ones_bd)

        wq_t = peer_wq[l].T.astype(BF16)
        keys = peer_keys[l].reshape(2 * P_HEADS, P_KEYS, P_HALF).astype(BF16)
        rk0, e0p, th1, e1p = _peer_scores(h2, wq_t, keys)
        x = _peer_experts(h2, peer_u[l].astype(BF16), peer_v[l].T.astype(BF16), rk0, e0p, th1, e1p,
                          x, mod3, map_peer, final_g[None] if l == depth - 1 else None)

    y = x
    new_state = jnp.stack(ctx_states, axis=1)
    return (y[:n_ctx].reshape(batch, seq, d), y[n_ctx:].reshape(dec_batch, dec_seq, d), new_state)
```
